```python
import math
import jax
import jax.numpy as jnp
from jax import lax
import numpy as np

D_MODEL = 1024
BATCH = 4
SEQ = 4096
DEPTH = 2
DEC_BATCH = 128
DEC_SEQ = 1
PAST_LEN = 2048
PAGE_SIZE = 128

HEAD_DIM = 64
NSA_HEADS = 8
NSA_KV = 2
NSA_GROUP = NSA_HEADS // NSA_KV
CMP_LEN = 32
CMP_STRIDE = 16
CMP_HIDDEN = 128
SLC_BLOCK = 64
N_SEL = 16
WINDOW = 512
Q_BLOCK = 128
DN_HEADS = 4
DN_DK = 128
DN_DV = 128
DN_CHUNK = 64
CONV_W = 4
HG_HEADS = 4
HG_DK = 128
HG_DV = 128
HG_CHUNK = 16
D_FF = 2816
NORM_EPS = 1e-6
L2_EPS = 1e-6
NEG_BIG = -1e30
SEL_BIG = 1e9

NSA_W = NSA_HEADS * HEAD_DIM
KV_W = NSA_KV * HEAD_DIM
DN_QK = DN_HEADS * DN_DK
DN_V = DN_HEADS * DN_DV
DN_QKV = 2 * DN_QK + DN_V
HG_WK = HG_HEADS * HG_DK
HG_WV = HG_HEADS * HG_DV
MIX_W = NSA_W + DN_V + HG_WV
IN_SPLITS = (NSA_W, 6 * KV_W, 3 * NSA_HEADS, DN_QKV, DN_HEADS, DN_HEADS, DN_V, HG_WK, HG_WV, HG_WK, HG_WV, 3 * D_MODEL)
IN_COLS = sum(IN_SPLITS)

kernel_name = 'nsa_deltanet_hgrn2_hybrid_step'


def rms_norm(x, g):
    xf = x.astype(jnp.float32)
    y = xf * lax.rsqrt(jnp.mean(xf * xf, axis=-1, keepdims=True) + NORM_EPS)
    return (y * g.astype(jnp.float32)).astype(x.dtype)


def l2_normalize(x):
    xf = x.astype(jnp.float32)
    return xf * lax.rsqrt(jnp.sum(xf * xf, axis=-1, keepdims=True) + L2_EPS)


def swiglu(x, w_gu, w_down):
    g, u = jnp.split(x @ w_gu, 2, axis=-1)
    return (jax.nn.silu(g) * u) @ w_down


def alibi_slopes(n):
    return 2.0 ** (-8.0 * (jnp.arange(n, dtype=jnp.float32) + 1.0) / n)


def masked_softmax(s, mask):
    s = jnp.where(mask, s, NEG_BIG)
    m = jnp.max(s, axis=-1, keepdims=True)
    e = jnp.where(mask, jnp.exp(s - m), 0.0)
    return e / jnp.maximum(jnp.sum(e, axis=-1, keepdims=True), 1e-30)


def masked_exp(diff, mask):
    return jnp.where(mask, jnp.exp(jnp.where(mask, diff, 0.0)), 0.0)


def split_in(z):
    offs = [int(o) for o in np.cumsum(IN_SPLITS)[:-1]]
    return jnp.split(z, offs, axis=-1)


def gather_pages(pool, page_table):
    rows = pool[page_table]
    return rows.reshape(rows.shape[0], rows.shape[1] * rows.shape[2], *rows.shape[3:])


def nsa_compress(rows, pe, w1, w2, k_gain):
    B, T = rows.shape[:2]
    pad = (-T) % CMP_STRIDE
    rows = jnp.pad(rows, ((0, 0), (0, pad), (0, 0), (0, 0), (0, 0)))
    n_chunk = (T + pad) // CMP_STRIDE
    ch = rows.reshape(B, n_chunk, CMP_STRIDE, 2, NSA_KV, HEAD_DIM)
    n_part = CMP_LEN // CMP_STRIDE
    n_blk = n_chunk - n_part + 1
    hid = 0.0
    for m in range(n_part):
        pe_m = pe[:, m * CMP_STRIDE:(m + 1) * CMP_STRIDE].transpose(1, 0, 2)[:, :, None, :]
        w1_m = w1[:, m * CMP_STRIDE:(m + 1) * CMP_STRIDE]
        hid = hid + jnp.einsum('bcrkgd,krdh->bckgh', ch + pe_m, w1_m)[:, m:m + n_blk]
    comp = jnp.einsum('bnkgh,khd->bnkgd', jax.nn.silu(hid), w2)
    ck = rms_norm(comp[:, :, 0], k_gain)
    cv = comp[:, :, 1]
    c_end = jnp.arange(n_blk) * CMP_STRIDE + CMP_LEN - 1
    return ck, cv, c_end


def nsa_attend(q, gates, q_pos, ck, cv, c_end, sk, sv, wk, wv, w_pos, slopes):
    B, Tq = q.shape[:2]
    f32 = jnp.float32
    qg = q.reshape(B, Tq, NSA_KV, NSA_GROUP, HEAD_DIM)
    m_h = slopes.reshape(NSA_KV, NSA_GROUP)[:, :, None]
    dist_c = q_pos[:, None] - c_end[None, :]
    s = jnp.einsum('btgjd,bngd->btgjn', qg, ck).astype(f32)
    s = s - m_h * jnp.abs(dist_c)[:, None, None, :].astype(f32)
    p_cmp = masked_softmax(s, (dist_c >= 0)[:, None, None, :])
    o_cmp = jnp.einsum('btgjn,bngd->btgjd', p_cmp.astype(cv.dtype), cv)
    NB = sk.shape[2]
    ci = jnp.arange(ck.shape[1])[:, None] * CMP_STRIDE
    bj = jnp.arange(NB)
    overlap = ((ci < (bj + 1) * SLC_BLOCK) & (ci + CMP_LEN > bj * SLC_BLOCK)).astype(f32)
    imp = jnp.einsum('btgjn,nm->btgm', p_cmp, overlap)
    cur = (q_pos // SLC_BLOCK)[:, None]
    causal_blk = (bj * SLC_BLOCK)[None, :] <= q_pos[:, None]
    forced = (bj == 0)[None, :] | (bj[None, :] == cur) | (bj[None, :] == cur - 1)
    imp = jnp.where(forced[:, None, :], SEL_BIG, jnp.where(causal_blk[:, None, :], imp, -SEL_BIG))
    _, sel = lax.top_k(imp, min(N_SEL, NB))
    bi = jnp.arange(B)[:, None, None, None]
    gi = jnp.arange(NSA_KV)[None, None, :, None]
    k_sel = sk[bi, gi, sel]
    v_sel = sv[bi, gi, sel]
    pos = sel[..., None] * SLC_BLOCK + jnp.arange(SLC_BLOCK)
    dist_s = (q_pos[None, :, None, None, None] - pos).reshape(B, Tq, NSA_KV, 1, -1)
    s = jnp.einsum('btgjd,btgkld->btgjkl', qg, k_sel).astype(f32).reshape(B, Tq, NSA_KV, NSA_GROUP, -1)
    s = s - m_h * jnp.abs(dist_s).astype(f32)
    p = masked_softmax(s, dist_s >= 0)
    o_slc = jnp.einsum('btgjn,btgnd->btgjd', p.astype(v_sel.dtype), v_sel.reshape(B, Tq, NSA_KV, -1, HEAD_DIM))
    dist_w = q_pos[:, None] - w_pos[None, :]
    mask_w = (dist_w >= 0) & (dist_w < WINDOW) & (w_pos >= 0)[None, :]
    s = jnp.einsum('btgjd,bsgd->btgjs', qg, wk).astype(f32) - m_h * jnp.abs(dist_w)[:, None, None, :].astype(f32)
    p = masked_softmax(s, mask_w[:, None, None, :])
    o_win = jnp.einsum('btgjs,bsgd->btgjd', p.astype(wv.dtype), wv)
    g = gates.reshape(B, Tq, NSA_KV, NSA_GROUP, 3)
    o = g[..., 0:1] * o_cmp + g[..., 1:2] * o_slc + g[..., 2:3] * o_win
    return o.reshape(B, Tq, NSA_W)


def nsa_mixer(q, gates, cmp_rows, slc_rows, win_rows, q_pos0, win_pos0, pe, w1, w2, k_gain_cmp, blocked):
    B, T = q.shape[:2]
    slopes = alibi_slopes(NSA_HEADS)
    ck, cv, c_end = nsa_compress(cmp_rows, pe, w1, w2, k_gain_cmp)
    Tk = slc_rows.shape[1]
    pad = (-Tk) % SLC_BLOCK
    nb = (Tk + pad) // SLC_BLOCK
    blk = jnp.pad(slc_rows, ((0, 0), (0, pad), (0, 0), (0, 0), (0, 0))).reshape(B, nb, SLC_BLOCK, 2, NSA_KV, HEAD_DIM)
    sk = blk[:, :, :, 0].transpose(0, 3, 1, 2, 4)
    sv = blk[:, :, :, 1].transpose(0, 3, 1, 2, 4)
    if not blocked:
        Tw = win_rows.shape[1]
        return nsa_attend(q, gates, q_pos0 + jnp.arange(T), ck, cv, c_end, sk, sv,
                          win_rows[:, :, 0], win_rows[:, :, 1], win_pos0 + jnp.arange(Tw), slopes)
    win_p = jnp.pad(win_rows, ((0, 0), (WINDOW, 0), (0, 0), (0, 0), (0, 0)))

    def one_block(b0):
        qb = lax.dynamic_slice_in_dim(q, b0, Q_BLOCK, 1)
        gb = lax.dynamic_slice_in_dim(gates, b0, Q_BLOCK, 1)
        wb = lax.dynamic_slice_in_dim(win_p, b0, WINDOW + Q_BLOCK, 1)
        return nsa_attend(qb, gb, q_pos0 + b0 + jnp.arange(Q_BLOCK), ck, cv, c_end, sk, sv,
                          wb[:, :, 0], wb[:, :, 1], win_pos0 - WINDOW + b0 + jnp.arange(WINDOW + Q_BLOCK), slopes)

    out = lax.map(one_block, jnp.arange(T // Q_BLOCK) * Q_BLOCK)
    return out.transpose(1, 0, 2, 3).reshape(B, T, NSA_W)


def to_chunks(a, C):
    B, T = a.shape[:2]
    pad = (-T) % C
    a = jnp.pad(a, [(0, 0), (0, pad)] + [(0, 0)] * (a.ndim - 2))
    a = a.reshape(B, (T + pad) // C, C, *a.shape[2:])
    return a.transpose((1, 0, 3, 2) + tuple(range(4, a.ndim)))


def from_chunks(o, T):
    n, B, H, C, d = o.shape
    return o.transpose(1, 0, 3, 2, 4).reshape(B, n * C, H, d)[:, :T]


def causal_conv(x_new, buf, w):
    T = x_new.shape[1]
    xx = jnp.concatenate([buf.astype(x_new.dtype), x_new], axis=1)
    y = sum(xx[:, j:j + T] * w[j] for j in range(CONV_W))
    return y, xx[:, -(CONV_W - 1):]


def gated_delta_rule(q, k, v, g, beta, S0):
    T = q.shape[1]
    C = DN_CHUNK
    f32 = jnp.float32
    xs = tuple(to_chunks(a.astype(f32), C) for a in (q, k, v, g, beta))
    incl = jnp.tril(jnp.ones((C, C), bool))
    strict = jnp.tril(jnp.ones((C, C), bool), -1)
    eye = jnp.eye(C, dtype=f32)

    def step(S, inp):
        qc, kc, vc, gc, bc = inp
        gcum = jnp.cumsum(gc, axis=-1)
        decay = masked_exp(gcum[..., :, None] - gcum[..., None, :], incl)
        L = jnp.where(strict, bc[..., :, None] * jnp.einsum('bhid,bhjd->bhij', kc, kc) * decay, 0.0)
        Tinv = lax.linalg.triangular_solve(eye + L, jnp.broadcast_to(eye, L.shape), left_side=True, lower=True, unit_diagonal=True)
        u = Tinv @ (vc * bc[..., None])
        w = Tinv @ (kc * (bc * jnp.exp(gcum))[..., None])
        v_new = u - w @ S
        attn = jnp.einsum('bhid,bhjd->bhij', qc, kc) * decay
        o = (qc * jnp.exp(gcum)[..., None]) @ S + attn @ v_new
        g_last = gcum[..., -1:]
        S = S * jnp.exp(g_last)[..., None] + jnp.einsum('bhcd,bhce->bhde', kc * jnp.exp(g_last - gcum)[..., None], v_new)
        return S, o

    S, o = lax.scan(step, S0.astype(f32), xs)
    return from_chunks(o, T), S


def hgrn2_recurrence(q, k, v, logf, S0):
    T = q.shape[1]
    C = HG_CHUNK
    f32 = jnp.float32
    xs = tuple(to_chunks(a.astype(f32), C) for a in (q, k, v, logf))
    incl = jnp.tril(jnp.ones((C, C), bool))[:, :, None]

    def step(S, inp):
        qc, kc, vc, lc = inp
        G = jnp.cumsum(lc, axis=2)
        dec = masked_exp(G[:, :, :, None, :] - G[:, :, None, :, :], incl)
        A = jnp.einsum('bhtc,bhsc,bhtsc->bhts', qc, kc, dec)
        o = jnp.einsum('bhtc,bhcd->bhtd', qc * jnp.exp(G), S) + A @ vc
        G_last = G[:, :, -1:]
        S = S * jnp.exp(G_last[:, :, 0])[..., None] + jnp.einsum('bhsc,bhsd->bhcd', kc * jnp.exp(G_last - G), vc)
        return S, o

    S, o = lax.scan(step, S0.astype(f32), xs)
    return from_chunks(o, T), S


def hgrn_lower_bounds(logits):
    p = jax.nn.softmax(logits.astype(jnp.float32), axis=0)
    return jnp.cumsum(p, axis=0) - p[0]


def decoder_layer(x, l, prm, past, pos0, blocked):
    B, T, _ = x.shape
    f32 = jnp.float32
    x = x + 0.5 * swiglu(rms_norm(x, prm['ffn1_norm'][l]), prm['ffn1_w_gu'][l], prm['ffn1_w_down'][l])
    h = rms_norm(x, prm['mix_norm'][l])
    (a_q, a_kv, a_g, d_qkv, d_b, d_a, d_z, r_f, r_i, r_q, r_og, m_g) = split_in(h @ prm['w_in'][l])

    q = rms_norm(a_q.reshape(B, T, NSA_HEADS, HEAD_DIM), prm['nsa_q_norm'][l]) * (HEAD_DIM ** -0.5)
    kv = a_kv.reshape(B, T, 3, 2, NSA_KV, HEAD_DIM)
    kg = prm['nsa_k_norm'][l]
    cmp_new = kv[:, :, 0]
    slc_new = jnp.stack([rms_norm(kv[:, :, 1, 0], kg[1]), kv[:, :, 1, 1]], axis=2)
    win_new = jnp.stack([rms_norm(kv[:, :, 2, 0], kg[2]), kv[:, :, 2, 1]], axis=2)
    gates = jax.nn.sigmoid(a_g.reshape(B, T, NSA_HEADS, 3))
    if past is None:
        cmp_rows, slc_rows, win_rows, win_pos0 = cmp_new, slc_new, win_new, pos0
        win_state = win_new[:, T - min(WINDOW, T):]
    else:
        cmp_rows = jnp.concatenate([past['cmp'], cmp_new], axis=1)
        slc_rows = jnp.concatenate([past['slc'], slc_new], axis=1)
        win_rows = jnp.concatenate([past['win'], win_new], axis=1)
        win_pos0 = pos0 - past['win'].shape[1]
        win_state = win_new
    o_a = nsa_mixer(q, gates, cmp_rows, slc_rows, win_rows, pos0, win_pos0, prm['nsa_cmp_pe'][l],
                    prm['nsa_cmp_w1'][l], prm['nsa_cmp_w2'][l], kg[0], blocked)

    conv_buf = jnp.zeros((B, CONV_W - 1, DN_QKV), x.dtype) if past is None else past['conv']
    qkv, conv_state = causal_conv(d_qkv, conv_buf, prm['dn_conv_w'][l])
    dq, dk, dv = jnp.split(jax.nn.silu(qkv), [DN_QK, 2 * DN_QK], axis=-1)
    dq = l2_normalize(dq.reshape(B, T, DN_HEADS, DN_DK)) * (DN_DK ** -0.5)
    dk = l2_normalize(dk.reshape(B, T, DN_HEADS, DN_DK))
    dv = dv.reshape(B, T, DN_HEADS, DN_DV)
    beta = jax.nn.sigmoid(d_b.astype(f32))
    g = -jnp.exp(prm['dn_A_log'][l].astype(f32)) * jax.nn.softplus(d_a.astype(f32) + prm['dn_dt_bias'][l].astype(f32))
    S0 = jnp.zeros((B, DN_HEADS, DN_DK, DN_DV), f32) if past is None else past['dn_S']
    o_d, dn_state = gated_delta_rule(dq, dk, dv, g, beta, S0)
    o_d = rms_norm(o_d.astype(x.dtype), prm['dn_out_norm'][l]) * jax.nn.silu(d_z.reshape(B, T, DN_HEADS, DN_DV))

    lb = hgrn_lower_bounds(prm['hg_lb_logits'])[l]
    zf = r_f.astype(f32)
    logf = jnp.log(lb + (1.0 - lb) * jax.nn.sigmoid(zf))
    k_in = (1.0 - lb) * jax.nn.sigmoid(-zf)
    hq = jax.nn.silu(r_q.reshape(B, T, HG_HEADS, HG_DK)) * (HG_DK ** -0.5)
    H0 = jnp.zeros((B, HG_HEADS, HG_DK, HG_DV), f32) if past is None else past['hg_S']
    o_h, hg_state = hgrn2_recurrence(hq, k_in.reshape(B, T, HG_HEADS, HG_DK), r_i.reshape(B, T, HG_HEADS, HG_DV),
                                     logf.reshape(B, T, HG_HEADS, HG_DK), H0)
    o_h = rms_norm(o_h.astype(x.dtype), prm['hg_out_norm'][l]) * jax.nn.sigmoid(r_og.reshape(B, T, HG_HEADS, HG_DV))

    wb = prm['w_branch'][l]
    mg = jax.nn.sigmoid(m_g.reshape(B, T, 3, D_MODEL))
    branches = (o_a, o_d.reshape(B, T, DN_V), o_h.reshape(B, T, HG_WV))
    offs = (0, NSA_W, NSA_W + DN_V, MIX_W)
    merged = sum(mg[:, :, i] * (branches[i] @ wb[offs[i]:offs[i + 1]]) for i in range(3))
    x = x + merged @ prm['w_out'][l]
    x = x + 0.5 * swiglu(rms_norm(x, prm['ffn2_norm'][l]), prm['ffn2_w_gu'][l], prm['ffn2_w_down'][l])
    return x, (cmp_new, slc_new, win_state, conv_state, dn_state.astype(x.dtype), hg_state.astype(x.dtype))


def run_trunk(x, prm, caches, page_table, pos0, blocked):
    new = []
    for l in range(DEPTH):
        if caches is None:
            past = None
        else:
            c_cmp, c_slc, c_win, s_conv, s_dn, s_hg = caches
            past = {'cmp': gather_pages(c_cmp[l], page_table), 'slc': gather_pages(c_slc[l], page_table),
                    'win': c_win[l], 'conv': s_conv[l], 'dn_S': s_dn[l], 'hg_S': s_hg[l]}
        x, st = decoder_layer(x, l, prm, past, pos0, blocked)
        new.append(st)
    stacked = [jnp.stack([s[i] for s in new], axis=0) for i in range(6)]
    return x, stacked


def setup_inputs(seed: int = 0) -> dict:
    key = jax.random.key(seed)
    keys = iter(jax.random.split(key, 48))

    def nrm(shape, scale=1.0):
        return jax.random.normal(next(keys), shape, jnp.float32) * scale

    def gain(shape):
        return 1.0 + 0.05 * jax.random.normal(next(keys), shape, jnp.float32)

    n_pages = PAST_LEN // PAGE_SIZE
    n_used = DEC_BATCH * n_pages
    n_pool = n_used + n_used // 4
    win_buf = min(WINDOW, PAST_LEN)
    x_prompt = nrm((BATCH, SEQ, D_MODEL))
    x_sample = nrm((DEC_BATCH, DEC_SEQ, D_MODEL))
    cache_cmp_kv = nrm((DEPTH, n_pool, PAGE_SIZE, 2, NSA_KV, HEAD_DIM))
    cache_slc_kv = nrm((DEPTH, n_pool, PAGE_SIZE, 2, NSA_KV, HEAD_DIM))
    cache_win_kv = nrm((DEPTH, DEC_BATCH, win_buf, 2, NSA_KV, HEAD_DIM))
    state_dn_conv = nrm((DEPTH, DEC_BATCH, CONV_W - 1, DN_QKV))
    state_dn_S = nrm((DEPTH, DEC_BATCH, DN_HEADS, DN_DK, DN_DV), 0.1)
    state_hg_S = nrm((DEPTH, DEC_BATCH, HG_HEADS, HG_DK, HG_DV), 0.3)
    page_table = jax.random.permutation(next(keys), n_pool)[:n_used].reshape(DEC_BATCH, n_pages).astype(jnp.int32)
    ffn1_norm = gain((DEPTH, D_MODEL))
    ffn1_w_gu = nrm((DEPTH, D_MODEL, 2 * D_FF), D_MODEL ** -0.5)
    ffn1_w_down = nrm((DEPTH, D_FF, D_MODEL), D_FF ** -0.5)
    mix_norm = gain((DEPTH, D_MODEL))
    w_in = nrm((DEPTH, D_MODEL, IN_COLS), D_MODEL ** -0.5)
    nsa_q_norm = gain((DEPTH, HEAD_DIM))
    nsa_k_norm = gain((DEPTH, 3, HEAD_DIM))
    nsa_cmp_pe = nrm((DEPTH, 2, CMP_LEN, HEAD_DIM), 0.1)
    nsa_cmp_w1 = nrm((DEPTH, 2, CMP_LEN, HEAD_DIM, CMP_HIDDEN), (CMP_LEN * HEAD_DIM) ** -0.5)
    nsa_cmp_w2 = nrm((DEPTH, 2, CMP_HIDDEN, HEAD_DIM), CMP_HIDDEN ** -0.5)
    dn_conv_w = nrm((DEPTH, CONV_W, DN_QKV), CONV_W ** -0.5)
    dn_A_log = jnp.log(jax.random.uniform(next(keys), (DEPTH, DN_HEADS), jnp.float32, 1.0, 16.0))
    dt = jnp.exp(jax.random.uniform(next(keys), (DEPTH, DN_HEADS), jnp.float32, math.log(1e-3), math.log(1e-1)))
    dn_dt_bias = dt + jnp.log(-jnp.expm1(-dt))
    dn_out_norm = gain((DEPTH, DN_DV))
    hg_lb_logits = nrm((DEPTH, HG_WK))
    hg_out_norm = gain((DEPTH, HG_DV))
    w_branch = nrm((DEPTH, MIX_W, D_MODEL), NSA_W ** -0.5)
    w_out = nrm((DEPTH, D_MODEL, D_MODEL), D_MODEL ** -0.5)
    ffn2_norm = gain((DEPTH, D_MODEL))
    ffn2_w_gu = nrm((DEPTH, D_MODEL, 2 * D_FF), D_MODEL ** -0.5)
    ffn2_w_down = nrm((DEPTH, D_FF, D_MODEL), D_FF ** -0.5)
    return {'x_prompt': x_prompt, 'x_sample': x_sample, 'cache_cmp_kv': cache_cmp_kv, 'cache_slc_kv': cache_slc_kv,
            'cache_win_kv': cache_win_kv, 'state_dn_conv': state_dn_conv, 'state_dn_S': state_dn_S,
            'state_hg_S': state_hg_S, 'page_table': page_table, 'ffn1_norm': ffn1_norm, 'ffn1_w_gu': ffn1_w_gu,
            'ffn1_w_down': ffn1_w_down, 'mix_norm': mix_norm, 'w_in': w_in, 'nsa_q_norm': nsa_q_norm,
            'nsa_k_norm': nsa_k_norm, 'nsa_cmp_pe': nsa_cmp_pe, 'nsa_cmp_w1': nsa_cmp_w1, 'nsa_cmp_w2': nsa_cmp_w2,
            'dn_conv_w': dn_conv_w, 'dn_A_log': dn_A_log, 'dn_dt_bias': dn_dt_bias, 'dn_out_norm': dn_out_norm,
            'hg_lb_logits': hg_lb_logits, 'hg_out_norm': hg_out_norm, 'w_branch': w_branch, 'w_out': w_out,
            'ffn2_norm': ffn2_norm, 'ffn2_w_gu': ffn2_w_gu, 'ffn2_w_down': ffn2_w_down}


def reference(x_prompt, x_sample, cache_cmp_kv, cache_slc_kv, cache_win_kv, state_dn_conv, state_dn_S, state_hg_S,
              page_table, ffn1_norm, ffn1_w_gu, ffn1_w_down, mix_norm, w_in, nsa_q_norm, nsa_k_norm, nsa_cmp_pe,
              nsa_cmp_w1, nsa_cmp_w2, dn_conv_w, dn_A_log, dn_dt_bias, dn_out_norm, hg_lb_logits, hg_out_norm,
              w_branch, w_out, ffn2_norm, ffn2_w_gu, ffn2_w_down):
    prm = {'ffn1_norm': ffn1_norm, 'ffn1_w_gu': ffn1_w_gu, 'ffn1_w_down': ffn1_w_down, 'mix_norm': mix_norm,
           'w_in': w_in, 'nsa_q_norm': nsa_q_norm, 'nsa_k_norm': nsa_k_norm, 'nsa_cmp_pe': nsa_cmp_pe,
           'nsa_cmp_w1': nsa_cmp_w1, 'nsa_cmp_w2': nsa_cmp_w2, 'dn_conv_w': dn_conv_w, 'dn_A_log': dn_A_log,
           'dn_dt_bias': dn_dt_bias, 'dn_out_norm': dn_out_norm, 'hg_lb_logits': hg_lb_logits,
           'hg_out_norm': hg_out_norm, 'w_branch': w_branch, 'w_out': w_out, 'ffn2_norm': ffn2_norm,
           'ffn2_w_gu': ffn2_w_gu, 'ffn2_w_down': ffn2_w_down}
    y_prompt, (p_cmp, p_slc, p_win, p_conv, p_dn, p_hg) = run_trunk(x_prompt, prm, None, None, 0, True)
    caches = (cache_cmp_kv, cache_slc_kv, cache_win_kv, state_dn_conv, state_dn_S, state_hg_S)
    y_sample, (s_cmp, s_slc, s_win, s_conv, s_dn, s_hg) = run_trunk(x_sample, prm, caches, page_table, PAST_LEN, False)
    return (y_prompt, y_sample, p_cmp, s_cmp, p_slc, s_slc, p_win, s_win, p_conv, s_conv, p_dn, s_dn, p_hg, s_hg)
```

```python
import functools

import jax
import jax.numpy as jnp
import numpy as np
from jax import lax
from jax.experimental import pallas as pl
from jax.experimental.pallas import tpu as pltpu

F32 = jnp.float32
BF16 = jnp.bfloat16

D_MODEL = 1024
DEPTH = 2
PAST_LEN = 2048
PAGE_SIZE = 128
HEAD_DIM = 64
NSA_HEADS = 8
NSA_KV = 2
NSA_GROUP = NSA_HEADS // NSA_KV
CMP_LEN = 32
CMP_STRIDE = 16
CMP_HIDDEN = 128
SLC_BLOCK = 64
N_SEL = 16
WINDOW = 512
Q_BLOCK = 128
DN_HEADS = 4
DN_DK = 128
DN_DV = 128
DN_CHUNK = 64
CONV_W = 4
HG_HEADS = 4
HG_DK = 128
HG_DV = 128
HG_CHUNK = 16
D_FF = 2816
NORM_EPS = 1e-6
L2_EPS = 1e-6
NEG_BIG = -1e30
SEL_BIG = 1e9

NSA_W = NSA_HEADS * HEAD_DIM
KV_W = NSA_KV * HEAD_DIM
ROW_W = 2 * KV_W
DN_QK = DN_HEADS * DN_DK
DN_V = DN_HEADS * DN_DV
DN_QKV = 2 * DN_QK + DN_V
HG_WK = HG_HEADS * HG_DK
HG_WV = HG_HEADS * HG_DV
MIX_W = NSA_W + DN_V + HG_WV
IN_SPLITS = (NSA_W, 6 * KV_W, 3 * NSA_HEADS, DN_QKV, DN_HEADS, DN_HEADS, DN_V, HG_WK, HG_WV, HG_WK, HG_WV, 3 * D_MODEL)
IN_NAMES = ("a_q", "a_kv", "a_g", "d_qkv", "d_b", "d_a", "d_z", "r_f", "r_i", "r_q", "r_og", "m_g")
Z_ORDER = ("m_g", "a_q", "a_kv", "d_qkv", "d_z", "r_f", "r_i", "r_q", "r_og", "a_g", "d_b", "d_a")
Z_TN = 512
CHUNK_W = CMP_STRIDE * ROW_W
CMP_HID_W = 2 * NSA_KV * CMP_HIDDEN

V7X_VMEM_BYTES = 64 * 1024 * 1024
LANE = 128


def _cparams(sem, vmem_mb):
    assert vmem_mb * 1024 * 1024 < V7X_VMEM_BYTES
    return pltpu.CompilerParams(dimension_semantics=sem, vmem_limit_bytes=vmem_mb * 1024 * 1024)


def _dot(a, b):
    return jnp.dot(a, b, preferred_element_type=F32)


def _dot_nt(a, b):
    return lax.dot_general(a, b, (((1,), (1,)), ((), ())), preferred_element_type=F32)


def _dot_tn(a, b):
    return lax.dot_general(a, b, (((0,), (0,)), ((), ())), preferred_element_type=F32)


def _split2(a):
    hi = a.astype(BF16)
    lo = (a - hi.astype(F32)).astype(BF16)
    return hi, lo


def _dot3(a, b):
    ah, al = _split2(a)
    bh, bl = _split2(b)
    return _dot(ah, bh) + (_dot(ah, bl) + _dot(al, bh))


def _rms(x, g):
    return x * lax.rsqrt(jnp.mean(x * x, axis=-1, keepdims=True) + NORM_EPS) * g


def _silu(x):
    return x * jax.nn.sigmoid(x)


def _iota(shape, dim):
    return lax.broadcasted_iota(jnp.int32, shape, dim)


def _alibi_slope(head_in_group, g):
    out = jnp.full(head_in_group.shape, 2.0 ** -(NSA_GROUP * g + NSA_GROUP), F32)
    for j in range(NSA_GROUP - 2, -1, -1):
        out = jnp.where(head_in_group == j, 2.0 ** -(NSA_GROUP * g + j + 1), out)
    return out


def _ffn_kernel(x_ref, g_ref, wg_ref, wu_ref, wd_ref, o_ref, xn_ref, acc_ref, *, nf):
    j = pl.program_id(1)

    @pl.when(j == 0)
    def _():
        xn_ref[...] = _rms(x_ref[...], g_ref[...]).astype(BF16)
        acc_ref[...] = jnp.zeros_like(acc_ref)

    xn = xn_ref[...]
    a = _silu(_dot(xn, wg_ref[...])) * _dot(xn, wu_ref[...])
    acc_ref[...] += _dot(a.astype(BF16), wd_ref[...])

    @pl.when(j == nf - 1)
    def _():
        o_ref[...] = x_ref[...] + 0.5 * acc_ref[...]


def _ffn(x, gain, w_gu, w_down, tm, tf=256):
    M, D = x.shape
    F = w_down.shape[0]
    nf = F // tf
    return pl.pallas_call(
        functools.partial(_ffn_kernel, nf=nf),
        grid=(M // tm, nf),
        in_specs=[
            pl.BlockSpec((tm, D), lambda i, j: (i, 0)),
            pl.BlockSpec((1, D), lambda i, j: (0, 0)),
            pl.BlockSpec((D, tf), lambda i, j: (0, j)),
            pl.BlockSpec((D, tf), lambda i, j: (0, j + nf)),
            pl.BlockSpec((tf, D), lambda i, j: (j, 0)),
        ],
        out_specs=pl.BlockSpec((tm, D), lambda i, j: (i, 0)),
        out_shape=jax.ShapeDtypeStruct((M, D), F32),
        scratch_shapes=[pltpu.VMEM((tm, D), BF16), pltpu.VMEM((tm, D), F32)],
        compiler_params=_cparams(("parallel", "arbitrary"), 40),
        name="ffn",
    )(x, gain.reshape(1, D), w_gu, w_gu, w_down)


def _inproj_kernel(x_ref, g_ref, w_ref, z_ref, xn_ref):
    @pl.when(pl.program_id(1) == 0)
    def _():
        xn_ref[...] = _rms(x_ref[...], g_ref[...]).astype(BF16)

    z_ref[...] = _dot(xn_ref[...], w_ref[...])


def _inproj(x, gain, w, tm):
    M, D = x.shape
    N = w.shape[1]
    return pl.pallas_call(
        _inproj_kernel,
        grid=(M // tm, N // Z_TN),
        in_specs=[
            pl.BlockSpec((tm, D), lambda i, j: (i, 0)),
            pl.BlockSpec((1, D), lambda i, j: (0, 0)),
            pl.BlockSpec((D, Z_TN), lambda i, j: (0, j)),
        ],
        out_specs=pl.BlockSpec((tm, Z_TN), lambda i, j: (i, j)),
        out_shape=jax.ShapeDtypeStruct((M, N), F32),
        scratch_shapes=[pltpu.VMEM((tm, D), BF16)],
        compiler_params=_cparams(("parallel", "arbitrary"), 40),
        name="inproj",
    )(x, gain.reshape(1, D), w)


def _z_layout():
    sizes = dict(zip(IN_NAMES, IN_SPLITS))
    src = dict(zip(IN_NAMES, np.cumsum((0,) + IN_SPLITS[:-1]).tolist()))
    offs, o = {}, 0
    for n in Z_ORDER:
        offs[n] = o
        o += sizes[n]
    total = -(-o // Z_TN) * Z_TN
    return sizes, src, offs, total


def _permute_w_in(w_in):
    sizes, src, _, total = _z_layout()
    cols = [w_in[:, src[n]:src[n] + sizes[n]] for n in Z_ORDER]
    w = jnp.concatenate(cols, axis=1)
    return jnp.pad(w, ((0, 0), (0, total - w.shape[1]))).astype(BF16)


def _merge_kernel(x_ref, oa_ref, od_ref, oh_ref, g0_ref, g1_ref, g2_ref, wb_ref, wo_ref, o_ref):
    m = jax.nn.sigmoid(g0_ref[...]) * _dot(oa_ref[...].astype(BF16), wb_ref[0:NSA_W, :])
    m += jax.nn.sigmoid(g1_ref[...]) * _dot(od_ref[...].astype(BF16), wb_ref[NSA_W:NSA_W + DN_V, :])
    m += jax.nn.sigmoid(g2_ref[...]) * _dot(oh_ref[...].astype(BF16), wb_ref[NSA_W + DN_V:MIX_W, :])
    o_ref[...] = x_ref[...] + _dot(m.astype(BF16), wo_ref[...])


def _merge(x, o_a, o_d, o_h, z, w_branch, w_out, tm):
    M, D = x.shape
    row = lambda w: pl.BlockSpec((tm, w), lambda i: (i, 0))
    return pl.pallas_call(
        _merge_kernel,
        grid=(M // tm,),
        in_specs=[
            row(D), row(NSA_W), row(DN_V), row(HG_WV),
            pl.BlockSpec((tm, D), lambda i: (i, 0)),
            pl.BlockSpec((tm, D), lambda i: (i, 1)),
            pl.BlockSpec((tm, D), lambda i: (i, 2)),
            pl.BlockSpec((MIX_W, D), lambda i: (0, 0)),
            pl.BlockSpec((D, D), lambda i: (0, 0)),
        ],
        out_specs=row(D),
        out_shape=jax.ShapeDtypeStruct((M, D), F32),
        compiler_params=_cparams(("parallel",), 40),
        name="merge",
    )(x, o_a, o_d, o_h, z, z, z, w_branch, w_out)


def _cmp_weights(pe, w1, w2):
    n_part = CMP_LEN // CMP_STRIDE
    eye = jnp.eye(NSA_KV, dtype=F32)
    eye2 = jnp.eye(2, dtype=F32)
    w1r = w1.reshape(2, n_part, CMP_STRIDE, HEAD_DIM, CMP_HIDDEN)
    w1e = jnp.einsum("kmrdh,kK,gG->mrkgdKGh", w1r, eye2, eye).reshape(n_part, CHUNK_W, CMP_HID_W)
    w2e = jnp.einsum("khd,kK,gG->kghKGd", w2, eye2, eye).reshape(CMP_HID_W, ROW_W)
    per = pe.reshape(2, n_part, CMP_STRIDE, HEAD_DIM).transpose(1, 2, 0, 3)
    pee = jnp.broadcast_to(per[:, :, :, None, :], (n_part, CMP_STRIDE, 2, NSA_KV, HEAD_DIM)).reshape(n_part, CHUNK_W)
    return pee, w1e.astype(BF16), w2e.astype(BF16)


def _compress(x, xnext_row, pe_ref, w1_ref, w2_ref, kg_ref):
    n = x.shape[0]
    p0 = _dot((x + pe_ref[0:1, :]).astype(BF16), w1_ref[0])
    p1 = _dot((x + pe_ref[1:2, :]).astype(BF16), w1_ref[1])
    p1s = pltpu.roll(p1, n - 1, 0)
    if xnext_row is not None:
        p1n = _dot((xnext_row + pe_ref[1:2, :]).astype(BF16), w1_ref[1])
        p1s = jnp.where(_iota((n, 1), 0) == n - 1, p1n[0:1, :], p1s)
    comp = _dot(_silu(p0 + p1s).astype(BF16), w2_ref[...])
    cks = []
    for g in range(NSA_KV):
        kc = comp[:, g * HEAD_DIM:(g + 1) * HEAD_DIM]
        cks.append(_rms(kc, kg_ref[...]))
    return cks, comp[:, KV_W:]


def _cmp_prompt_kernel(x_ref, pe_ref, w1_ref, w2_ref, kg_ref, ck_ref, cvt_ref):
    cks, cv = _compress(x_ref[0], None, pe_ref, w1_ref, w2_ref, kg_ref)
    for g in range(NSA_KV):
        ck_ref[0, g] = cks[g].astype(BF16)
    cvt = cv.T
    for g in range(NSA_KV):
        cvt_ref[0, g] = cvt[g * HEAD_DIM:(g + 1) * HEAD_DIM, :].astype(BF16)


def _cmp_prompt(cmp_rows, pee, w1e, w2e, kg0):
    B, T = cmp_rows.shape[:2]
    nc = T // CMP_STRIDE
    x = cmp_rows.reshape(B, nc, CHUNK_W)
    full = lambda s: pl.BlockSpec(s, lambda b: (0,) * len(s))
    return pl.pallas_call(
        _cmp_prompt_kernel,
        grid=(B,),
        in_specs=[pl.BlockSpec((1, nc, CHUNK_W), lambda b: (b, 0, 0)), full(pee.shape), full(w1e.shape),
                  full(w2e.shape), full((1, HEAD_DIM))],
        out_specs=[pl.BlockSpec((1, NSA_KV, nc, HEAD_DIM), lambda b: (b, 0, 0, 0)),
                   pl.BlockSpec((1, NSA_KV, HEAD_DIM, nc), lambda b: (b, 0, 0, 0))],
        out_shape=[jax.ShapeDtypeStruct((B, NSA_KV, nc, HEAD_DIM), BF16),
                   jax.ShapeDtypeStruct((B, NSA_KV, HEAD_DIM, nc), BF16)],
        compiler_params=_cparams(("parallel",), 48),
        name="nsa_compress",
    )(x, pee, w1e, w2e, kg0.reshape(1, HEAD_DIM))


def _softmax_step(s, mask, vt, carry):
    m, l, acc = carry
    s = jnp.where(mask, s, NEG_BIG)
    m_new = jnp.maximum(m, jnp.max(s, axis=0, keepdims=True))
    alpha = jnp.exp(m - m_new)
    e = jnp.where(mask, jnp.exp(s - m_new), 0.0)
    l = alpha * l + jnp.sum(e, axis=0, keepdims=True)
    acc = alpha * acc + _dot(vt, e.astype(BF16))
    return m_new, l, acc


def _nsa_prompt_kernel(q_ref, ck_ref, cvt_ref, sk_ref, svt_ref, wk_ref, wvt_ref, gt_ref, ovl_ref, o_ref, sel_ref,
                       *, n_cmp, n_blk):
    qb = pl.program_id(1)
    b0 = qb * Q_BLOCK
    QW = NSA_GROUP * Q_BLOCK
    lane = _iota((1, QW), 1)
    q_pos = b0 + lane % Q_BLOCK
    head = lane // Q_BLOCK
    qp1 = b0 + _iota((1, Q_BLOCK), 1)
    bj = _iota((n_blk, 1), 0)
    init = (jnp.full((1, QW), NEG_BIG, F32), jnp.zeros((1, QW), F32), jnp.zeros((HEAD_DIM, QW), F32))
    TK = Q_BLOCK
    gates = gt_ref[0, 0]
    outs = []
    for g in range(NSA_KV):
        slope = _alibi_slope(head, g)
        qg = q_ref[0, NSA_GROUP * g:NSA_GROUP * (g + 1)].reshape(QW, HEAD_DIM)

        dist = q_pos - (_iota((n_cmp, 1), 0) * CMP_STRIDE + (CMP_LEN - 1))
        mask = dist >= 0
        s = jnp.where(mask, _dot_nt(ck_ref[0, g], qg) - slope * dist.astype(F32), NEG_BIG)
        m = jnp.max(s, axis=0, keepdims=True)
        e = jnp.where(mask, jnp.exp(s - m), 0.0)
        p = (e / jnp.maximum(jnp.sum(e, axis=0, keepdims=True), 1e-30)).astype(BF16)
        o_cmp = _dot(cvt_ref[0, g], p)

        imp = _dot(ovl_ref[...], p[:, 0:Q_BLOCK])
        for j in range(1, NSA_GROUP):
            imp += _dot(ovl_ref[...], p[:, j * Q_BLOCK:(j + 1) * Q_BLOCK])
        cur = qp1 // SLC_BLOCK
        forced = (bj == 0) | (bj == cur) | (bj == cur - 1)
        imp = jnp.where(forced, SEL_BIG, jnp.where(bj * SLC_BLOCK <= qp1, imp, -SEL_BIG))
        rank = jnp.zeros((n_blk, Q_BLOCK), F32)
        for i in range(n_blk):
            row = imp[i:i + 1, :]
            rank += jnp.where(row > imp, 1.0, jnp.where((row == imp) & (bj > i), 1.0, 0.0))
        sel_ref[g] = jnp.where(rank < float(min(N_SEL, n_blk)), 1.0, 0.0)

        def slc_body(kt, carry):
            k0 = pl.multiple_of(kt * TK, TK)
            s = _dot_nt(sk_ref[0, g, pl.ds(k0, TK), :], qg)
            dist = q_pos - (k0 + _iota((TK, 1), 0))
            per_blk = TK // SLC_BLOCK
            selm = sel_ref[g, pl.ds(kt * per_blk, 1), :]
            for i in range(1, per_blk):
                nxt = sel_ref[g, pl.ds(kt * per_blk + i, 1), :]
                selm = jnp.where(_iota((TK, 1), 0) < i * SLC_BLOCK, selm, nxt)
            selm = jnp.broadcast_to(selm, (TK, Q_BLOCK))
            selm = jnp.concatenate([selm] * NSA_GROUP, axis=1)
            mask = (dist >= 0) & (selm > 0.5)
            return _softmax_step(s - slope * dist.astype(F32), mask, svt_ref[0, g, :, pl.ds(k0, TK)], carry)

        _, l, acc = lax.fori_loop(0, qb + 1, slc_body, init)
        o_slc = acc / jnp.maximum(l, 1e-30)

        def win_body(kt, carry):
            k0 = pl.multiple_of(kt * TK, TK)
            s = _dot_nt(wk_ref[0, g, pl.ds(k0, TK), :], qg)
            dist = q_pos - (k0 + _iota((TK, 1), 0))
            mask = (dist >= 0) & (dist < WINDOW)
            return _softmax_step(s - slope * dist.astype(F32), mask, wvt_ref[0, g, :, pl.ds(k0, TK)], carry)

        _, l, acc = lax.fori_loop(jnp.maximum(qb - WINDOW // TK, 0), qb + 1, win_body, init)
        o_win = acc / jnp.maximum(l, 1e-30)

        outs.append(gates[g:g + 1] * o_cmp + gates[NSA_KV + g:NSA_KV + g + 1] * o_slc
                    + gates[2 * NSA_KV + g:2 * NSA_KV + g + 1] * o_win)

    for g in range(NSA_KV):
        for jp in range(NSA_GROUP // 2):
            pair = jnp.concatenate([outs[g][:, (2 * jp) * Q_BLOCK:(2 * jp + 1) * Q_BLOCK],
                                    outs[g][:, (2 * jp + 1) * Q_BLOCK:(2 * jp + 2) * Q_BLOCK]], axis=0)
            c0 = (NSA_GROUP * g + 2 * jp) * HEAD_DIM
            o_ref[0, :, c0:c0 + 2 * HEAD_DIM] = pair.T


def _overlap(n_cmp, n_blk):
    ci = np.arange(n_cmp)[:, None] * CMP_STRIDE
    bj = np.arange(n_blk)[None, :]
    return ((ci < (bj + 1) * SLC_BLOCK) & (ci + CMP_LEN > bj * SLC_BLOCK)).astype(np.float32)


def _nsa_prompt(q, ck, cvt, sk, svt, wk, wvt, gates):
    B, _, T, _ = q.shape
    nq = T // Q_BLOCK
    n_cmp = ck.shape[2]
    n_blk = T // SLC_BLOCK
    ovl = jnp.asarray(_overlap(n_cmp, n_blk).T, BF16)
    per_b = lambda s: pl.BlockSpec((1,) + s, lambda b, i: (b,) + (0,) * len(s))
    return pl.pallas_call(
        functools.partial(_nsa_prompt_kernel, n_cmp=n_cmp, n_blk=n_blk),
        grid=(B, nq),
        in_specs=[
            pl.BlockSpec((1, NSA_HEADS, Q_BLOCK, HEAD_DIM), lambda b, i: (b, 0, i, 0)),
            per_b((NSA_KV, n_cmp, HEAD_DIM)), per_b((NSA_KV, HEAD_DIM, n_cmp)),
            per_b((NSA_KV, T, HEAD_DIM)), per_b((NSA_KV, HEAD_DIM, T)),
            per_b((NSA_KV, T, HEAD_DIM)), per_b((NSA_KV, HEAD_DIM, T)),
            pl.BlockSpec((1, 1, 3 * NSA_KV, NSA_GROUP * Q_BLOCK), lambda b, i: (b, i, 0, 0)),
            pl.BlockSpec((n_blk, n_cmp), lambda b, i: (0, 0)),
        ],
        out_specs=pl.BlockSpec((1, Q_BLOCK, NSA_W), lambda b, i: (b, i, 0)),
        out_shape=jax.ShapeDtypeStruct((B, T, NSA_W), F32),
        scratch_shapes=[pltpu.VMEM((NSA_KV, n_blk, Q_BLOCK), F32)],
        compiler_params=_cparams(("parallel", "arbitrary"), 48),
        name="nsa_prompt",
    )(q, ck, cvt, sk, svt, wk, wvt, gates, ovl)


def _nsa_decode_kernel(pt_ref, q_ref, new_ref, gt_ref, pe_ref, w1_ref, w2_ref, kg_ref, ovl_ref, exp_ref, *rest,
                       n_pages):
    del pt_ref
    cmp_pages = rest[:n_pages]
    slc_pages = rest[n_pages:2 * n_pages]
    win_ref, o_ref = rest[2 * n_pages], rest[2 * n_pages + 1]
    n_past = n_pages * PAGE_SIZE
    n_cmp = n_past // CMP_STRIDE
    q_pos = n_past
    new = new_ref[0]
    x = jnp.concatenate([r[0] for r in cmp_pages], axis=0)
    xnew = jnp.concatenate([new[:, 0:ROW_W], jnp.zeros((8, CHUNK_W - ROW_W), F32)], axis=1)
    cks, cv = _compress(x, xnew, pe_ref, w1_ref, w2_ref, kg_ref)

    jrow = _iota((8, 1), 0)
    lanes = _iota((1, LANE), 1)
    n_blk = (n_past + 1 + SLC_BLOCK - 1) // SLC_BLOCK
    cur = q_pos // SLC_BLOCK
    ii = _iota((LANE, LANE), 0)
    jj = _iota((LANE, LANE), 1)
    for g in range(NSA_KV):
        slope = _alibi_slope(jrow, g)
        qg = q_ref[0, g]
        lo, hi = g * HEAD_DIM, (g + 1) * HEAD_DIM

        dist = q_pos - (lanes * CMP_STRIDE + (CMP_LEN - 1))
        mask = dist >= 0
        s = jnp.where(mask, _dot_nt(qg, cks[g].astype(BF16)) - slope * dist.astype(F32), NEG_BIG)
        m = jnp.max(s, axis=1, keepdims=True)
        e = jnp.where(mask, jnp.exp(s - m), 0.0)
        p = (e / jnp.maximum(jnp.sum(e, axis=1, keepdims=True), 1e-30)).astype(BF16)
        o_cmp = _dot(p, cv[:, lo:hi].astype(BF16))

        imp = jnp.sum(jnp.where(jrow < NSA_GROUP, _dot(p, ovl_ref[...]), 0.0), axis=0, keepdims=True)
        forced = (lanes == 0) | (lanes == cur) | (lanes == cur - 1)
        imp = jnp.where(forced, SEL_BIG, jnp.where(lanes * SLC_BLOCK <= q_pos, imp, -SEL_BIG))
        imp = jnp.where(lanes < n_blk, imp, -3e38)
        impr = jnp.broadcast_to(imp, (LANE, LANE))
        impc = jnp.sum(jnp.where(ii == jj, impr, 0.0), axis=1, keepdims=True)
        beats = jnp.where(impc > impr, 1.0, jnp.where((impc == impr) & (ii < jj), 1.0, 0.0))
        rank = jnp.sum(beats, axis=0, keepdims=True)
        sel = jnp.where(rank < float(min(N_SEL, n_blk)), 1.0, 0.0)
        selk = _dot(jnp.broadcast_to(sel, (8, LANE)).astype(BF16), exp_ref[...])

        def attend(kmat, vmat, mask, dist, knew, vnew, mask_new):
            s = jnp.where(mask, _dot_nt(qg, kmat) - slope * dist.astype(F32), NEG_BIG)
            s_new = jnp.sum(qg.astype(F32) * knew.astype(BF16).astype(F32), axis=1, keepdims=True)
            if mask_new is not None:
                s_new = jnp.where(mask_new, s_new, NEG_BIG)
            m = jnp.maximum(jnp.max(s, axis=1, keepdims=True), s_new)
            e = jnp.where(mask, jnp.exp(s - m), 0.0)
            e_new = jnp.exp(s_new - m)
            if mask_new is not None:
                e_new = jnp.where(mask_new, e_new, 0.0)
            l = jnp.sum(e, axis=1, keepdims=True) + e_new
            o = _dot(e.astype(BF16), vmat) + e_new.astype(BF16).astype(F32) * vnew.astype(BF16).astype(F32)
            return o / jnp.maximum(l, 1e-30)

        kpos = _iota((1, n_past), 1)
        sk = jnp.concatenate([r[0][:, lo:hi] for r in slc_pages], axis=0).astype(BF16)
        sv = jnp.concatenate([r[0][:, KV_W + lo:KV_W + hi] for r in slc_pages], axis=0).astype(BF16)
        o_slc = attend(sk, sv, selk > 0.5, q_pos - kpos, new[1:2, lo:hi], new[1:2, KV_W + lo:KV_W + hi],
                       sel[:, cur:cur + 1] > 0.5)

        n_win = win_ref.shape[1]
        wpos = (q_pos - n_win) + _iota((1, n_win), 1)
        dist_w = q_pos - wpos
        wrows = win_ref[0]
        o_win = attend(wrows[:, lo:hi].astype(BF16), wrows[:, KV_W + lo:KV_W + hi].astype(BF16),
                       (dist_w < WINDOW) & (wpos >= 0), dist_w, new[2:3, lo:hi], new[2:3, KV_W + lo:KV_W + hi],
                       None)

        gt = gt_ref[0, g]
        o_ref[0, g] = gt[:, 0:1] * o_cmp + gt[:, 1:2] * o_slc + gt[:, 2:3] * o_win


def _nsa_decode(page_table, q, new_rows, gates, cmp_pool, slc_pool, win, pee, w1e, w2e, kg0):
    B, n_pages = page_table.shape
    n_past = n_pages * PAGE_SIZE
    n_cmp = n_past // CMP_STRIDE
    assert n_cmp == LANE and win.shape[1] <= n_past
    n_blk = (n_past + 1 + SLC_BLOCK - 1) // SLC_BLOCK
    ovl = np.zeros((n_cmp, LANE), np.float32)
    ovl[:, :n_blk] = _overlap(n_cmp, n_blk)
    expand = (np.arange(n_past)[None, :] // SLC_BLOCK == np.arange(LANE)[:, None]).astype(np.float32)
    chunks_per_page = PAGE_SIZE // CMP_STRIDE
    cmp_chunks = cmp_pool.reshape(cmp_pool.shape[0], chunks_per_page, CHUNK_W)
    full = lambda s: pl.BlockSpec(s, lambda b, pt: (0,) * len(s))
    per_b = lambda s: pl.BlockSpec((1,) + s, lambda b, pt: (b,) + (0,) * len(s))

    def page_spec(shape, p):
        return pl.BlockSpec((1,) + shape, lambda b, pt: (pt[b, p], 0, 0))

    in_specs = [per_b((NSA_KV, 8, HEAD_DIM)), per_b((8, ROW_W)), per_b((NSA_KV, 8, LANE)), full(pee.shape),
                full(w1e.shape), full(w2e.shape), full((1, HEAD_DIM)), full(ovl.shape), full(expand.shape)]
    in_specs += [page_spec((chunks_per_page, CHUNK_W), p) for p in range(n_pages)]
    in_specs += [page_spec((PAGE_SIZE, ROW_W), p) for p in range(n_pages)]
    in_specs += [per_b(win.shape[1:])]
    return pl.pallas_call(
        functools.partial(_nsa_decode_kernel, n_pages=n_pages),
        grid_spec=pltpu.PrefetchScalarGridSpec(
            num_scalar_prefetch=1, grid=(B,), in_specs=in_specs,
            out_specs=pl.BlockSpec((1, NSA_KV, 8, HEAD_DIM), lambda b, pt: (b, 0, 0, 0))),
        out_shape=jax.ShapeDtypeStruct((B, NSA_KV, 8, HEAD_DIM), F32),
        compiler_params=_cparams(("arbitrary",), 48),
        name="nsa_decode",
    )(page_table, q, new_rows, gates, pee, w1e, w2e, kg0.reshape(1, HEAD_DIM), jnp.asarray(ovl, BF16),
      jnp.asarray(expand, BF16), *([cmp_chunks] * n_pages), *([slc_pool] * n_pages), win)


def _dn_prompt_kernel(q_ref, k_ref, v_ref, aux_ref, o_ref, s_out_ref, s_ref, *, n_chunk):
    c = pl.program_id(1)

    @pl.when(c == 0)
    def _():
        s_ref[...] = jnp.zeros_like(s_ref)

    C = DN_CHUNK
    ii = _iota((C, C), 0)
    jj = _iota((C, C), 1)
    incl = ii >= jj
    eye = (ii == jj).astype(F32)
    aux = aux_ref[0]
    for h in range(DN_HEADS):
        cs = slice(h * DN_DK, (h + 1) * DN_DK)
        q, k, v = q_ref[0, :, cs], k_ref[0, :, cs], v_ref[0, :, cs]
        gcol = aux[:, h:h + 1]
        bcol = aux[:, DN_HEADS + h:DN_HEADS + h + 1]
        grow = jnp.sum(jnp.where(ii <= jj, gcol, 0.0), axis=0, keepdims=True)
        gcum = jnp.sum(jnp.where(ii == jj, grow, 0.0), axis=1, keepdims=True)
        decay = jnp.where(incl, jnp.exp(jnp.where(incl, gcum - grow, 0.0)), 0.0)
        kb = k.astype(BF16)
        lower = jnp.where(ii > jj, bcol * _dot_nt(kb, kb) * decay, 0.0)
        npow = -lower
        tinv = eye + npow
        for _ in range(int(np.log2(C)) - 1):
            npow = _dot3(npow, npow)
            tinv = tinv + _dot3(tinv, npow)
        tb = tinv.astype(BF16)
        eg = jnp.exp(gcum)
        u = _dot(tb, (v * bcol).astype(BF16))
        w = _dot(tb, (k * (bcol * eg)).astype(BF16))
        S = s_ref[h]
        Sb = S.astype(BF16)
        v_new = u - _dot(w.astype(BF16), Sb)
        vnb = v_new.astype(BF16)
        attn = _dot_nt(q.astype(BF16), kb) * decay
        o_ref[0, :, cs] = _dot((q * eg).astype(BF16), Sb) + _dot(attn.astype(BF16), vnb)
        g_last = gcum[C - 1:C, :]
        s_ref[h] = S * jnp.exp(g_last) + _dot_tn((k * jnp.exp(g_last - gcum)).astype(BF16), vnb)

    @pl.when(c == n_chunk - 1)
    def _():
        s_out_ref[0] = s_ref[...]


def _dn_prompt(q, k, v, aux):
    B, T, _ = q.shape
    n_chunk = T // DN_CHUNK
    blk = lambda w: pl.BlockSpec((1, DN_CHUNK, w), lambda b, c: (b, c, 0))
    return pl.pallas_call(
        functools.partial(_dn_prompt_kernel, n_chunk=n_chunk),
        grid=(B, n_chunk),
        in_specs=[blk(DN_QK), blk(DN_QK), blk(DN_V), blk(2 * DN_HEADS)],
        out_specs=[blk(DN_V), pl.BlockSpec((1, DN_HEADS, DN_DK, DN_DV), lambda b, c: (b, 0, 0, 0))],
        out_shape=[jax.ShapeDtypeStruct((B, T, DN_V), F32), jax.ShapeDtypeStruct((B, DN_HEADS, DN_DK, DN_DV), F32)],
        scratch_shapes=[pltpu.VMEM((DN_HEADS, DN_DK, DN_DV), F32)],
        compiler_params=_cparams(("parallel", "arbitrary"), 32),
        name="dn_prompt",
    )(q, k, v, aux)


def _row0(x):
    return jnp.where(_iota(x.shape, 0) == 0, x, 0.0)


def _dn_decode_kernel(vec_ref, s_ref, o_ref, s_out_ref, *, bb):
    def body(b, _):
        for h in range(DN_HEADS):
            x = vec_ref[b, h]
            S = s_ref[b, h]
            xs = _dot(x.astype(BF16), S.astype(BF16))
            k, q, v, eg, beta = x[0:1], x[1:2], x[2:3], x[3:4], x[4:5]
            v_new = beta * (v - eg * xs[0:1])
            kb = k.astype(BF16).astype(F32)
            qk = jnp.sum(q.astype(BF16).astype(F32) * kb, axis=1, keepdims=True)
            o = eg * xs[1:2] + qk.astype(BF16).astype(F32) * v_new.astype(BF16).astype(F32)
            o_ref[b, h] = jnp.broadcast_to(o, (8, DN_DV))
            s_out_ref[b, h] = S * eg[:, 0:1] + _dot_tn(_row0(x).astype(BF16),
                                                      _row0(jnp.broadcast_to(v_new, (8, DN_DV))).astype(BF16))
        return 0

    lax.fori_loop(0, bb, body, 0)


def _dn_decode(vec, S0, bb=8):
    B = vec.shape[0]
    spec_v = pl.BlockSpec((bb, DN_HEADS, 8, DN_DK), lambda i: (i, 0, 0, 0))
    spec_s = pl.BlockSpec((bb, DN_HEADS, DN_DK, DN_DV), lambda i: (i, 0, 0, 0))
    return pl.pallas_call(
        functools.partial(_dn_decode_kernel, bb=bb),
        grid=(B // bb,),
        in_specs=[spec_v, spec_s],
        out_specs=[spec_v, spec_s],
        out_shape=[jax.ShapeDtypeStruct(vec.shape, F32), jax.ShapeDtypeStruct(S0.shape, F32)],
        compiler_params=_cparams(("parallel",), 32),
        name="dn_decode",
    )(vec, S0)


def _hg_prompt_kernel(q_ref, k_ref, v_ref, lf_ref, tril_ref, o_ref, s_out_ref, st_ref, g_ref, *, n_step, tb):
    t = pl.program_id(1)

    @pl.when(t == 0)
    def _():
        st_ref[...] = jnp.zeros_like(st_ref)

    lf = lf_ref[0]
    hi = lf.astype(BF16)
    r1 = lf - hi.astype(F32)
    mid = r1.astype(BF16)
    lo = (r1 - mid.astype(F32)).astype(BF16)
    tril = tril_ref[...]
    g_ref[...] = _dot(tril, hi) + (_dot(tril, mid) + _dot(tril, lo))
    C = HG_CHUNK
    sidx = _iota((C, 1), 0)

    def sub(sc, _):
        r0 = pl.multiple_of(sc * C, C)
        for h in range(HG_HEADS):
            cs = slice(h * HG_DK, (h + 1) * HG_DK)
            q = q_ref[0, pl.ds(r0, C), cs]
            k = k_ref[0, pl.ds(r0, C), cs]
            v = v_ref[0, pl.ds(r0, C), cs]
            G = g_ref[pl.ds(r0, C), cs]
            st = st_ref[h]
            cols = []
            for tt in range(C):
                msk = sidx <= tt
                dec = jnp.where(msk, jnp.exp(jnp.where(msk, G[tt:tt + 1, :] - G, 0.0)), 0.0)
                cols.append(jnp.sum(dec * k * q[tt:tt + 1, :], axis=1, keepdims=True))
            a_t = jnp.concatenate(cols, axis=1)
            vb = v.astype(BF16)
            o_ref[0, pl.ds(r0, C), cs] = (_dot_nt((q * jnp.exp(G)).astype(BF16), st.astype(BF16))
                                          + _dot_tn(a_t.astype(BF16), vb))
            g_last = G[C - 1:C, :]
            st_ref[h] = st * jnp.exp(g_last) + _dot_tn(vb, (k * jnp.exp(g_last - G)).astype(BF16))
        return 0

    lax.fori_loop(0, tb // C, sub, 0)

    @pl.when(t == n_step - 1)
    def _():
        for h in range(HG_HEADS):
            s_out_ref[0, h] = st_ref[h].T


def _hg_prompt(q, k, v, logf, tb=128):
    B, T, _ = q.shape
    n_step = T // tb
    r = np.arange(tb)
    tril = ((r[:, None] >= r[None, :]) & (r[:, None] // HG_CHUNK == r[None, :] // HG_CHUNK)).astype(np.float32)
    blk = pl.BlockSpec((1, tb, HG_WK), lambda b, t: (b, t, 0))
    return pl.pallas_call(
        functools.partial(_hg_prompt_kernel, n_step=n_step, tb=tb),
        grid=(B, n_step),
        in_specs=[blk, blk, blk, blk, pl.BlockSpec((tb, tb), lambda b, t: (0, 0))],
        out_specs=[blk, pl.BlockSpec((1, HG_HEADS, HG_DK, HG_DV), lambda b, t: (b, 0, 0, 0))],
        out_shape=[jax.ShapeDtypeStruct((B, T, HG_WV), F32), jax.ShapeDtypeStruct((B, HG_HEADS, HG_DK, HG_DV), F32)],
        scratch_shapes=[pltpu.VMEM((HG_HEADS, HG_DV, HG_DK), F32), pltpu.VMEM((tb, HG_WK), F32)],
        compiler_params=_cparams(("parallel", "arbitrary"), 32),
        name="hg_prompt",
    )(q, k, v, logf, jnp.asarray(tril, BF16))


def _hg_decode_kernel(vec_ref, s_ref, o_ref, s_out_ref, *, bb):
    ii = _iota((HG_DK, HG_DK), 0)
    jj = _iota((HG_DK, HG_DK), 1)

    def body(b, _):
        for h in range(HG_HEADS):
            x = vec_ref[b, h]
            S = s_ref[b, h]
            k, q, v, lf = x[0:1], x[1:2], x[2:3], x[3:4]
            f = jnp.exp(lf)
            qs = _dot(jnp.broadcast_to(q * f, (8, HG_DK)).astype(BF16), S.astype(BF16))
            a = jnp.sum(q * k, axis=1, keepdims=True)
            o = qs[0:1] + a.astype(BF16).astype(F32) * v.astype(BF16).astype(F32)
            o_ref[b, h] = jnp.broadcast_to(o, (8, HG_DV))
            fcol = jnp.sum(jnp.where(ii == jj, f, 0.0), axis=1, keepdims=True)
            s_out_ref[b, h] = S * fcol + _dot_tn(_row0(x).astype(BF16),
                                                 _row0(jnp.broadcast_to(v, (8, HG_DV))).astype(BF16))
        return 0

    lax.fori_loop(0, bb, body, 0)


def _hg_decode(vec, S0, bb=8):
    B = vec.shape[0]
    spec_v = pl.BlockSpec((bb, HG_HEADS, 8, HG_DK), lambda i: (i, 0, 0, 0))
    spec_s = pl.BlockSpec((bb, HG_HEADS, HG_DK, HG_DV), lambda i: (i, 0, 0, 0))
    return pl.pallas_call(
        functools.partial(_hg_decode_kernel, bb=bb),
        grid=(B // bb,),
        in_specs=[spec_v, spec_s],
        out_specs=[spec_v, spec_s],
        out_shape=[jax.ShapeDtypeStruct(vec.shape, F32), jax.ShapeDtypeStruct(S0.shape, F32)],
        compiler_params=_cparams(("parallel",), 32),
        name="hg_decode",
    )(vec, S0)


def _head_rms(x, g):
    return x * lax.rsqrt(jnp.mean(x * x, axis=-1, keepdims=True) + NORM_EPS) * g


def _l2n(x):
    return x * lax.rsqrt(jnp.sum(x * x, axis=-1, keepdims=True) + L2_EPS)


def _rows8(rows):
    x = jnp.stack(rows, axis=-2)
    pad = [(0, 0)] * x.ndim
    pad[-2] = (0, 8 - len(rows))
    return jnp.pad(x, pad)


def _layer(x, l, prm, wts, past, page_table):
    B, T, D = x.shape
    M = B * T
    tm = 512 if M >= 512 else M
    x2 = x.reshape(M, D)
    x2 = _ffn(x2, prm["ffn1_norm"][l], wts["ffn1_w_gu"][l], wts["ffn1_w_down"][l], tm)
    z = _inproj(x2, prm["mix_norm"][l], wts["w_in"][l], tm)
    sizes, _, offs, _ = _z_layout()
    zs = {n: z[:, offs[n]:offs[n] + sizes[n]].reshape(B, T, sizes[n]) for n in Z_ORDER if n != "m_g"}

    q = _head_rms(zs["a_q"].reshape(B, T, NSA_HEADS, HEAD_DIM), prm["nsa_q_norm"][l]) * (HEAD_DIM ** -0.5)
    kv = zs["a_kv"].reshape(B, T, 3, 2, NSA_KV, HEAD_DIM)
    kg = prm["nsa_k_norm"][l]
    cmp_new = kv[:, :, 0]
    slc_new = jnp.stack([_head_rms(kv[:, :, 1, 0], kg[1]), kv[:, :, 1, 1]], axis=2)
    win_new = jnp.stack([_head_rms(kv[:, :, 2, 0], kg[2]), kv[:, :, 2, 1]], axis=2)
    gates = jax.nn.sigmoid(zs["a_g"].reshape(B, T, NSA_HEADS, 3))
    pee, w1e, w2e = wts["cmp"][l]
    if past is None:
        nq = T // Q_BLOCK
        ck, cvt = _cmp_prompt(cmp_new.reshape(B, T, ROW_W), pee, w1e, w2e, kg[0])
        to_k = lambda a: a.transpose(0, 2, 1, 3).astype(BF16)
        to_vt = lambda a: a.transpose(0, 2, 3, 1).astype(BF16)
        gt = gates.reshape(B, nq, Q_BLOCK, NSA_KV, NSA_GROUP, 3).transpose(0, 1, 5, 3, 4, 2)
        gt = gt.reshape(B, nq, 3 * NSA_KV, NSA_GROUP * Q_BLOCK)
        o_a = _nsa_prompt(q.transpose(0, 2, 1, 3).astype(BF16), ck, cvt, to_k(slc_new[:, :, 0]),
                          to_vt(slc_new[:, :, 1]), to_k(win_new[:, :, 0]), to_vt(win_new[:, :, 1]), gt)
        win_state = win_new[:, T - min(WINDOW, T):]
    else:
        qd = q.reshape(B, NSA_KV, NSA_GROUP, HEAD_DIM)
        qd = jnp.pad(qd, ((0, 0), (0, 0), (0, 8 - NSA_GROUP), (0, 0))).astype(BF16)
        new_rows = _rows8([cmp_new.reshape(B, ROW_W), slc_new.reshape(B, ROW_W), win_new.reshape(B, ROW_W)])
        gd = gates.reshape(B, NSA_KV, NSA_GROUP, 3)
        gd = jnp.pad(gd, ((0, 0), (0, 0), (0, 8 - NSA_GROUP), (0, LANE - 3)))
        n_pool = past["cmp"].shape[0]
        o8 = _nsa_decode(page_table, qd, new_rows, gd, past["cmp"].reshape(n_pool, PAGE_SIZE, ROW_W),
                         past["slc"].reshape(n_pool, PAGE_SIZE, ROW_W),
                         past["win"].reshape(B, -1, ROW_W), pee, w1e, w2e, kg[0])
        o_a = o8[:, :, :NSA_GROUP].reshape(B, T, NSA_W)
        win_state = win_new

    d_qkv = zs["d_qkv"]
    buf = jnp.zeros((B, CONV_W - 1, DN_QKV), F32) if past is None else past["conv"]
    xx = jnp.concatenate([buf, d_qkv], axis=1)
    cw = prm["dn_conv_w"][l]
    qkv = sum(xx[:, j:j + T] * cw[j] for j in range(CONV_W))
    conv_state = xx[:, -(CONV_W - 1):]
    dq, dk, dv = jnp.split(jax.nn.silu(qkv), [DN_QK, 2 * DN_QK], axis=-1)
    dq = _l2n(dq.reshape(B, T, DN_HEADS, DN_DK)) * (DN_DK ** -0.5)
    dk = _l2n(dk.reshape(B, T, DN_HEADS, DN_DK))
    beta = jax.nn.sigmoid(zs["d_b"])
    g = -jnp.exp(prm["dn_A_log"][l]) * jax.nn.softplus(zs["d_a"] + prm["dn_dt_bias"][l])
    if past is None:
        o_d, dn_state = _dn_prompt(dq.reshape(B, T, DN_QK), dk.reshape(B, T, DN_QK), dv,
                                   jnp.concatenate([g, beta], axis=-1))
    else:
        lanes = lambda a: jnp.broadcast_to(a[:, 0, :, None], (B, DN_HEADS, DN_DK))
        vec = _rows8([dk[:, 0], dq[:, 0], dv.reshape(B, DN_HEADS, DN_DV), lanes(jnp.exp(g)), lanes(beta)])
        o8, dn_state = _dn_decode(vec, past["dn_S"])
        o_d = o8[:, :, 0].reshape(B, T, DN_V)
    o_d = _head_rms(o_d.reshape(B, T, DN_HEADS, DN_DV), prm["dn_out_norm"][l]) * jax.nn.silu(
        zs["d_z"].reshape(B, T, DN_HEADS, DN_DV))

    p = jax.nn.softmax(prm["hg_lb_logits"], axis=0)
    lb = (jnp.cumsum(p, axis=0) - p[0])[l]
    zf = zs["r_f"]
    logf = jnp.log(lb + (1.0 - lb) * jax.nn.sigmoid(zf))
    k_in = (1.0 - lb) * jax.nn.sigmoid(-zf)
    hq = jax.nn.silu(zs["r_q"]) * (HG_DK ** -0.5)
    if past is None:
        o_h, hg_state = _hg_prompt(hq, k_in, zs["r_i"], logf)
    else:
        hd = lambda a: a.reshape(B, HG_HEADS, HG_DK)
        vec = _rows8([hd(k_in), hd(hq), hd(zs["r_i"]), hd(logf)])
        o8, hg_state = _hg_decode(vec, past["hg_S"])
        o_h = o8[:, :, 0].reshape(B, T, HG_WV)
    o_h = _head_rms(o_h.reshape(B, T, HG_HEADS, HG_DV), prm["hg_out_norm"][l]) * jax.nn.sigmoid(
        zs["r_og"].reshape(B, T, HG_HEADS, HG_DV))

    x2 = _merge(x2, o_a.reshape(M, NSA_W), o_d.reshape(M, DN_V), o_h.reshape(M, HG_WV), z, wts["w_branch"][l],
                wts["w_out"][l], tm)
    x2 = _ffn(x2, prm["ffn2_norm"][l], wts["ffn2_w_gu"][l], wts["ffn2_w_down"][l], tm)
    return x2.reshape(B, T, D), (cmp_new, slc_new, win_state, conv_state, dn_state, hg_state)


def _trunk(x, prm, wts, caches, page_table):
    new = []
    for l in range(DEPTH):
        past = None
        if caches is not None:
            past = {k: v[l] for k, v in caches.items()}
        x, st = _layer(x, l, prm, wts, past, page_table)
        new.append(st)
    return x, [jnp.stack([s[i] for s in new], axis=0) for i in range(6)]


def kernel(x_prompt, x_sample, cache_cmp_kv, cache_slc_kv, cache_win_kv, state_dn_conv, state_dn_S, state_hg_S,
           page_table, ffn1_norm, ffn1_w_gu, ffn1_w_down, mix_norm, w_in, nsa_q_norm, nsa_k_norm, nsa_cmp_pe,
           nsa_cmp_w1, nsa_cmp_w2, dn_conv_w, dn_A_log, dn_dt_bias, dn_out_norm, hg_lb_logits, hg_out_norm,
           w_branch, w_out, ffn2_norm, ffn2_w_gu, ffn2_w_down):
    prm = dict(ffn1_norm=ffn1_norm, mix_norm=mix_norm, nsa_q_norm=nsa_q_norm, nsa_k_norm=nsa_k_norm,
               dn_conv_w=dn_conv_w, dn_A_log=dn_A_log, dn_dt_bias=dn_dt_bias, dn_out_norm=dn_out_norm,
               hg_lb_logits=hg_lb_logits, hg_out_norm=hg_out_norm, ffn2_norm=ffn2_norm)
    bf = lambda w: w.astype(BF16)
    wts = dict(ffn1_w_gu=bf(ffn1_w_gu), ffn1_w_down=bf(ffn1_w_down), ffn2_w_gu=bf(ffn2_w_gu),
               ffn2_w_down=bf(ffn2_w_down), w_branch=bf(w_branch), w_out=bf(w_out),
               w_in=jnp.stack([_permute_w_in(w_in[l]) for l in range(DEPTH)]),
               cmp=[_cmp_weights(nsa_cmp_pe[l], nsa_cmp_w1[l], nsa_cmp_w2[l]) for l in range(DEPTH)])
    y_p, (p_cmp, p_slc, p_win, p_conv, p_dn, p_hg) = _trunk(x_prompt, prm, wts, None, None)
    caches = dict(cmp=cache_cmp_kv, slc=cache_slc_kv, win=cache_win_kv, conv=state_dn_conv, dn_S=state_dn_S,
                  hg_S=state_hg_S)
    y_s, (s_cmp, s_slc, s_win, s_conv, s_dn, s_hg) = _trunk(x_sample, prm, wts, caches, page_table)
    return (y_p, y_s, p_cmp, s_cmp, p_slc, s_slc, p_win, s_win, p_conv, s_conv, p_dn, s_dn, p_hg, s_hg)
```

```python
import functools

import jax
import jax.numpy as jnp
import numpy as np
from jax import lax
from jax.experimental import pallas as pl
from jax.experimental.pallas import tpu as pltpu

F32 = jnp.float32
BF16 = jnp.bfloat16

D_MODEL = 1024
DEPTH = 2
PAST_LEN = 2048
PAGE_SIZE = 128
HEAD_DIM = 64
NSA_HEADS = 8
NSA_KV = 2
NSA_GROUP = NSA_HEADS // NSA_KV
CMP_LEN = 32
CMP_STRIDE = 16
CMP_HIDDEN = 128
SLC_BLOCK = 64
N_SEL = 16
WINDOW = 512
Q_BLOCK = 128
DN_HEADS = 4
DN_DK = 128
DN_DV = 128
DN_CHUNK = 64
CONV_W = 4
HG_HEADS = 4
HG_DK = 128
HG_DV = 128
HG_CHUNK = 16
D_FF = 2816
NORM_EPS = 1e-6
L2_EPS = 1e-6
NEG_BIG = -1e30
SEL_BIG = 1e9

NSA_W = NSA_HEADS * HEAD_DIM
KV_W = NSA_KV * HEAD_DIM
ROW_W = 2 * KV_W
DN_QK = DN_HEADS * DN_DK
DN_V = DN_HEADS * DN_DV
DN_QKV = 2 * DN_QK + DN_V
HG_WK = HG_HEADS * HG_DK
HG_WV = HG_HEADS * HG_DV
MIX_W = NSA_W + DN_V + HG_WV
IN_SPLITS = (NSA_W, 6 * KV_W, 3 * NSA_HEADS, DN_QKV, DN_HEADS, DN_HEADS, DN_V, HG_WK, HG_WV, HG_WK, HG_WV, 3 * D_MODEL)
IN_NAMES = ("a_q", "a_kv", "a_g", "d_qkv", "d_b", "d_a", "d_z", "r_f", "r_i", "r_q", "r_og", "m_g")
Z_ORDER = ("m_g", "a_q", "a_kv", "d_qkv", "d_z", "r_f", "r_i", "r_q", "r_og", "a_g", "d_b", "d_a")
Z_TN = 512
CHUNK_W = CMP_STRIDE * ROW_W
CMP_HID_W = 2 * NSA_KV * CMP_HIDDEN

V7X_VMEM_BYTES = 64 * 1024 * 1024
LANE = 128


def _cparams(sem, vmem_mb):
    assert vmem_mb * 1024 * 1024 < V7X_VMEM_BYTES
    return pltpu.CompilerParams(dimension_semantics=sem, vmem_limit_bytes=vmem_mb * 1024 * 1024)


def _dot(a, b):
    return jnp.dot(a, b, preferred_element_type=F32)


def _dot_nt(a, b):
    return lax.dot_general(a, b, (((1,), (1,)), ((), ())), preferred_element_type=F32)


def _dot_tn(a, b):
    return lax.dot_general(a, b, (((0,), (0,)), ((), ())), preferred_element_type=F32)


def _split2(a):
    hi = a.astype(BF16)
    lo = (a - hi.astype(F32)).astype(BF16)
    return hi, lo


def _dot3(a, b):
    ah, al = _split2(a)
    bh, bl = _split2(b)
    return _dot(ah, bh) + (_dot(ah, bl) + _dot(al, bh))


def _rms(x, g):
    return x * lax.rsqrt(jnp.mean(x * x, axis=-1, keepdims=True) + NORM_EPS) * g


def _silu(x):
    return x * jax.nn.sigmoid(x)


def _iota(shape, dim):
    return lax.broadcasted_iota(jnp.int32, shape, dim)


def _alibi_slope(head_in_group, g):
    out = jnp.full(head_in_group.shape, 2.0 ** -(NSA_GROUP * g + NSA_GROUP), F32)
    for j in range(NSA_GROUP - 2, -1, -1):
        out = jnp.where(head_in_group == j, 2.0 ** -(NSA_GROUP * g + j + 1), out)
    return out


def _ffn_kernel(x_ref, g_ref, wg_ref, wu_ref, wd_ref, o_ref, xn_ref, acc_ref, *, nf):
    j = pl.program_id(1)

    @pl.when(j == 0)
    def _():
        xn_ref[...] = _rms(x_ref[...], g_ref[...]).astype(BF16)
        acc_ref[...] = jnp.zeros_like(acc_ref)

    xn = xn_ref[...]
    a = _silu(_dot(xn, wg_ref[...])) * _dot(xn, wu_ref[...])
    acc_ref[...] += _dot(a.astype(BF16), wd_ref[...])

    @pl.when(j == nf - 1)
    def _():
        o_ref[...] = x_ref[...] + 0.5 * acc_ref[...]


def _ffn(x, gain, w_gu, w_down, tm, tf=256):
    M, D = x.shape
    F = w_down.shape[0]
    nf = F // tf
    return pl.pallas_call(
        functools.partial(_ffn_kernel, nf=nf),
        grid=(M // tm, nf),
        in_specs=[
            pl.BlockSpec((tm, D), lambda i, j: (i, 0)),
            pl.BlockSpec((1, D), lambda i, j: (0, 0)),
            pl.BlockSpec((D, tf), lambda i, j: (0, j)),
            pl.BlockSpec((D, tf), lambda i, j: (0, j + nf)),
            pl.BlockSpec((tf, D), lambda i, j: (j, 0)),
        ],
        out_specs=pl.BlockSpec((tm, D), lambda i, j: (i, 0)),
        out_shape=jax.ShapeDtypeStruct((M, D), F32),
        scratch_shapes=[pltpu.VMEM((tm, D), BF16), pltpu.VMEM((tm, D), F32)],
        compiler_params=_cparams(("parallel", "arbitrary"), 40),
        name="ffn",
    )(x, gain.reshape(1, D), w_gu, w_gu, w_down)


def _inproj_kernel(x_ref, g_ref, w_ref, z_ref, xn_ref):
    @pl.when(pl.program_id(1) == 0)
    def _():
        xn_ref[...] = _rms(x_ref[...], g_ref[...]).astype(BF16)

    z_ref[...] = _dot(xn_ref[...], w_ref[...])


def _inproj(x, gain, w, tm):
    M, D = x.shape
    N = w.shape[1]
    return pl.pallas_call(
        _inproj_kernel,
        grid=(M // tm, N // Z_TN),
        in_specs=[
            pl.BlockSpec((tm, D), lambda i, j: (i, 0)),
            pl.BlockSpec((1, D), lambda i, j: (0, 0)),
            pl.BlockSpec((D, Z_TN), lambda i, j: (0, j)),
        ],
        out_specs=pl.BlockSpec((tm, Z_TN), lambda i, j: (i, j)),
        out_shape=jax.ShapeDtypeStruct((M, N), F32),
        scratch_shapes=[pltpu.VMEM((tm, D), BF16)],
        compiler_params=_cparams(("parallel", "arbitrary"), 40),
        name="inproj",
    )(x, gain.reshape(1, D), w)


def _z_layout():
    sizes = dict(zip(IN_NAMES, IN_SPLITS))
    src = dict(zip(IN_NAMES, np.cumsum((0,) + IN_SPLITS[:-1]).tolist()))
    offs, o = {}, 0
    for n in Z_ORDER:
        offs[n] = o
        o += sizes[n]
    total = -(-o // Z_TN) * Z_TN
    return sizes, src, offs, total


def _permute_w_in(w_in):
    sizes, src, _, total = _z_layout()
    cols = [w_in[:, src[n]:src[n] + sizes[n]] for n in Z_ORDER]
    w = jnp.concatenate(cols, axis=1)
    return jnp.pad(w, ((0, 0), (0, total - w.shape[1]))).astype(BF16)


def _merge_kernel(x_ref, oa_ref, od_ref, oh_ref, g0_ref, g1_ref, g2_ref, wb_ref, wo_ref, o_ref):
    m = jax.nn.sigmoid(g0_ref[...]) * _dot(oa_ref[...].astype(BF16), wb_ref[0:NSA_W, :])
    m += jax.nn.sigmoid(g1_ref[...]) * _dot(od_ref[...].astype(BF16), wb_ref[NSA_W:NSA_W + DN_V, :])
    m += jax.nn.sigmoid(g2_ref[...]) * _dot(oh_ref[...].astype(BF16), wb_ref[NSA_W + DN_V:MIX_W, :])
    o_ref[...] = x_ref[...] + _dot(m.astype(BF16), wo_ref[...])


def _merge(x, o_a, o_d, o_h, z, w_branch, w_out, tm):
    M, D = x.shape
    row = lambda w: pl.BlockSpec((tm, w), lambda i: (i, 0))
    return pl.pallas_call(
        _merge_kernel,
        grid=(M // tm,),
        in_specs=[
            row(D), row(NSA_W), row(DN_V), row(HG_WV),
            pl.BlockSpec((tm, D), lambda i: (i, 0)),
            pl.BlockSpec((tm, D), lambda i: (i, 1)),
            pl.BlockSpec((tm, D), lambda i: (i, 2)),
            pl.BlockSpec((MIX_W, D), lambda i: (0, 0)),
            pl.BlockSpec((D, D), lambda i: (0, 0)),
        ],
        out_specs=row(D),
        out_shape=jax.ShapeDtypeStruct((M, D), F32),
        compiler_params=_cparams(("parallel",), 40),
        name="merge",
    )(x, o_a, o_d, o_h, z, z, z, w_branch, w_out)


def _cmp_weights(pe, w1, w2):
    n_part = CMP_LEN // CMP_STRIDE
    eye = jnp.eye(NSA_KV, dtype=F32)
    eye2 = jnp.eye(2, dtype=F32)
    w1r = w1.reshape(2, n_part, CMP_STRIDE, HEAD_DIM, CMP_HIDDEN)
    w1e = jnp.einsum("kmrdh,kK,gG->mrkgdKGh", w1r, eye2, eye).reshape(n_part, CHUNK_W, CMP_HID_W)
    w2e = jnp.einsum("khd,kK,gG->kghKGd", w2, eye2, eye).reshape(CMP_HID_W, ROW_W)
    per = pe.reshape(2, n_part, CMP_STRIDE, HEAD_DIM).transpose(1, 2, 0, 3)
    pee = jnp.broadcast_to(per[:, :, :, None, :], (n_part, CMP_STRIDE, 2, NSA_KV, HEAD_DIM)).reshape(n_part, CHUNK_W)
    return pee, w1e.astype(BF16), w2e.astype(BF16)


def _compress(x, xnext_row, pe_ref, w1_ref, w2_ref, kg_ref):
    n = x.shape[0]
    p0 = _dot((x + pe_ref[0:1, :]).astype(BF16), w1_ref[0])
    p1 = _dot((x + pe_ref[1:2, :]).astype(BF16), w1_ref[1])
    p1s = pltpu.roll(p1, n - 1, 0)
    if xnext_row is not None:
        p1n = _dot((xnext_row + pe_ref[1:2, :]).astype(BF16), w1_ref[1])
        p1s = jnp.where(_iota((n, 1), 0) == n - 1, p1n[0:1, :], p1s)
    comp = _dot(_silu(p0 + p1s).astype(BF16), w2_ref[...])
    cks = []
    for g in range(NSA_KV):
        kc = comp[:, g * HEAD_DIM:(g + 1) * HEAD_DIM]
        cks.append(_rms(kc, kg_ref[...]))
    return cks, comp[:, KV_W:]


def _cmp_prompt_kernel(x_ref, pe_ref, w1_ref, w2_ref, kg_ref, ck_ref, cvt_ref):
    cks, cv = _compress(x_ref[0], None, pe_ref, w1_ref, w2_ref, kg_ref)
    for g in range(NSA_KV):
        ck_ref[0, g] = cks[g].astype(BF16)
    cvt = cv.T
    for g in range(NSA_KV):
        cvt_ref[0, g] = cvt[g * HEAD_DIM:(g + 1) * HEAD_DIM, :].astype(BF16)


def _cmp_prompt(cmp_rows, pee, w1e, w2e, kg0):
    B, T = cmp_rows.shape[:2]
    nc = T // CMP_STRIDE
    x = cmp_rows.reshape(B, nc, CHUNK_W)
    full = lambda s: pl.BlockSpec(s, lambda b: (0,) * len(s))
    return pl.pallas_call(
        _cmp_prompt_kernel,
        grid=(B,),
        in_specs=[pl.BlockSpec((1, nc, CHUNK_W), lambda b: (b, 0, 0)), full(pee.shape), full(w1e.shape),
                  full(w2e.shape), full((1, HEAD_DIM))],
        out_specs=[pl.BlockSpec((1, NSA_KV, nc, HEAD_DIM), lambda b: (b, 0, 0, 0)),
                   pl.BlockSpec((1, NSA_KV, HEAD_DIM, nc), lambda b: (b, 0, 0, 0))],
        out_shape=[jax.ShapeDtypeStruct((B, NSA_KV, nc, HEAD_DIM), BF16),
                   jax.ShapeDtypeStruct((B, NSA_KV, HEAD_DIM, nc), BF16)],
        compiler_params=_cparams(("parallel",), 48),
        name="nsa_compress",
    )(x, pee, w1e, w2e, kg0.reshape(1, HEAD_DIM))


def _softmax_first(s, vt):
    m = jnp.max(s, axis=0, keepdims=True)
    e = jnp.exp(s - m)
    return m, jnp.sum(e, axis=0, keepdims=True), _dot(vt, e.astype(BF16))


def _softmax_step(s, vt, carry):
    m, l, acc = carry
    m_new = jnp.maximum(m, jnp.max(s, axis=0, keepdims=True))
    alpha = jnp.exp(m - m_new)
    e = jnp.exp(s - m_new)
    l = alpha * l + jnp.sum(e, axis=0, keepdims=True)
    acc = alpha * acc + _dot(vt, e.astype(BF16))
    return m_new, l, acc


KC_W = 2 * LANE - HEAD_DIM
KC_POS = 64
KC_POS_RADIX = 128
KC_PAD = KC_POS + 2


def _key_features(T, front_pad=0):
    k = np.arange(T)
    f = np.zeros((front_pad + T, KC_W), np.float32)
    f[front_pad + k, k // SLC_BLOCK] = 1.0
    f[front_pad:, KC_POS] = k // KC_POS_RADIX
    f[front_pad:, KC_POS + 1] = k % KC_POS_RADIX
    f[:front_pad, KC_PAD] = 1.0
    return f


def _nsa_prompt_kernel(qt_ref, ck_ref, cvt_ref, sk_ref, svt_ref, wk_ref, wvt_ref, gt_ref, ovl_ref, o_ref,
                       *, n_cmp, n_blk):
    qb = pl.program_id(1)
    b0 = qb * Q_BLOCK
    QW = NSA_GROUP * Q_BLOCK
    TK = Q_BLOCK
    lane = _iota((1, QW), 1)
    tq = lane % Q_BLOCK
    q_pos = b0 + tq
    head = lane // Q_BLOCK
    qp1 = b0 + _iota((1, Q_BLOCK), 1)
    bj = _iota((n_blk, 1), 0)
    krel = _iota((TK, 1), 0) - tq
    frow = _iota((KC_W, 1), 0)
    k_diag = pl.multiple_of(b0, TK)
    gates = gt_ref[0, 0]
    qc_slcs, qc_wins, o_cmps = [], [], []
    for g in range(NSA_KV):
        slope = _alibi_slope(head, g)
        qgt = jnp.concatenate([qt_ref[0, NSA_GROUP * g + j] for j in range(NSA_GROUP)], axis=1)
        pos_rows = jnp.where(frow == KC_POS, slope * float(KC_POS_RADIX), jnp.where(frow == KC_POS + 1, slope, 0.0))

        dist = q_pos - (_iota((n_cmp, 1), 0) * CMP_STRIDE + (CMP_LEN - 1))
        mask = dist >= 0
        s = jnp.where(mask, _dot(ck_ref[0, g], qgt) - slope * dist.astype(F32), NEG_BIG)
        m = jnp.max(s, axis=0, keepdims=True)
        e = jnp.where(mask, jnp.exp(s - m), 0.0)
        p = (e / jnp.maximum(jnp.sum(e, axis=0, keepdims=True), 1e-30)).astype(BF16)
        o_cmps.append(_dot(cvt_ref[0, g], p))

        imp = _dot(ovl_ref[...], p[:, 0:Q_BLOCK])
        for j in range(1, NSA_GROUP):
            imp += _dot(ovl_ref[...], p[:, j * Q_BLOCK:(j + 1) * Q_BLOCK])
        cur = qp1 // SLC_BLOCK
        forced = (bj == 0) | (bj == cur) | (bj == cur - 1)
        imp = jnp.where(forced, SEL_BIG, jnp.where(bj * SLC_BLOCK <= qp1, imp, -SEL_BIG))
        tiles = [imp[8 * v:8 * v + 8, :] for v in range(n_blk // 8)]
        ranks = [jnp.zeros((8, Q_BLOCK), F32) for _ in tiles]
        for i in range(n_blk):
            row = imp[i:i + 1, :]
            for v, tile in enumerate(tiles):
                if 8 * v > i:
                    beats = row >= tile
                elif 8 * v + 8 <= i:
                    beats = row > tile
                else:
                    beats = (row > tile) | ((row == tile) & (bj[8 * v:8 * v + 8] > i))
                ranks[v] = ranks[v] + jnp.where(beats, 1.0, 0.0)
        rank = jnp.concatenate(ranks, axis=0)
        selb = jnp.where(rank < float(min(N_SEL, n_blk)), 0.0, NEG_BIG)
        selb = jnp.concatenate([selb] * NSA_GROUP, axis=1)
        if n_blk < KC_W:
            selb = jnp.concatenate([selb, jnp.zeros((KC_W - n_blk, QW), F32)], axis=0)
        qc_slcs.append(jnp.concatenate([qgt, (selb + pos_rows).astype(BF16)], axis=0))
        qc_wins.append(jnp.concatenate([qgt, jnp.where(frow == KC_PAD, NEG_BIG, pos_rows).astype(BF16)], axis=0))

    def scores(k_ref, g, k0, n, qcs):
        return _dot(k_ref[0, g, pl.ds(k0, n), :], qcs[g])

    WS = WINDOW + Q_BLOCK
    o_wins = []
    for g in range(NSA_KV):
        s = scores(wk_ref, g, k_diag, WS, qc_wins)
        s = jnp.concatenate([jnp.where(krel > 0, s[:TK], NEG_BIG), s[TK:WS - TK],
                             jnp.where(krel <= 0, s[WS - TK:], NEG_BIG)], axis=0)
        _, l, acc = _softmax_first(s, wvt_ref[0, g, :, pl.ds(k_diag, WS)])
        o_wins.append(acc / jnp.maximum(l, 1e-30))

    carry = []
    for g in range(NSA_KV):
        s = jnp.where(krel <= 0, scores(sk_ref, g, k_diag, TK, qc_slcs), NEG_BIG)
        carry.append(_softmax_first(s, svt_ref[0, g, :, pl.ds(k_diag, TK)]))

    def slc_body(n):
        def body(i, carry):
            k0 = pl.multiple_of(i * n, n)
            G = range(NSA_KV)
            ss = [scores(sk_ref, g, k0, n, qc_slcs) for g in G]
            ms = [jnp.maximum(carry[g][0], jnp.max(ss[g], axis=0, keepdims=True)) for g in G]
            es = [jnp.exp(ss[g] - ms[g]) for g in G]
            pvs = [_dot(svt_ref[0, g, :, pl.ds(k0, n)], es[g].astype(BF16)) for g in G]
            out = []
            for g in G:
                m, l, acc = carry[g]
                alpha = jnp.exp(m - ms[g])
                out.append((ms[g], alpha * l + jnp.sum(es[g], axis=0, keepdims=True), alpha * acc + pvs[g]))
            return tuple(out)
        return body

    pairs = qb // 2
    carry = lax.fori_loop(0, pairs, slc_body(2 * TK), tuple(carry))
    carry = lax.fori_loop(2 * pairs, qb, slc_body(TK), carry)

    outs = []
    for g in range(NSA_KV):
        _, l, acc = carry[g]
        o_slc = acc / jnp.maximum(l, 1e-30)
        outs.append(gates[g:g + 1] * o_cmps[g] + gates[NSA_KV + g:NSA_KV + g + 1] * o_slc
                    + gates[2 * NSA_KV + g:2 * NSA_KV + g + 1] * o_wins[g])

    for g in range(NSA_KV):
        for jp in range(NSA_GROUP // 2):
            pair = jnp.concatenate([outs[g][:, (2 * jp) * Q_BLOCK:(2 * jp + 1) * Q_BLOCK],
                                    outs[g][:, (2 * jp + 1) * Q_BLOCK:(2 * jp + 2) * Q_BLOCK]], axis=0)
            c0 = (NSA_GROUP * g + 2 * jp) * HEAD_DIM
            o_ref[0, :, c0:c0 + 2 * HEAD_DIM] = pair.T


def _overlap(n_cmp, n_blk):
    ci = np.arange(n_cmp)[:, None] * CMP_STRIDE
    bj = np.arange(n_blk)[None, :]
    return ((ci < (bj + 1) * SLC_BLOCK) & (ci + CMP_LEN > bj * SLC_BLOCK)).astype(np.float32)


def _nsa_prompt(qt, ck, cvt, sk, svt, wk, wvt, gates):
    B, _, _, T = qt.shape
    nq = T // Q_BLOCK
    n_cmp = ck.shape[2]
    n_blk = T // SLC_BLOCK
    assert n_blk <= KC_POS and T <= KC_POS_RADIX * 256
    ovl = jnp.asarray(_overlap(n_cmp, n_blk).T, BF16)
    feat = lambda f: jnp.broadcast_to(jnp.asarray(f, BF16), (B, NSA_KV) + f.shape)
    sk = jnp.concatenate([sk, feat(_key_features(T))], axis=-1)
    wk = jnp.pad(wk, ((0, 0), (0, 0), (WINDOW, 0), (0, 0)))
    wk = jnp.concatenate([wk, feat(_key_features(T, WINDOW))], axis=-1)
    wvt = jnp.pad(wvt, ((0, 0), (0, 0), (0, 0), (WINDOW, 0)))
    KW = HEAD_DIM + KC_W
    per_b = lambda s: pl.BlockSpec((1,) + s, lambda b, i: (b,) + (0,) * len(s))
    return pl.pallas_call(
        functools.partial(_nsa_prompt_kernel, n_cmp=n_cmp, n_blk=n_blk),
        grid=(B, nq),
        in_specs=[
            pl.BlockSpec((1, NSA_HEADS, HEAD_DIM, Q_BLOCK), lambda b, i: (b, 0, 0, i)),
            per_b((NSA_KV, n_cmp, HEAD_DIM)), per_b((NSA_KV, HEAD_DIM, n_cmp)),
            per_b((NSA_KV, T, KW)), per_b((NSA_KV, HEAD_DIM, T)),
            per_b((NSA_KV, WINDOW + T, KW)), per_b((NSA_KV, HEAD_DIM, WINDOW + T)),
            pl.BlockSpec((1, 1, 3 * NSA_KV, NSA_GROUP * Q_BLOCK), lambda b, i: (b, i, 0, 0)),
            pl.BlockSpec((n_blk, n_cmp), lambda b, i: (0, 0)),
        ],
        out_specs=pl.BlockSpec((1, Q_BLOCK, NSA_W), lambda b, i: (b, i, 0)),
        out_shape=jax.ShapeDtypeStruct((B, T, NSA_W), F32),
        compiler_params=_cparams(("parallel", "arbitrary"), 48),
        name="nsa_prompt",
    )(qt, ck, cvt, sk, svt, wk, wvt, gates, ovl)


def _nsa_decode_kernel(pt_ref, q_ref, new_ref, gt_ref, pe_ref, w1_ref, w2_ref, kg_ref, ovl_ref, exp_ref, *rest,
                       n_pages):
    del pt_ref
    cmp_pages = rest[:n_pages]
    slc_pages = rest[n_pages:2 * n_pages]
    win_ref, o_ref = rest[2 * n_pages], rest[2 * n_pages + 1]
    n_past = n_pages * PAGE_SIZE
    n_cmp = n_past // CMP_STRIDE
    q_pos = n_past
    new = new_ref[0]
    x = jnp.concatenate([r[0] for r in cmp_pages], axis=0)
    xnew = jnp.concatenate([new[:, 0:ROW_W], jnp.zeros((8, CHUNK_W - ROW_W), F32)], axis=1)
    cks, cv = _compress(x, xnew, pe_ref, w1_ref, w2_ref, kg_ref)

    jrow = _iota((8, 1), 0)
    lanes = _iota((1, LANE), 1)
    n_blk = (n_past + 1 + SLC_BLOCK - 1) // SLC_BLOCK
    cur = q_pos // SLC_BLOCK
    ii = _iota((LANE, LANE), 0)
    jj = _iota((LANE, LANE), 1)
    for g in range(NSA_KV):
        slope = _alibi_slope(jrow, g)
        qg = q_ref[0, g]
        lo, hi = g * HEAD_DIM, (g + 1) * HEAD_DIM

        dist = q_pos - (lanes * CMP_STRIDE + (CMP_LEN - 1))
        mask = dist >= 0
        s = jnp.where(mask, _dot_nt(qg, cks[g].astype(BF16)) - slope * dist.astype(F32), NEG_BIG)
        m = jnp.max(s, axis=1, keepdims=True)
        e = jnp.where(mask, jnp.exp(s - m), 0.0)
        p = (e / jnp.maximum(jnp.sum(e, axis=1, keepdims=True), 1e-30)).astype(BF16)
        o_cmp = _dot(p, cv[:, lo:hi].astype(BF16))

        imp = jnp.sum(jnp.where(jrow < NSA_GROUP, _dot(p, ovl_ref[...]), 0.0), axis=0, keepdims=True)
        forced = (lanes == 0) | (lanes == cur) | (lanes == cur - 1)
        imp = jnp.where(forced, SEL_BIG, jnp.where(lanes * SLC_BLOCK <= q_pos, imp, -SEL_BIG))
        imp = jnp.where(lanes < n_blk, imp, -3e38)
        impr = jnp.broadcast_to(imp, (LANE, LANE))
        impc = jnp.sum(jnp.where(ii == jj, impr, 0.0), axis=1, keepdims=True)
        beats = jnp.where(impc > impr, 1.0, jnp.where((impc == impr) & (ii < jj), 1.0, 0.0))
        rank = jnp.sum(beats, axis=0, keepdims=True)
        sel = jnp.where(rank < float(min(N_SEL, n_blk)), 1.0, 0.0)
        selk = _dot(jnp.broadcast_to(sel, (8, LANE)).astype(BF16), exp_ref[...])

        def attend(kmat, vmat, mask, dist, knew, vnew, mask_new):
            s = jnp.where(mask, _dot_nt(qg, kmat) - slope * dist.astype(F32), NEG_BIG)
            s_new = jnp.sum(qg.astype(F32) * knew.astype(BF16).astype(F32), axis=1, keepdims=True)
            if mask_new is not None:
                s_new = jnp.where(mask_new, s_new, NEG_BIG)
            m = jnp.maximum(jnp.max(s, axis=1, keepdims=True), s_new)
            e = jnp.where(mask, jnp.exp(s - m), 0.0)
            e_new = jnp.exp(s_new - m)
            if mask_new is not None:
                e_new = jnp.where(mask_new, e_new, 0.0)
            l = jnp.sum(e, axis=1, keepdims=True) + e_new
            o = _dot(e.astype(BF16), vmat) + e_new.astype(BF16).astype(F32) * vnew.astype(BF16).astype(F32)
            return o / jnp.maximum(l, 1e-30)

        kpos = _iota((1, n_past), 1)
        sk = jnp.concatenate([r[0][:, lo:hi] for r in slc_pages], axis=0).astype(BF16)
        sv = jnp.concatenate([r[0][:, KV_W + lo:KV_W + hi] for r in slc_pages], axis=0).astype(BF16)
        o_slc = attend(sk, sv, selk > 0.5, q_pos - kpos, new[1:2, lo:hi], new[1:2, KV_W + lo:KV_W + hi],
                       sel[:, cur:cur + 1] > 0.5)

        n_win = win_ref.shape[1]
        wpos = (q_pos - n_win) + _iota((1, n_win), 1)
        dist_w = q_pos - wpos
        wrows = win_ref[0]
        o_win = attend(wrows[:, lo:hi].astype(BF16), wrows[:, KV_W + lo:KV_W + hi].astype(BF16),
                       (dist_w < WINDOW) & (wpos >= 0), dist_w, new[2:3, lo:hi], new[2:3, KV_W + lo:KV_W + hi],
                       None)

        gt = gt_ref[0, g]
        o_ref[0, g] = gt[:, 0:1] * o_cmp + gt[:, 1:2] * o_slc + gt[:, 2:3] * o_win


def _nsa_decode(page_table, q, new_rows, gates, cmp_pool, slc_pool, win, pee, w1e, w2e, kg0):
    B, n_pages = page_table.shape
    n_past = n_pages * PAGE_SIZE
    n_cmp = n_past // CMP_STRIDE
    assert n_cmp == LANE and win.shape[1] <= n_past
    n_blk = (n_past + 1 + SLC_BLOCK - 1) // SLC_BLOCK
    ovl = np.zeros((n_cmp, LANE), np.float32)
    ovl[:, :n_blk] = _overlap(n_cmp, n_blk)
    expand = (np.arange(n_past)[None, :] // SLC_BLOCK == np.arange(LANE)[:, None]).astype(np.float32)
    chunks_per_page = PAGE_SIZE // CMP_STRIDE
    cmp_chunks = cmp_pool.reshape(cmp_pool.shape[0], chunks_per_page, CHUNK_W)
    full = lambda s: pl.BlockSpec(s, lambda b, pt: (0,) * len(s))
    per_b = lambda s: pl.BlockSpec((1,) + s, lambda b, pt: (b,) + (0,) * len(s))

    def page_spec(shape, p):
        return pl.BlockSpec((1,) + shape, lambda b, pt: (pt[b, p], 0, 0))

    in_specs = [per_b((NSA_KV, 8, HEAD_DIM)), per_b((8, ROW_W)), per_b((NSA_KV, 8, LANE)), full(pee.shape),
                full(w1e.shape), full(w2e.shape), full((1, HEAD_DIM)), full(ovl.shape), full(expand.shape)]
    in_specs += [page_spec((chunks_per_page, CHUNK_W), p) for p in range(n_pages)]
    in_specs += [page_spec((PAGE_SIZE, ROW_W), p) for p in range(n_pages)]
    in_specs += [per_b(win.shape[1:])]
    return pl.pallas_call(
        functools.partial(_nsa_decode_kernel, n_pages=n_pages),
        grid_spec=pltpu.PrefetchScalarGridSpec(
            num_scalar_prefetch=1, grid=(B,), in_specs=in_specs,
            out_specs=pl.BlockSpec((1, NSA_KV, 8, HEAD_DIM), lambda b, pt: (b, 0, 0, 0))),
        out_shape=jax.ShapeDtypeStruct((B, NSA_KV, 8, HEAD_DIM), F32),
        compiler_params=_cparams(("arbitrary",), 48),
        name="nsa_decode",
    )(page_table, q, new_rows, gates, pee, w1e, w2e, kg0.reshape(1, HEAD_DIM), jnp.asarray(ovl, BF16),
      jnp.asarray(expand, BF16), *([cmp_chunks] * n_pages), *([slc_pool] * n_pages), win)


def _dn_prompt_kernel(q_ref, k_ref, v_ref, aux_ref, o_ref, s_out_ref, s_ref, *, n_chunk):
    c = pl.program_id(1)

    @pl.when(c == 0)
    def _():
        s_ref[...] = jnp.zeros_like(s_ref)

    C = DN_CHUNK
    ii = _iota((C, C), 0)
    jj = _iota((C, C), 1)
    incl = ii >= jj
    eye = (ii == jj).astype(F32)
    aux = aux_ref[0]
    for h in range(DN_HEADS):
        cs = slice(h * DN_DK, (h + 1) * DN_DK)
        q, k, v = q_ref[0, :, cs], k_ref[0, :, cs], v_ref[0, :, cs]
        gcol = aux[:, h:h + 1]
        bcol = aux[:, DN_HEADS + h:DN_HEADS + h + 1]
        grow = jnp.sum(jnp.where(ii <= jj, gcol, 0.0), axis=0, keepdims=True)
        gcum = jnp.sum(jnp.where(ii == jj, grow, 0.0), axis=1, keepdims=True)
        decay = jnp.where(incl, jnp.exp(jnp.where(incl, gcum - grow, 0.0)), 0.0)
        kb = k.astype(BF16)
        lower = jnp.where(ii > jj, bcol * _dot_nt(kb, kb) * decay, 0.0)
        npow = -lower
        tinv = eye + npow
        for _ in range(int(np.log2(C)) - 1):
            npow = _dot3(npow, npow)
            tinv = tinv + _dot3(tinv, npow)
        tb = tinv.astype(BF16)
        eg = jnp.exp(gcum)
        u = _dot(tb, (v * bcol).astype(BF16))
        w = _dot(tb, (k * (bcol * eg)).astype(BF16))
        S = s_ref[h]
        Sb = S.astype(BF16)
        v_new = u - _dot(w.astype(BF16), Sb)
        vnb = v_new.astype(BF16)
        attn = _dot_nt(q.astype(BF16), kb) * decay
        o_ref[0, :, cs] = _dot((q * eg).astype(BF16), Sb) + _dot(attn.astype(BF16), vnb)
        g_last = gcum[C - 1:C, :]
        s_ref[h] = S * jnp.exp(g_last) + _dot_tn((k * jnp.exp(g_last - gcum)).astype(BF16), vnb)

    @pl.when(c == n_chunk - 1)
    def _():
        s_out_ref[0] = s_ref[...]


def _dn_prompt(q, k, v, aux):
    B, T, _ = q.shape
    n_chunk = T // DN_CHUNK
    blk = lambda w: pl.BlockSpec((1, DN_CHUNK, w), lambda b, c: (b, c, 0))
    return pl.pallas_call(
        functools.partial(_dn_prompt_kernel, n_chunk=n_chunk),
        grid=(B, n_chunk),
        in_specs=[blk(DN_QK), blk(DN_QK), blk(DN_V), blk(2 * DN_HEADS)],
        out_specs=[blk(DN_V), pl.BlockSpec((1, DN_HEADS, DN_DK, DN_DV), lambda b, c: (b, 0, 0, 0))],
        out_shape=[jax.ShapeDtypeStruct((B, T, DN_V), F32), jax.ShapeDtypeStruct((B, DN_HEADS, DN_DK, DN_DV), F32)],
        scratch_shapes=[pltpu.VMEM((DN_HEADS, DN_DK, DN_DV), F32)],
        compiler_params=_cparams(("parallel", "arbitrary"), 32),
        name="dn_prompt",
    )(q, k, v, aux)


def _row0(x):
    return jnp.where(_iota(x.shape, 0) == 0, x, 0.0)


def _dn_decode_kernel(vec_ref, s_ref, o_ref, s_out_ref, *, bb):
    def body(b, _):
        for h in range(DN_HEADS):
            x = vec_ref[b, h]
            S = s_ref[b, h]
            xs = _dot(x.astype(BF16), S.astype(BF16))
            k, q, v, eg, beta = x[0:1], x[1:2], x[2:3], x[3:4], x[4:5]
            v_new = beta * (v - eg * xs[0:1])
            kb = k.astype(BF16).astype(F32)
            qk = jnp.sum(q.astype(BF16).astype(F32) * kb, axis=1, keepdims=True)
            o = eg * xs[1:2] + qk.astype(BF16).astype(F32) * v_new.astype(BF16).astype(F32)
            o_ref[b, h] = jnp.broadcast_to(o, (8, DN_DV))
            s_out_ref[b, h] = S * eg[:, 0:1] + _dot_tn(_row0(x).astype(BF16),
                                                      _row0(jnp.broadcast_to(v_new, (8, DN_DV))).astype(BF16))
        return 0

    lax.fori_loop(0, bb, body, 0)


def _dn_decode(vec, S0, bb=8):
    B = vec.shape[0]
    spec_v = pl.BlockSpec((bb, DN_HEADS, 8, DN_DK), lambda i: (i, 0, 0, 0))
    spec_s = pl.BlockSpec((bb, DN_HEADS, DN_DK, DN_DV), lambda i: (i, 0, 0, 0))
    return pl.pallas_call(
        functools.partial(_dn_decode_kernel, bb=bb),
        grid=(B // bb,),
        in_specs=[spec_v, spec_s],
        out_specs=[spec_v, spec_s],
        out_shape=[jax.ShapeDtypeStruct(vec.shape, F32), jax.ShapeDtypeStruct(S0.shape, F32)],
        compiler_params=_cparams(("parallel",), 32),
        name="dn_decode",
    )(vec, S0)


def _hg_prompt_kernel(q_ref, k_ref, v_ref, lf_ref, tril_ref, o_ref, s_out_ref, st_ref, g_ref, *, n_step, tb):
    t = pl.program_id(1)

    @pl.when(t == 0)
    def _():
        st_ref[...] = jnp.zeros_like(st_ref)

    lf = lf_ref[0]
    hi = lf.astype(BF16)
    r1 = lf - hi.astype(F32)
    mid = r1.astype(BF16)
    lo = (r1 - mid.astype(F32)).astype(BF16)
    tril = tril_ref[...]
    g_ref[...] = _dot(tril, hi) + (_dot(tril, mid) + _dot(tril, lo))
    C = HG_CHUNK
    sidx = _iota((C, 1), 0)

    def sub(sc, _):
        r0 = pl.multiple_of(sc * C, C)
        for h in range(HG_HEADS):
            cs = slice(h * HG_DK, (h + 1) * HG_DK)
            q = q_ref[0, pl.ds(r0, C), cs]
            k = k_ref[0, pl.ds(r0, C), cs]
            v = v_ref[0, pl.ds(r0, C), cs]
            G = g_ref[pl.ds(r0, C), cs]
            st = st_ref[h]
            cols = []
            for tt in range(C):
                msk = sidx <= tt
                dec = jnp.where(msk, jnp.exp(jnp.where(msk, G[tt:tt + 1, :] - G, 0.0)), 0.0)
                cols.append(jnp.sum(dec * k * q[tt:tt + 1, :], axis=1, keepdims=True))
            a_t = jnp.concatenate(cols, axis=1)
            vb = v.astype(BF16)
            o_ref[0, pl.ds(r0, C), cs] = (_dot_nt((q * jnp.exp(G)).astype(BF16), st.astype(BF16))
                                          + _dot_tn(a_t.astype(BF16), vb))
            g_last = G[C - 1:C, :]
            st_ref[h] = st * jnp.exp(g_last) + _dot_tn(vb, (k * jnp.exp(g_last - G)).astype(BF16))
        return 0

    lax.fori_loop(0, tb // C, sub, 0)

    @pl.when(t == n_step - 1)
    def _():
        for h in range(HG_HEADS):
            s_out_ref[0, h] = st_ref[h].T


def _hg_prompt(q, k, v, logf, tb=128):
    B, T, _ = q.shape
    n_step = T // tb
    r = np.arange(tb)
    tril = ((r[:, None] >= r[None, :]) & (r[:, None] // HG_CHUNK == r[None, :] // HG_CHUNK)).astype(np.float32)
    blk = pl.BlockSpec((1, tb, HG_WK), lambda b, t: (b, t, 0))
    return pl.pallas_call(
        functools.partial(_hg_prompt_kernel, n_step=n_step, tb=tb),
        grid=(B, n_step),
        in_specs=[blk, blk, blk, blk, pl.BlockSpec((tb, tb), lambda b, t: (0, 0))],
        out_specs=[blk, pl.BlockSpec((1, HG_HEADS, HG_DK, HG_DV), lambda b, t: (b, 0, 0, 0))],
        out_shape=[jax.ShapeDtypeStruct((B, T, HG_WV), F32), jax.ShapeDtypeStruct((B, HG_HEADS, HG_DK, HG_DV), F32)],
        scratch_shapes=[pltpu.VMEM((HG_HEADS, HG_DV, HG_DK), F32), pltpu.VMEM((tb, HG_WK), F32)],
        compiler_params=_cparams(("parallel", "arbitrary"), 32),
        name="hg_prompt",
    )(q, k, v, logf, jnp.asarray(tril, BF16))


def _hg_decode_kernel(vec_ref, s_ref, o_ref, s_out_ref, *, bb):
    ii = _iota((HG_DK, HG_DK), 0)
    jj = _iota((HG_DK, HG_DK), 1)

    def body(b, _):
        for h in range(HG_HEADS):
            x = vec_ref[b, h]
            S = s_ref[b, h]
            k, q, v, lf = x[0:1], x[1:2], x[2:3], x[3:4]
            f = jnp.exp(lf)
            qs = _dot(jnp.broadcast_to(q * f, (8, HG_DK)).astype(BF16), S.astype(BF16))
            a = jnp.sum(q * k, axis=1, keepdims=True)
            o = qs[0:1] + a.astype(BF16).astype(F32) * v.astype(BF16).astype(F32)
            o_ref[b, h] = jnp.broadcast_to(o, (8, HG_DV))
            fcol = jnp.sum(jnp.where(ii == jj, f, 0.0), axis=1, keepdims=True)
            s_out_ref[b, h] = S * fcol + _dot_tn(_row0(x).astype(BF16),
                                                 _row0(jnp.broadcast_to(v, (8, HG_DV))).astype(BF16))
        return 0

    lax.fori_loop(0, bb, body, 0)


def _hg_decode(vec, S0, bb=8):
    B = vec.shape[0]
    spec_v = pl.BlockSpec((bb, HG_HEADS, 8, HG_DK), lambda i: (i, 0, 0, 0))
    spec_s = pl.BlockSpec((bb, HG_HEADS, HG_DK, HG_DV), lambda i: (i, 0, 0, 0))
    return pl.pallas_call(
        functools.partial(_hg_decode_kernel, bb=bb),
        grid=(B // bb,),
        in_specs=[spec_v, spec_s],
        out_specs=[spec_v, spec_s],
        out_shape=[jax.ShapeDtypeStruct(vec.shape, F32), jax.ShapeDtypeStruct(S0.shape, F32)],
        compiler_params=_cparams(("parallel",), 32),
        name="hg_decode",
    )(vec, S0)


def _head_rms(x, g):
    return x * lax.rsqrt(jnp.mean(x * x, axis=-1, keepdims=True) + NORM_EPS) * g


def _l2n(x):
    return x * lax.rsqrt(jnp.sum(x * x, axis=-1, keepdims=True) + L2_EPS)


def _rows8(rows):
    x = jnp.stack(rows, axis=-2)
    pad = [(0, 0)] * x.ndim
    pad[-2] = (0, 8 - len(rows))
    return jnp.pad(x, pad)


def _layer(x, l, prm, wts, past, page_table):
    B, T, D = x.shape
    M = B * T
    tm = 512 if M >= 512 else M
    x2 = x.reshape(M, D)
    x2 = _ffn(x2, prm["ffn1_norm"][l], wts["ffn1_w_gu"][l], wts["ffn1_w_down"][l], tm)
    z = _inproj(x2, prm["mix_norm"][l], wts["w_in"][l], tm)
    sizes, _, offs, _ = _z_layout()
    zs = {n: z[:, offs[n]:offs[n] + sizes[n]].reshape(B, T, sizes[n]) for n in Z_ORDER if n != "m_g"}

    q = _head_rms(zs["a_q"].reshape(B, T, NSA_HEADS, HEAD_DIM), prm["nsa_q_norm"][l]) * (HEAD_DIM ** -0.5)
    kv = zs["a_kv"].reshape(B, T, 3, 2, NSA_KV, HEAD_DIM)
    kg = prm["nsa_k_norm"][l]
    cmp_new = kv[:, :, 0]
    slc_new = jnp.stack([_head_rms(kv[:, :, 1, 0], kg[1]), kv[:, :, 1, 1]], axis=2)
    win_new = jnp.stack([_head_rms(kv[:, :, 2, 0], kg[2]), kv[:, :, 2, 1]], axis=2)
    gates = jax.nn.sigmoid(zs["a_g"].reshape(B, T, NSA_HEADS, 3))
    pee, w1e, w2e = wts["cmp"][l]
    if past is None:
        nq = T // Q_BLOCK
        ck, cvt = _cmp_prompt(cmp_new.reshape(B, T, ROW_W), pee, w1e, w2e, kg[0])
        to_k = lambda a: a.transpose(0, 2, 1, 3).astype(BF16)
        to_vt = lambda a: a.transpose(0, 2, 3, 1).astype(BF16)
        gt = gates.reshape(B, nq, Q_BLOCK, NSA_KV, NSA_GROUP, 3).transpose(0, 1, 5, 3, 4, 2)
        gt = gt.reshape(B, nq, 3 * NSA_KV, NSA_GROUP * Q_BLOCK)
        o_a = _nsa_prompt(q.transpose(0, 2, 3, 1).astype(BF16), ck, cvt, to_k(slc_new[:, :, 0]),
                          to_vt(slc_new[:, :, 1]), to_k(win_new[:, :, 0]), to_vt(win_new[:, :, 1]), gt)
        win_state = win_new[:, T - min(WINDOW, T):]
    else:
        qd = q.reshape(B, NSA_KV, NSA_GROUP, HEAD_DIM)
        qd = jnp.pad(qd, ((0, 0), (0, 0), (0, 8 - NSA_GROUP), (0, 0))).astype(BF16)
        new_rows = _rows8([cmp_new.reshape(B, ROW_W), slc_new.reshape(B, ROW_W), win_new.reshape(B, ROW_W)])
        gd = gates.reshape(B, NSA_KV, NSA_GROUP, 3)
        gd = jnp.pad(gd, ((0, 0), (0, 0), (0, 8 - NSA_GROUP), (0, LANE - 3)))
        n_pool = past["cmp"].shape[0]
        o8 = _nsa_decode(page_table, qd, new_rows, gd, past["cmp"].reshape(n_pool, PAGE_SIZE, ROW_W),
                         past["slc"].reshape(n_pool, PAGE_SIZE, ROW_W),
                         past["win"].reshape(B, -1, ROW_W), pee, w1e, w2e, kg[0])
        o_a = o8[:, :, :NSA_GROUP].reshape(B, T, NSA_W)
        win_state = win_new

    d_qkv = zs["d_qkv"]
    buf = jnp.zeros((B, CONV_W - 1, DN_QKV), F32) if past is None else past["conv"]
    xx = jnp.concatenate([buf, d_qkv], axis=1)
    cw = prm["dn_conv_w"][l]
    qkv = sum(xx[:, j:j + T] * cw[j] for j in range(CONV_W))
    conv_state = xx[:, -(CONV_W - 1):]
    dq, dk, dv = jnp.split(jax.nn.silu(qkv), [DN_QK, 2 * DN_QK], axis=-1)
    dq = _l2n(dq.reshape(B, T, DN_HEADS, DN_DK)) * (DN_DK ** -0.5)
    dk = _l2n(dk.reshape(B, T, DN_HEADS, DN_DK))
    beta = jax.nn.sigmoid(zs["d_b"])
    g = -jnp.exp(prm["dn_A_log"][l]) * jax.nn.softplus(zs["d_a"] + prm["dn_dt_bias"][l])
    if past is None:
        o_d, dn_state = _dn_prompt(dq.reshape(B, T, DN_QK), dk.reshape(B, T, DN_QK), dv,
                                   jnp.concatenate([g, beta], axis=-1))
    else:
        lanes = lambda a: jnp.broadcast_to(a[:, 0, :, None], (B, DN_HEADS, DN_DK))
        vec = _rows8([dk[:, 0], dq[:, 0], dv.reshape(B, DN_HEADS, DN_DV), lanes(jnp.exp(g)), lanes(beta)])
        o8, dn_state = _dn_decode(vec, past["dn_S"])
        o_d = o8[:, :, 0].reshape(B, T, DN_V)
    o_d = _head_rms(o_d.reshape(B, T, DN_HEADS, DN_DV), prm["dn_out_norm"][l]) * jax.nn.silu(
        zs["d_z"].reshape(B, T, DN_HEADS, DN_DV))

    p = jax.nn.softmax(prm["hg_lb_logits"], axis=0)
    lb = (jnp.cumsum(p, axis=0) - p[0])[l]
    zf = zs["r_f"]
    logf = jnp.log(lb + (1.0 - lb) * jax.nn.sigmoid(zf))
    k_in = (1.0 - lb) * jax.nn.sigmoid(-zf)
    hq = jax.nn.silu(zs["r_q"]) * (HG_DK ** -0.5)
    if past is None:
        o_h, hg_state = _hg_prompt(hq, k_in, zs["r_i"], logf)
    else:
        hd = lambda a: a.reshape(B, HG_HEADS, HG_DK)
        vec = _rows8([hd(k_in), hd(hq), hd(zs["r_i"]), hd(logf)])
        o8, hg_state = _hg_decode(vec, past["hg_S"])
        o_h = o8[:, :, 0].reshape(B, T, HG_WV)
    o_h = _head_rms(o_h.reshape(B, T, HG_HEADS, HG_DV), prm["hg_out_norm"][l]) * jax.nn.sigmoid(
        zs["r_og"].reshape(B, T, HG_HEADS, HG_DV))

    x2 = _merge(x2, o_a.reshape(M, NSA_W), o_d.reshape(M, DN_V), o_h.reshape(M, HG_WV), z, wts["w_branch"][l],
                wts["w_out"][l], tm)
    x2 = _ffn(x2, prm["ffn2_norm"][l], wts["ffn2_w_gu"][l], wts["ffn2_w_down"][l], tm)
    return x2.reshape(B, T, D), (cmp_new, slc_new, win_state, conv_state, dn_state, hg_state)


def _trunk(x, prm, wts, caches, page_table):
    new = []
    for l in range(DEPTH):
        past = None
        if caches is not None:
            past = {k: v[l] for k, v in caches.items()}
        x, st = _layer(x, l, prm, wts, past, page_table)
        new.append(st)
    return x, [jnp.stack([s[i] for s in new], axis=0) for i in range(6)]


def kernel(x_prompt, x_sample, cache_cmp_kv, cache_slc_kv, cache_win_kv, state_dn_conv, state_dn_S, state_hg_S,
           page_table, ffn1_norm, ffn1_w_gu, ffn1_w_down, mix_norm, w_in, nsa_q_norm, nsa_k_norm, nsa_cmp_pe,
           nsa_cmp_w1, nsa_cmp_w2, dn_conv_w, dn_A_log, dn_dt_bias, dn_out_norm, hg_lb_logits, hg_out_norm,
           w_branch, w_out, ffn2_norm, ffn2_w_gu, ffn2_w_down):
    prm = dict(ffn1_norm=ffn1_norm, mix_norm=mix_norm, nsa_q_norm=nsa_q_norm, nsa_k_norm=nsa_k_norm,
               dn_conv_w=dn_conv_w, dn_A_log=dn_A_log, dn_dt_bias=dn_dt_bias, dn_out_norm=dn_out_norm,
               hg_lb_logits=hg_lb_logits, hg_out_norm=hg_out_norm, ffn2_norm=ffn2_norm)
    bf = lambda w: w.astype(BF16)
    wts = dict(ffn1_w_gu=bf(ffn1_w_gu), ffn1_w_down=bf(ffn1_w_down), ffn2_w_gu=bf(ffn2_w_gu),
               ffn2_w_down=bf(ffn2_w_down), w_branch=bf(w_branch), w_out=bf(w_out),
               w_in=jnp.stack([_permute_w_in(w_in[l]) for l in range(DEPTH)]),
               cmp=[_cmp_weights(nsa_cmp_pe[l], nsa_cmp_w1[l], nsa_cmp_w2[l]) for l in range(DEPTH)])
    y_p, (p_cmp, p_slc, p_win, p_conv, p_dn, p_hg) = _trunk(x_prompt, prm, wts, None, None)
    caches = dict(cmp=cache_cmp_kv, slc=cache_slc_kv, win=cache_win_kv, conv=state_dn_conv, dn_S=state_dn_S,
                  hg_S=state_hg_S)
    y_s, (s_cmp, s_slc, s_win, s_conv, s_dn, s_hg) = _trunk(x_sample, prm, wts, caches, page_table)
    return (y_p, y_s, p_cmp, s_cmp, p_slc, s_slc, p_win, s_win, p_conv, s_conv, p_dn, s_dn, p_hg, s_hg)
```

```python
import functools

import jax
import jax.numpy as jnp
import numpy as np
from jax import lax
from jax.experimental import pallas as pl
from jax.experimental.pallas import tpu as pltpu

F32 = jnp.float32
BF16 = jnp.bfloat16

D_MODEL = 1024
DEPTH = 2
PAST_LEN = 2048
PAGE_SIZE = 128
HEAD_DIM = 64
NSA_HEADS = 8
NSA_KV = 2
NSA_GROUP = NSA_HEADS // NSA_KV
CMP_LEN = 32
CMP_STRIDE = 16
CMP_HIDDEN = 128
SLC_BLOCK = 64
N_SEL = 16
WINDOW = 512
Q_BLOCK = 128
DN_HEADS = 4
DN_DK = 128
DN_DV = 128
DN_CHUNK = 64
CONV_W = 4
HG_HEADS = 4
HG_DK = 128
HG_DV = 128
HG_CHUNK = 16
D_FF = 2816
NORM_EPS = 1e-6
L2_EPS = 1e-6
NEG_BIG = -1e30
SEL_BIG = 1e9

NSA_W = NSA_HEADS * HEAD_DIM
KV_W = NSA_KV * HEAD_DIM
ROW_W = 2 * KV_W
DN_QK = DN_HEADS * DN_DK
DN_V = DN_HEADS * DN_DV
DN_QKV = 2 * DN_QK + DN_V
HG_WK = HG_HEADS * HG_DK
HG_WV = HG_HEADS * HG_DV
MIX_W = NSA_W + DN_V + HG_WV
IN_SPLITS = (NSA_W, 6 * KV_W, 3 * NSA_HEADS, DN_QKV, DN_HEADS, DN_HEADS, DN_V, HG_WK, HG_WV, HG_WK, HG_WV, 3 * D_MODEL)
IN_NAMES = ("a_q", "a_kv", "a_g", "d_qkv", "d_b", "d_a", "d_z", "r_f", "r_i", "r_q", "r_og", "m_g")
Z_ORDER = ("m_g", "d_qkv", "d_z", "r_f", "r_i", "r_q", "r_og", "a_q", "a_kv", "a_g", "d_b", "d_a")
Z_TN = 512
CHUNK_W = CMP_STRIDE * ROW_W
CMP_HID_W = 2 * NSA_KV * CMP_HIDDEN

V7X_VMEM_BYTES = 64 * 1024 * 1024
LANE = 128
ROW_TILE = 512
ROW_TILE_STREAMED = 1024


def _cparams(sem, vmem_mb):
    assert vmem_mb * 1024 * 1024 < V7X_VMEM_BYTES
    return pltpu.CompilerParams(dimension_semantics=sem, vmem_limit_bytes=vmem_mb * 1024 * 1024)


def _dot(a, b):
    return jnp.dot(a, b, preferred_element_type=F32)


def _dot_nt(a, b):
    return lax.dot_general(a, b, (((1,), (1,)), ((), ())), preferred_element_type=F32)


def _dot_tn(a, b):
    return lax.dot_general(a, b, (((0,), (0,)), ((), ())), preferred_element_type=F32)


def _split2(a):
    hi = a.astype(BF16)
    lo = (a - hi.astype(F32)).astype(BF16)
    return hi, lo


def _dot3(a, b):
    ah, al = _split2(a)
    bh, bl = _split2(b)
    return _dot(ah, bh) + (_dot(ah, bl) + _dot(al, bh))


def _rms(x, g):
    return x * lax.rsqrt(jnp.mean(x * x, axis=-1, keepdims=True) + NORM_EPS) * g


def _silu(x):
    return x * jax.nn.sigmoid(x)


def _iota(shape, dim):
    return lax.broadcasted_iota(jnp.int32, shape, dim)


def _alibi_slope(head_in_group, g):
    out = jnp.full(head_in_group.shape, 2.0 ** -(NSA_GROUP * g + NSA_GROUP), F32)
    for j in range(NSA_GROUP - 2, -1, -1):
        out = jnp.where(head_in_group == j, 2.0 ** -(NSA_GROUP * g + j + 1), out)
    return out


def _ffn_kernel(x_ref, g_ref, wg_ref, wu_ref, wd_ref, o_ref, xn_ref, acc_ref, *, nf):
    j = pl.program_id(1)

    @pl.when(j == 0)
    def _():
        xn_ref[...] = _rms(x_ref[...], g_ref[...]).astype(BF16)
        acc_ref[...] = jnp.zeros_like(acc_ref)

    xn = xn_ref[...]
    a = _silu(_dot(xn, wg_ref[...])) * _dot(xn, wu_ref[...])
    acc_ref[...] += _dot(a.astype(BF16), wd_ref[...])

    @pl.when(j == nf - 1)
    def _():
        o_ref[...] = x_ref[...] + 0.5 * acc_ref[...]


def _ffn(x, gain, w_gu, w_down, tm, tf=256):
    M, D = x.shape
    F = w_down.shape[0]
    nf = F // tf
    return pl.pallas_call(
        functools.partial(_ffn_kernel, nf=nf),
        grid=(M // tm, nf),
        in_specs=[
            pl.BlockSpec((tm, D), lambda i, j: (i, 0)),
            pl.BlockSpec((1, D), lambda i, j: (0, 0)),
            pl.BlockSpec((D, tf), lambda i, j: (0, j)),
            pl.BlockSpec((D, tf), lambda i, j: (0, j + nf)),
            pl.BlockSpec((tf, D), lambda i, j: (j, 0)),
        ],
        out_specs=pl.BlockSpec((tm, D), lambda i, j: (i, 0)),
        out_shape=jax.ShapeDtypeStruct((M, D), F32),
        scratch_shapes=[pltpu.VMEM((tm, D), BF16), pltpu.VMEM((tm, D), F32)],
        compiler_params=_cparams(("parallel", "arbitrary"), 40),
        name="ffn",
    )(x, gain.reshape(1, D), w_gu, w_gu, w_down)


def _inproj_kernel(x_ref, g_ref, w_ref, z_ref, xn_ref):
    @pl.when(pl.program_id(1) == 0)
    def _():
        xn_ref[...] = _rms(x_ref[...], g_ref[...]).astype(BF16)

    z_ref[...] = _dot(xn_ref[...], w_ref[...])


def _inproj(x, gain, w, tm):
    M, D = x.shape
    N = w.shape[1]
    return pl.pallas_call(
        _inproj_kernel,
        grid=(M // tm, N // Z_TN),
        in_specs=[
            pl.BlockSpec((tm, D), lambda i, j: (i, 0)),
            pl.BlockSpec((1, D), lambda i, j: (0, 0)),
            pl.BlockSpec((D, Z_TN), lambda i, j: (0, j)),
        ],
        out_specs=pl.BlockSpec((tm, Z_TN), lambda i, j: (i, j)),
        out_shape=jax.ShapeDtypeStruct((M, N), F32),
        scratch_shapes=[pltpu.VMEM((tm, D), BF16)],
        compiler_params=_cparams(("parallel", "arbitrary"), 40),
        name="inproj",
    )(x, gain.reshape(1, D), w)


def _z_layout():
    sizes = dict(zip(IN_NAMES, IN_SPLITS))
    src = dict(zip(IN_NAMES, np.cumsum((0,) + IN_SPLITS[:-1]).tolist()))
    offs, o = {}, 0
    for n in Z_ORDER:
        offs[n] = o
        o += sizes[n]
    total = -(-o // Z_TN) * Z_TN
    return sizes, src, offs, total


def _permute_w_in(w_in):
    sizes, src, _, total = _z_layout()
    cols = [w_in[:, src[n]:src[n] + sizes[n]] for n in Z_ORDER]
    w = jnp.concatenate(cols, axis=1)
    return jnp.pad(w, ((0, 0), (0, total - w.shape[1]))).astype(BF16)


def _merge_kernel(x_ref, oa_ref, od_ref, oh_ref, g0_ref, g1_ref, g2_ref, wb_ref, wo_ref, o_ref):
    m = jax.nn.sigmoid(g0_ref[...]) * _dot(oa_ref[...].astype(BF16), wb_ref[0:NSA_W, :])
    m += jax.nn.sigmoid(g1_ref[...]) * _dot(od_ref[...].astype(BF16), wb_ref[NSA_W:NSA_W + DN_V, :])
    m += jax.nn.sigmoid(g2_ref[...]) * _dot(oh_ref[...].astype(BF16), wb_ref[NSA_W + DN_V:MIX_W, :])
    o_ref[...] = x_ref[...] + _dot(m.astype(BF16), wo_ref[...])


def _merge(x, o_a, o_d, o_h, z, w_branch, w_out, tm):
    M, D = x.shape
    row = lambda w: pl.BlockSpec((tm, w), lambda i: (i, 0))
    return pl.pallas_call(
        _merge_kernel,
        grid=(M // tm,),
        in_specs=[
            row(D), row(NSA_W), row(DN_V), row(HG_WV),
            pl.BlockSpec((tm, D), lambda i: (i, 0)),
            pl.BlockSpec((tm, D), lambda i: (i, 1)),
            pl.BlockSpec((tm, D), lambda i: (i, 2)),
            pl.BlockSpec((MIX_W, D), lambda i: (0, 0)),
            pl.BlockSpec((D, D), lambda i: (0, 0)),
        ],
        out_specs=row(D),
        out_shape=jax.ShapeDtypeStruct((M, D), F32),
        compiler_params=_cparams(("parallel",), 40),
        name="merge",
    )(x, o_a, o_d, o_h, z, z, z, w_branch, w_out)


def _cmp_weights(pe, w1, w2):
    n_part = CMP_LEN // CMP_STRIDE
    eye = jnp.eye(NSA_KV, dtype=F32)
    eye2 = jnp.eye(2, dtype=F32)
    w1r = w1.reshape(2, n_part, CMP_STRIDE, HEAD_DIM, CMP_HIDDEN)
    w1e = jnp.einsum("kmrdh,kK,gG->mrkgdKGh", w1r, eye2, eye).reshape(n_part, CHUNK_W, CMP_HID_W)
    w2e = jnp.einsum("khd,kK,gG->kghKGd", w2, eye2, eye).reshape(CMP_HID_W, ROW_W)
    per = pe.reshape(2, n_part, CMP_STRIDE, HEAD_DIM).transpose(1, 2, 0, 3)
    pee = jnp.broadcast_to(per[:, :, :, None, :], (n_part, CMP_STRIDE, 2, NSA_KV, HEAD_DIM)).reshape(n_part, CHUNK_W)
    return pee, w1e.astype(BF16), w2e.astype(BF16)


def _compress(x, xnext_row, pe_ref, w1_ref, w2_ref, kg_ref):
    n = x.shape[0]
    p0 = _dot((x + pe_ref[0:1, :]).astype(BF16), w1_ref[0])
    p1 = _dot((x + pe_ref[1:2, :]).astype(BF16), w1_ref[1])
    p1s = pltpu.roll(p1, n - 1, 0)
    if xnext_row is not None:
        p1n = _dot((xnext_row + pe_ref[1:2, :]).astype(BF16), w1_ref[1])
        p1s = jnp.where(_iota((n, 1), 0) == n - 1, p1n[0:1, :], p1s)
    comp = _dot(_silu(p0 + p1s).astype(BF16), w2_ref[...])
    cks = []
    for g in range(NSA_KV):
        kc = comp[:, g * HEAD_DIM:(g + 1) * HEAD_DIM]
        cks.append(_rms(kc, kg_ref[...]))
    return cks, comp[:, KV_W:]


def _cmp_prompt_kernel(x_ref, pe_ref, w1_ref, w2_ref, kg_ref, ck_ref, cvt_ref):
    cks, cv = _compress(x_ref[0], None, pe_ref, w1_ref, w2_ref, kg_ref)
    for g in range(NSA_KV):
        ck_ref[0, g] = cks[g].astype(BF16)
    cvt = cv.T
    for g in range(NSA_KV):
        cvt_ref[0, g] = cvt[g * HEAD_DIM:(g + 1) * HEAD_DIM, :].astype(BF16)


def _cmp_prompt(cmp_rows, pee, w1e, w2e, kg0):
    B, T = cmp_rows.shape[:2]
    nc = T // CMP_STRIDE
    x = cmp_rows.reshape(B, nc, CHUNK_W)
    full = lambda s: pl.BlockSpec(s, lambda b: (0,) * len(s))
    return pl.pallas_call(
        _cmp_prompt_kernel,
        grid=(B,),
        in_specs=[pl.BlockSpec((1, nc, CHUNK_W), lambda b: (b, 0, 0)), full(pee.shape), full(w1e.shape),
                  full(w2e.shape), full((1, HEAD_DIM))],
        out_specs=[pl.BlockSpec((1, NSA_KV, nc, HEAD_DIM), lambda b: (b, 0, 0, 0)),
                   pl.BlockSpec((1, NSA_KV, HEAD_DIM, nc), lambda b: (b, 0, 0, 0))],
        out_shape=[jax.ShapeDtypeStruct((B, NSA_KV, nc, HEAD_DIM), BF16),
                   jax.ShapeDtypeStruct((B, NSA_KV, HEAD_DIM, nc), BF16)],
        compiler_params=_cparams(("parallel",), 48),
        name="nsa_compress",
    )(x, pee, w1e, w2e, kg0.reshape(1, HEAD_DIM))


def _softmax_first(s, vt):
    m = jnp.max(s, axis=0, keepdims=True)
    e = jnp.exp(s - m)
    return m, jnp.sum(e, axis=0, keepdims=True), _dot(vt, e.astype(BF16))


def _softmax_step(s, vt, carry):
    m, l, acc = carry
    m_new = jnp.maximum(m, jnp.max(s, axis=0, keepdims=True))
    alpha = jnp.exp(m - m_new)
    e = jnp.exp(s - m_new)
    l = alpha * l + jnp.sum(e, axis=0, keepdims=True)
    acc = alpha * acc + _dot(vt, e.astype(BF16))
    return m_new, l, acc


KC_W = 2 * LANE - HEAD_DIM
KC_POS = 64
KC_POS_RADIX = 128
KW = HEAD_DIM + KC_W


def _key_features(T):
    k = np.arange(T)
    f = np.zeros((T, KW), np.float32)
    f[k, HEAD_DIM + k // SLC_BLOCK] = 1.0
    f[:, HEAD_DIM + KC_POS] = k // KC_POS_RADIX
    f[:, HEAD_DIM + KC_POS + 1] = k % KC_POS_RADIX
    return f


def _nsa_prompt_kernel(qt_ref, ck_ref, cvt_ref, sk_ref, svt_ref, wk_ref, wvt_ref, gt_ref, ovl_ref, o_ref,
                       *, n_cmp, n_blk):
    qb = pl.program_id(1)
    b0 = qb * Q_BLOCK
    QW = NSA_GROUP * Q_BLOCK
    TK = Q_BLOCK
    lane = _iota((1, QW), 1)
    tq = lane % Q_BLOCK
    q_pos = b0 + tq
    head = lane // Q_BLOCK
    qp1 = b0 + _iota((1, Q_BLOCK), 1)
    bj = _iota((n_blk, 1), 0)
    krel = _iota((TK, 1), 0) - tq
    frow = _iota((KC_W, 1), 0)
    k_diag = pl.multiple_of(b0, TK)
    gt = gt_ref[0, 0]

    def gate(g, branch):
        return jnp.concatenate([gt[(NSA_GROUP * g + j) * 3 + branch:(NSA_GROUP * g + j) * 3 + branch + 1, :]
                                for j in range(NSA_GROUP)], axis=1)

    qc_slcs, qc_wins, o_cmps = [], [], []
    for g in range(NSA_KV):
        slope = _alibi_slope(head, g)
        qgt = jnp.concatenate([qt_ref[0, NSA_GROUP * g + j] for j in range(NSA_GROUP)], axis=1)
        pos_rows = jnp.where(frow == KC_POS, slope * float(KC_POS_RADIX), jnp.where(frow == KC_POS + 1, slope, 0.0))

        dist = q_pos - (_iota((n_cmp, 1), 0) * CMP_STRIDE + (CMP_LEN - 1))
        mask = dist >= 0
        s = jnp.where(mask, _dot(ck_ref[0, g], qgt) - slope * dist.astype(F32), NEG_BIG)
        m = jnp.max(s, axis=0, keepdims=True)
        e = jnp.where(mask, jnp.exp(s - m), 0.0)
        p = (e / jnp.maximum(jnp.sum(e, axis=0, keepdims=True), 1e-30)).astype(BF16)
        o_cmps.append(_dot(cvt_ref[0, g], p))

        imp = _dot(ovl_ref[...], p[:, 0:Q_BLOCK])
        for j in range(1, NSA_GROUP):
            imp += _dot(ovl_ref[...], p[:, j * Q_BLOCK:(j + 1) * Q_BLOCK])
        cur = qp1 // SLC_BLOCK
        forced = (bj == 0) | (bj == cur) | (bj == cur - 1)
        imp = jnp.where(forced, SEL_BIG, jnp.where(bj * SLC_BLOCK <= qp1, imp, -SEL_BIG))
        tiles = [imp[8 * v:8 * v + 8, :] for v in range(n_blk // 8)]
        ranks = [jnp.zeros((8, Q_BLOCK), F32) for _ in tiles]
        for i in range(n_blk):
            row = imp[i:i + 1, :]
            for v, tile in enumerate(tiles):
                if 8 * v > i:
                    beats = row >= tile
                elif 8 * v + 8 <= i:
                    beats = row > tile
                else:
                    beats = (row > tile) | ((row == tile) & (bj[8 * v:8 * v + 8] > i))
                ranks[v] = ranks[v] + jnp.where(beats, 1.0, 0.0)
        rank = jnp.concatenate(ranks, axis=0)
        selb = jnp.where(rank < float(min(N_SEL, n_blk)), 0.0, NEG_BIG)
        selb = jnp.concatenate([selb] * NSA_GROUP, axis=1)
        if n_blk < KC_W:
            selb = jnp.concatenate([selb, jnp.zeros((KC_W - n_blk, QW), F32)], axis=0)
        qc_slcs.append(jnp.concatenate([qgt, (selb + pos_rows).astype(BF16)], axis=0))
        qc_wins.append(jnp.concatenate([qgt, pos_rows.astype(BF16)], axis=0))

    def scores(k_ref, g, k0, n, qcs):
        return _dot(k_ref[0, g, pl.ds(k0, n), :], qcs[g])

    WS = WINDOW + Q_BLOCK
    ws = pl.multiple_of(jnp.maximum(b0 - WINDOW, 0), TK)
    off = b0 - ws
    drel = tq - _iota((WS, 1), 0)
    in_win = (drel >= -off) & (drel < WINDOW - off)
    o_wins = []
    for g in range(NSA_KV):
        s = jnp.where(in_win, scores(wk_ref, g, ws, WS, qc_wins), NEG_BIG)
        _, l, acc = _softmax_first(s, wvt_ref[0, g, :, pl.ds(ws, WS)])
        o_wins.append(acc / jnp.maximum(l, 1e-30))

    carry = []
    for g in range(NSA_KV):
        s = jnp.where(krel <= 0, scores(sk_ref, g, k_diag, TK, qc_slcs), NEG_BIG)
        carry.append(_softmax_first(s, svt_ref[0, g, :, pl.ds(k_diag, TK)]))

    def slc_body(n):
        def body(i, carry):
            k0 = pl.multiple_of(i * n, n)
            G = range(NSA_KV)
            ss = [scores(sk_ref, g, k0, n, qc_slcs) for g in G]
            ms = [jnp.maximum(carry[g][0], jnp.max(ss[g], axis=0, keepdims=True)) for g in G]
            es = [jnp.exp(ss[g] - ms[g]) for g in G]
            pvs = [_dot(svt_ref[0, g, :, pl.ds(k0, n)], es[g].astype(BF16)) for g in G]
            out = []
            for g in G:
                m, l, acc = carry[g]
                alpha = jnp.exp(m - ms[g])
                out.append((ms[g], alpha * l + jnp.sum(es[g], axis=0, keepdims=True), alpha * acc + pvs[g]))
            return tuple(out)
        return body

    pairs = qb // 2
    carry = lax.fori_loop(0, pairs, slc_body(2 * TK), tuple(carry))
    carry = lax.fori_loop(2 * pairs, qb, slc_body(TK), carry)

    outs = []
    for g in range(NSA_KV):
        _, l, acc = carry[g]
        o_slc = acc / jnp.maximum(l, 1e-30)
        outs.append(gate(g, 0) * o_cmps[g] + gate(g, 1) * o_slc + gate(g, 2) * o_wins[g])

    for g in range(NSA_KV):
        for jp in range(NSA_GROUP // 2):
            pair = jnp.concatenate([outs[g][:, (2 * jp) * Q_BLOCK:(2 * jp + 1) * Q_BLOCK],
                                    outs[g][:, (2 * jp + 1) * Q_BLOCK:(2 * jp + 2) * Q_BLOCK]], axis=0)
            c0 = (NSA_GROUP * g + 2 * jp) * HEAD_DIM
            o_ref[0, :, c0:c0 + 2 * HEAD_DIM] = pair.T.astype(o_ref.dtype)


def _overlap(n_cmp, n_blk):
    ci = np.arange(n_cmp)[:, None] * CMP_STRIDE
    bj = np.arange(n_blk)[None, :]
    return ((ci < (bj + 1) * SLC_BLOCK) & (ci + CMP_LEN > bj * SLC_BLOCK)).astype(np.float32)


def _nsa_prompt(qt, ck, cvt, sk, svt, wk, wvt, gates):
    B, _, _, T = qt.shape
    nq = T // Q_BLOCK
    n_cmp = ck.shape[2]
    n_blk = T // SLC_BLOCK
    assert n_blk <= KC_POS and T <= KC_POS_RADIX * 256 and T >= WINDOW + Q_BLOCK
    ovl = jnp.asarray(_overlap(n_cmp, n_blk).T, BF16)
    per_b = lambda s: pl.BlockSpec((1,) + s, lambda b, i: (b,) + (0,) * len(s))
    return pl.pallas_call(
        functools.partial(_nsa_prompt_kernel, n_cmp=n_cmp, n_blk=n_blk),
        grid=(B, nq),
        in_specs=[
            pl.BlockSpec((1, NSA_HEADS, HEAD_DIM, Q_BLOCK), lambda b, i: (b, 0, 0, i)),
            per_b((NSA_KV, n_cmp, HEAD_DIM)), per_b((NSA_KV, HEAD_DIM, n_cmp)),
            per_b((NSA_KV, T, KW)), per_b((NSA_KV, HEAD_DIM, T)),
            per_b((NSA_KV, T, KW)), per_b((NSA_KV, HEAD_DIM, T)),
            pl.BlockSpec((1, 1, GATE_ROWS, Q_BLOCK), lambda b, i: (b, i, 0, 0)),
            pl.BlockSpec((n_blk, n_cmp), lambda b, i: (0, 0)),
        ],
        out_specs=pl.BlockSpec((1, Q_BLOCK, NSA_W), lambda b, i: (b, i, 0)),
        out_shape=jax.ShapeDtypeStruct((B, T, NSA_W), BF16),
        compiler_params=_cparams(("parallel", "arbitrary"), 48),
        name="nsa_prompt",
    )(qt, ck, cvt, sk, svt, wk, wvt, gates, ovl)


GATE_ROWS = 32


def _pair_rms(x, gain2, ones_bd):
    hi, lo = _split2(x * x)
    ms = (_dot(hi, ones_bd) + _dot(lo, ones_bd)) * (1.0 / HEAD_DIM)
    return x * lax.rsqrt(ms + NORM_EPS) * gain2


def _nsa_prep_kernel(zq_ref, zc0_ref, zc1_ref, zs0_ref, zs1_ref, zw0_ref, zw1_ref, zg_ref, qg_ref, kg_ref, kf_ref,
                     qt_ref, cmp_ref, slc_ref, win_ref, sk_ref, svt_ref, wk_ref, wvt_ref, gt_ref):
    lane = _iota((1, LANE), 1)
    ones_bd = jnp.where(_iota((LANE, LANE), 0) // HEAD_DIM == _iota((LANE, LANE), 1) // HEAD_DIM, 1.0, 0.0).astype(BF16)
    for hp in range(NSA_HEADS // 2):
        qn = _pair_rms(zq_ref[0, :, hp * LANE:(hp + 1) * LANE], qg_ref[...], ones_bd) * (HEAD_DIM ** -0.5)
        qnt = qn.T
        qt_ref[0, 2 * hp] = qnt[0:HEAD_DIM].astype(BF16)
        qt_ref[0, 2 * hp + 1] = qnt[HEAD_DIM:].astype(BF16)
    cmp_ref[0, :, 0:KV_W] = zc0_ref[0]
    cmp_ref[0, :, KV_W:] = zc1_ref[0]
    feat = kf_ref[...]
    for i, (zk_ref, zv_ref, rows_ref, ka_ref, vt_ref) in enumerate(
            ((zs0_ref, zs1_ref, slc_ref, sk_ref, svt_ref), (zw0_ref, zw1_ref, win_ref, wk_ref, wvt_ref))):
        k = _pair_rms(zk_ref[0], kg_ref[i:i + 1, :], ones_bd)
        v = zv_ref[0]
        rows_ref[0, :, 0:KV_W] = k
        rows_ref[0, :, KV_W:] = v
        for g in range(NSA_KV):
            kg = k if g == 0 else pltpu.roll(k, HEAD_DIM, 1)
            ka_ref[0, g, :, 0:LANE] = jnp.where(lane < HEAD_DIM, kg.astype(BF16), feat[:, 0:LANE])
            ka_ref[0, g, :, LANE:] = feat[:, LANE:]
        vt = v.T
        for g in range(NSA_KV):
            vt_ref[0, g] = vt[g * HEAD_DIM:(g + 1) * HEAD_DIM].astype(BF16)
    gt_ref[0, 0] = jax.nn.sigmoid(zg_ref[0]).T[0:GATE_ROWS]


def _nsa_prep(z, q_gain, k_gains, offs):
    B, T, _ = z.shape
    nq = T // Q_BLOCK
    assert offs["a_q"] % NSA_W == 0 and offs["a_kv"] % LANE == 0 and offs["a_g"] % LANE == 0
    kv0 = offs["a_kv"] // LANE
    zcol = lambda c: pl.BlockSpec((1, Q_BLOCK, LANE), lambda b, i: (b, i, c))
    full = lambda s: pl.BlockSpec(s, lambda b, i: (0,) * len(s))
    rows = pl.BlockSpec((1, Q_BLOCK, ROW_W), lambda b, i: (b, i, 0))
    keys = pl.BlockSpec((1, NSA_KV, Q_BLOCK, KW), lambda b, i: (b, 0, i, 0))
    vals = pl.BlockSpec((1, NSA_KV, HEAD_DIM, Q_BLOCK), lambda b, i: (b, 0, 0, i))
    two = lambda gain: jnp.concatenate([gain, gain], axis=-1)
    return pl.pallas_call(
        _nsa_prep_kernel,
        grid=(B, nq),
        in_specs=[pl.BlockSpec((1, Q_BLOCK, NSA_W), lambda b, i, c=offs["a_q"] // NSA_W: (b, i, c))]
        + [zcol(kv0 + j) for j in range(6)] + [zcol(offs["a_g"] // LANE)]
        + [full((1, LANE)), full((2, LANE)), pl.BlockSpec((Q_BLOCK, KW), lambda b, i: (i, 0))],
        out_specs=[pl.BlockSpec((1, NSA_HEADS, HEAD_DIM, Q_BLOCK), lambda b, i: (b, 0, 0, i)), rows, rows, rows,
                   keys, vals, keys, vals, pl.BlockSpec((1, 1, GATE_ROWS, Q_BLOCK), lambda b, i: (b, i, 0, 0))],
        out_shape=[jax.ShapeDtypeStruct((B, NSA_HEADS, HEAD_DIM, T), BF16)]
        + [jax.ShapeDtypeStruct((B, T, ROW_W), F32)] * 3
        + [jax.ShapeDtypeStruct((B, NSA_KV, T, KW), BF16), jax.ShapeDtypeStruct((B, NSA_KV, HEAD_DIM, T), BF16)] * 2
        + [jax.ShapeDtypeStruct((B, nq, GATE_ROWS, Q_BLOCK), F32)],
        compiler_params=_cparams(("parallel", "parallel"), 32),
        name="nsa_prep",
    )(z, z, z, z, z, z, z, z, two(q_gain).reshape(1, LANE), two(k_gains[1:3]), jnp.asarray(_key_features(T), BF16))


def _nsa_decode_kernel(pt_ref, q_ref, new_ref, gt_ref, pe_ref, w1_ref, w2_ref, kg_ref, ovl_ref, exp_ref, *rest,
                       n_pages):
    del pt_ref
    cmp_pages = rest[:n_pages]
    slc_pages = rest[n_pages:2 * n_pages]
    win_ref, o_ref = rest[2 * n_pages], rest[2 * n_pages + 1]
    n_past = n_pages * PAGE_SIZE
    n_cmp = n_past // CMP_STRIDE
    q_pos = n_past
    new = new_ref[0]
    x = jnp.concatenate([r[0] for r in cmp_pages], axis=0)
    xnew = jnp.concatenate([new[:, 0:ROW_W], jnp.zeros((8, CHUNK_W - ROW_W), F32)], axis=1)
    cks, cv = _compress(x, xnew, pe_ref, w1_ref, w2_ref, kg_ref)

    jrow = _iota((8, 1), 0)
    lanes = _iota((1, LANE), 1)
    n_blk = (n_past + 1 + SLC_BLOCK - 1) // SLC_BLOCK
    cur = q_pos // SLC_BLOCK
    ii = _iota((LANE, LANE), 0)
    jj = _iota((LANE, LANE), 1)
    for g in range(NSA_KV):
        slope = _alibi_slope(jrow, g)
        qg = q_ref[0, g]
        lo, hi = g * HEAD_DIM, (g + 1) * HEAD_DIM

        dist = q_pos - (lanes * CMP_STRIDE + (CMP_LEN - 1))
        mask = dist >= 0
        s = jnp.where(mask, _dot_nt(qg, cks[g].astype(BF16)) - slope * dist.astype(F32), NEG_BIG)
        m = jnp.max(s, axis=1, keepdims=True)
        e = jnp.where(mask, jnp.exp(s - m), 0.0)
        p = (e / jnp.maximum(jnp.sum(e, axis=1, keepdims=True), 1e-30)).astype(BF16)
        o_cmp = _dot(p, cv[:, lo:hi].astype(BF16))

        imp = jnp.sum(jnp.where(jrow < NSA_GROUP, _dot(p, ovl_ref[...]), 0.0), axis=0, keepdims=True)
        forced = (lanes == 0) | (lanes == cur) | (lanes == cur - 1)
        imp = jnp.where(forced, SEL_BIG, jnp.where(lanes * SLC_BLOCK <= q_pos, imp, -SEL_BIG))
        imp = jnp.where(lanes < n_blk, imp, -3e38)
        impr = jnp.broadcast_to(imp, (LANE, LANE))
        impc = jnp.sum(jnp.where(ii == jj, impr, 0.0), axis=1, keepdims=True)
        beats = jnp.where(impc > impr, 1.0, jnp.where((impc == impr) & (ii < jj), 1.0, 0.0))
        rank = jnp.sum(beats, axis=0, keepdims=True)
        sel = jnp.where(rank < float(min(N_SEL, n_blk)), 1.0, 0.0)
        selk = _dot(jnp.broadcast_to(sel, (8, LANE)).astype(BF16), exp_ref[...])

        def attend(kmat, vmat, mask, dist, knew, vnew, mask_new):
            s = jnp.where(mask, _dot_nt(qg, kmat) - slope * dist.astype(F32), NEG_BIG)
            s_new = jnp.sum(qg.astype(F32) * knew.astype(BF16).astype(F32), axis=1, keepdims=True)
            if mask_new is not None:
                s_new = jnp.where(mask_new, s_new, NEG_BIG)
            m = jnp.maximum(jnp.max(s, axis=1, keepdims=True), s_new)
            e = jnp.where(mask, jnp.exp(s - m), 0.0)
            e_new = jnp.exp(s_new - m)
            if mask_new is not None:
                e_new = jnp.where(mask_new, e_new, 0.0)
            l = jnp.sum(e, axis=1, keepdims=True) + e_new
            o = _dot(e.astype(BF16), vmat) + e_new.astype(BF16).astype(F32) * vnew.astype(BF16).astype(F32)
            return o / jnp.maximum(l, 1e-30)

        kpos = _iota((1, n_past), 1)
        sk = jnp.concatenate([r[0][:, lo:hi] for r in slc_pages], axis=0).astype(BF16)
        sv = jnp.concatenate([r[0][:, KV_W + lo:KV_W + hi] for r in slc_pages], axis=0).astype(BF16)
        o_slc = attend(sk, sv, selk > 0.5, q_pos - kpos, new[1:2, lo:hi], new[1:2, KV_W + lo:KV_W + hi],
                       sel[:, cur:cur + 1] > 0.5)

        n_win = win_ref.shape[1]
        wpos = (q_pos - n_win) + _iota((1, n_win), 1)
        dist_w = q_pos - wpos
        wrows = win_ref[0]
        o_win = attend(wrows[:, lo:hi].astype(BF16), wrows[:, KV_W + lo:KV_W + hi].astype(BF16),
                       (dist_w < WINDOW) & (wpos >= 0), dist_w, new[2:3, lo:hi], new[2:3, KV_W + lo:KV_W + hi],
                       None)

        gt = gt_ref[0, g]
        o_ref[0, g] = gt[:, 0:1] * o_cmp + gt[:, 1:2] * o_slc + gt[:, 2:3] * o_win


def _nsa_decode(page_table, q, new_rows, gates, cmp_pool, slc_pool, win, pee, w1e, w2e, kg0):
    B, n_pages = page_table.shape
    n_past = n_pages * PAGE_SIZE
    n_cmp = n_past // CMP_STRIDE
    assert n_cmp == LANE and win.shape[1] <= n_past
    n_blk = (n_past + 1 + SLC_BLOCK - 1) // SLC_BLOCK
    ovl = np.zeros((n_cmp, LANE), np.float32)
    ovl[:, :n_blk] = _overlap(n_cmp, n_blk)
    expand = (np.arange(n_past)[None, :] // SLC_BLOCK == np.arange(LANE)[:, None]).astype(np.float32)
    chunks_per_page = PAGE_SIZE // CMP_STRIDE
    cmp_chunks = cmp_pool.reshape(cmp_pool.shape[0], chunks_per_page, CHUNK_W)
    full = lambda s: pl.BlockSpec(s, lambda b, pt: (0,) * len(s))
    per_b = lambda s: pl.BlockSpec((1,) + s, lambda b, pt: (b,) + (0,) * len(s))

    def page_spec(shape, p):
        return pl.BlockSpec((1,) + shape, lambda b, pt: (pt[b, p], 0, 0))

    in_specs = [per_b((NSA_KV, 8, HEAD_DIM)), per_b((8, ROW_W)), per_b((NSA_KV, 8, LANE)), full(pee.shape),
                full(w1e.shape), full(w2e.shape), full((1, HEAD_DIM)), full(ovl.shape), full(expand.shape)]
    in_specs += [page_spec((chunks_per_page, CHUNK_W), p) for p in range(n_pages)]
    in_specs += [page_spec((PAGE_SIZE, ROW_W), p) for p in range(n_pages)]
    in_specs += [per_b(win.shape[1:])]
    return pl.pallas_call(
        functools.partial(_nsa_decode_kernel, n_pages=n_pages),
        grid_spec=pltpu.PrefetchScalarGridSpec(
            num_scalar_prefetch=1, grid=(B,), in_specs=in_specs,
            out_specs=pl.BlockSpec((1, NSA_KV, 8, HEAD_DIM), lambda b, pt: (b, 0, 0, 0))),
        out_shape=jax.ShapeDtypeStruct((B, NSA_KV, 8, HEAD_DIM), F32),
        compiler_params=_cparams(("arbitrary",), 48),
        name="nsa_decode",
    )(page_table, q, new_rows, gates, pee, w1e, w2e, kg0.reshape(1, HEAD_DIM), jnp.asarray(ovl, BF16),
      jnp.asarray(expand, BF16), *([cmp_chunks] * n_pages), *([slc_pool] * n_pages), win)


DN_TB = 2 * DN_CHUNK
HALO = 8


def _softplus(x):
    return jnp.maximum(x, 0.0) + jnp.log(1.0 + jnp.exp(-jnp.abs(x)))


def _dn_prompt_kernel(zq_ref, zk_ref, zv_ref, zz_ref, zs_ref, cw_ref, a_ref, dtb_ref, gain_ref, o_ref, s_out_ref,
                      s_ref, buf_ref, *, n_step, col_b, col_a):
    t = pl.program_id(1)
    TB, C = DN_TB, DN_CHUNK

    @pl.when(t == 0)
    def _():
        s_ref[...] = jnp.zeros_like(s_ref)
        buf_ref[0:HALO, :] = jnp.zeros((HALO, DN_QKV), F32)

    @pl.when(t > 0)
    def _():
        buf_ref[0:HALO, :] = buf_ref[TB:TB + HALO, :]

    buf_ref[HALO:HALO + TB, 0:DN_QK] = zq_ref[0]
    buf_ref[HALO:HALO + TB, DN_QK:2 * DN_QK] = zk_ref[0]
    buf_ref[HALO:HALO + TB, 2 * DN_QK:] = zv_ref[0]
    cw = cw_ref[...]
    first = HALO - (CONV_W - 1)
    y = buf_ref[pl.ds(first, TB), :] * cw[0:1, :]
    for j in range(1, CONV_W):
        y += buf_ref[pl.ds(first + j, TB), :] * cw[j:j + 1, :]
    act = _silu(y)

    zs = zs_ref[0]
    beta_all = jax.nn.sigmoid(zs)
    g_all = -a_ref[...] * _softplus(zs + dtb_ref[...])
    zz = zz_ref[0]
    gain = gain_ref[...]

    ii = _iota((C, C), 0)
    jj = _iota((C, C), 1)
    incl = ii >= jj
    eye = (ii == jj).astype(F32)
    n_sub = TB // C
    ch = []
    for sc in range(n_sub):
        rows = slice(sc * C, (sc + 1) * C)
        for h in range(DN_HEADS):
            cs = slice(h * DN_DK, (h + 1) * DN_DK)
            q = _l2n(act[rows, cs]) * (DN_DK ** -0.5)
            k = _l2n(act[rows, DN_QK + h * DN_DK:DN_QK + (h + 1) * DN_DK])
            v = act[rows, 2 * DN_QK + h * DN_DV:2 * DN_QK + (h + 1) * DN_DV]
            gcol = g_all[rows, col_a + h:col_a + h + 1]
            bcol = beta_all[rows, col_b + h:col_b + h + 1]
            grow = jnp.sum(jnp.where(ii <= jj, gcol, 0.0), axis=0, keepdims=True)
            gcum = jnp.sum(jnp.where(ii == jj, grow, 0.0), axis=1, keepdims=True)
            decay = jnp.where(incl, jnp.exp(jnp.where(incl, gcum - grow, 0.0)), 0.0)
            kb = k.astype(BF16)
            ch.append(dict(q=q, k=k, v=v, kb=kb, bcol=bcol, gcum=gcum, decay=decay, sc=sc, h=h, rows=rows, cs=cs))
    for c in ch:
        c["npow"] = -jnp.where(ii > jj, c["bcol"] * _dot_nt(c["kb"], c["kb"]) * c["decay"], 0.0)
        c["tinv"] = eye + c["npow"]
    for _ in range(int(np.log2(C)) - 1):
        for c in ch:
            nb = c["npow"].astype(BF16)
            c["npow"] = _dot(nb, nb)
        for c in ch:
            c["tinv"] = c["tinv"] + _dot(c["tinv"].astype(BF16), c["npow"].astype(BF16))
    for c in ch:
        tb = c["tinv"].astype(BF16)
        c["eg"] = jnp.exp(c["gcum"])
        c["u"] = _dot(tb, (c["v"] * c["bcol"]).astype(BF16))
        c["w"] = _dot(tb, (c["k"] * (c["bcol"] * c["eg"])).astype(BF16)).astype(BF16)
        c["attn"] = (_dot_nt(c["q"].astype(BF16), c["kb"]) * c["decay"]).astype(BF16)
    for sc in range(n_sub):
        cur = [c for c in ch if c["sc"] == sc]
        Ss = [s_ref[c["h"]] for c in cur]
        Sbs = [S.astype(BF16) for S in Ss]
        vns = [c["u"] - _dot(c["w"], Sb) for c, Sb in zip(cur, Sbs)]
        for c, S, Sb, v_new in zip(cur, Ss, Sbs, vns):
            vnb = v_new.astype(BF16)
            o = _dot((c["q"] * c["eg"]).astype(BF16), Sb) + _dot(c["attn"], vnb)
            g_last = c["gcum"][C - 1:C, :]
            s_ref[c["h"]] = S * jnp.exp(g_last) + _dot_tn((c["k"] * jnp.exp(g_last - c["gcum"])).astype(BF16), vnb)
            o_ref[0, c["rows"], c["cs"]] = (_rms(o, gain) * _silu(zz[c["rows"], c["cs"]])).astype(o_ref.dtype)

    @pl.when(t == n_step - 1)
    def _():
        s_out_ref[0] = s_ref[...]


def _dn_prompt(z, conv_w, a_log, dt_bias, out_gain, offs):
    B, T, _ = z.shape
    n_step = T // DN_TB
    wide = lambda name, k=0: pl.BlockSpec((1, DN_TB, DN_QK), lambda b, t, c=offs[name] // DN_QK + k: (b, t, c))
    assert offs["d_qkv"] % DN_QK == 0 and offs["d_z"] % DN_V == 0 and offs["a_g"] % LANE == 0
    col_b, col_a = offs["d_b"] - offs["a_g"], offs["d_a"] - offs["a_g"]
    lanes = lambda vals, col: jnp.zeros((1, LANE), F32).at[0, col:col + DN_HEADS].set(vals)
    full = lambda s: pl.BlockSpec(s, lambda b, t: (0,) * len(s))
    return pl.pallas_call(
        functools.partial(_dn_prompt_kernel, n_step=n_step, col_b=col_b, col_a=col_a),
        grid=(B, n_step),
        in_specs=[wide("d_qkv", 0), wide("d_qkv", 1), wide("d_qkv", 2), wide("d_z"),
                  pl.BlockSpec((1, DN_TB, LANE), lambda b, t, c=offs["a_g"] // LANE: (b, t, c)),
                  full((CONV_W, DN_QKV)), full((1, LANE)), full((1, LANE)), full((1, DN_DV))],
        out_specs=[pl.BlockSpec((1, DN_TB, DN_V), lambda b, t: (b, t, 0)),
                   pl.BlockSpec((1, DN_HEADS, DN_DK, DN_DV), lambda b, t: (b, 0, 0, 0))],
        out_shape=[jax.ShapeDtypeStruct((B, T, DN_V), BF16), jax.ShapeDtypeStruct((B, DN_HEADS, DN_DK, DN_DV), F32)],
        scratch_shapes=[pltpu.VMEM((DN_HEADS, DN_DK, DN_DV), F32), pltpu.VMEM((HALO + DN_TB, DN_QKV), F32)],
        compiler_params=_cparams(("parallel", "arbitrary"), 32),
        name="dn_prompt",
    )(z, z, z, z, z, conv_w, lanes(jnp.exp(a_log), col_a), lanes(dt_bias, col_a), out_gain.reshape(1, DN_DV))


def _row0(x):
    return jnp.where(_iota(x.shape, 0) == 0, x, 0.0)


def _dn_decode_kernel(vec_ref, s_ref, o_ref, s_out_ref, *, bb):
    def body(b, _):
        for h in range(DN_HEADS):
            x = vec_ref[b, h]
            S = s_ref[b, h]
            xs = _dot(x.astype(BF16), S.astype(BF16))
            k, q, v, eg, beta = x[0:1], x[1:2], x[2:3], x[3:4], x[4:5]
            v_new = beta * (v - eg * xs[0:1])
            kb = k.astype(BF16).astype(F32)
            qk = jnp.sum(q.astype(BF16).astype(F32) * kb, axis=1, keepdims=True)
            o = eg * xs[1:2] + qk.astype(BF16).astype(F32) * v_new.astype(BF16).astype(F32)
            o_ref[b, h] = jnp.broadcast_to(o, (8, DN_DV))
            s_out_ref[b, h] = S * eg[:, 0:1] + _dot_tn(_row0(x).astype(BF16),
                                                      _row0(jnp.broadcast_to(v_new, (8, DN_DV))).astype(BF16))
        return 0

    lax.fori_loop(0, bb, body, 0)


def _dn_decode(vec, S0, bb=8):
    B = vec.shape[0]
    spec_v = pl.BlockSpec((bb, DN_HEADS, 8, DN_DK), lambda i: (i, 0, 0, 0))
    spec_s = pl.BlockSpec((bb, DN_HEADS, DN_DK, DN_DV), lambda i: (i, 0, 0, 0))
    return pl.pallas_call(
        functools.partial(_dn_decode_kernel, bb=bb),
        grid=(B // bb,),
        in_specs=[spec_v, spec_s],
        out_specs=[spec_v, spec_s],
        out_shape=[jax.ShapeDtypeStruct(vec.shape, F32), jax.ShapeDtypeStruct(S0.shape, F32)],
        compiler_params=_cparams(("parallel",), 32),
        name="dn_decode",
    )(vec, S0)


def _hg_prompt_kernel(zf_ref, zi_ref, zq_ref, zo_ref, lb_ref, gain_ref, tril_ref, o_ref, s_out_ref, st_ref, g_ref,
                      q_ref, k_ref, *, n_step, tb):
    t = pl.program_id(1)

    @pl.when(t == 0)
    def _():
        st_ref[...] = jnp.zeros_like(st_ref)

    zf = zf_ref[0]
    lb = lb_ref[...]
    lf = jnp.log(lb + (1.0 - lb) * jax.nn.sigmoid(zf))
    k_ref[...] = (1.0 - lb) * jax.nn.sigmoid(-zf)
    q_ref[...] = _silu(zq_ref[0]) * (HG_DK ** -0.5)
    v_ref = zi_ref.at[0]
    gain = gain_ref[...]
    hi = lf.astype(BF16)
    r1 = lf - hi.astype(F32)
    mid = r1.astype(BF16)
    lo = (r1 - mid.astype(F32)).astype(BF16)
    tril = tril_ref[...]
    g_ref[...] = _dot(tril, hi) + (_dot(tril, mid) + _dot(tril, lo))
    C = HG_CHUNK
    sidx = _iota((C, 1), 0)

    def sub(sc, _):
        r0 = pl.multiple_of(sc * C, C)
        for h in range(HG_HEADS):
            cs = slice(h * HG_DK, (h + 1) * HG_DK)
            q = q_ref[pl.ds(r0, C), cs]
            k = k_ref[pl.ds(r0, C), cs]
            v = v_ref[pl.ds(r0, C), cs]
            G = g_ref[pl.ds(r0, C), cs]
            st = st_ref[h]
            cols = []
            for tt in range(C):
                msk = sidx <= tt
                dec = jnp.where(msk, jnp.exp(jnp.where(msk, G[tt:tt + 1, :] - G, 0.0)), 0.0)
                cols.append(jnp.sum(dec * k * q[tt:tt + 1, :], axis=1, keepdims=True))
            a_t = jnp.concatenate(cols, axis=1)
            vb = v.astype(BF16)
            o = _dot_nt((q * jnp.exp(G)).astype(BF16), st.astype(BF16)) + _dot_tn(a_t.astype(BF16), vb)
            og = jax.nn.sigmoid(zo_ref[0, pl.ds(r0, C), cs])
            o_ref[0, pl.ds(r0, C), cs] = (_rms(o, gain) * og).astype(o_ref.dtype)
            g_last = G[C - 1:C, :]
            st_ref[h] = st * jnp.exp(g_last) + _dot_tn(vb, (k * jnp.exp(g_last - G)).astype(BF16))
        return 0

    lax.fori_loop(0, tb // C, sub, 0)

    @pl.when(t == n_step - 1)
    def _():
        for h in range(HG_HEADS):
            s_out_ref[0, h] = st_ref[h].T


def _hg_prompt(z, lb, out_gain, offs, tb=128):
    B, T, _ = z.shape
    n_step = T // tb
    r = np.arange(tb)
    tril = ((r[:, None] >= r[None, :]) & (r[:, None] // HG_CHUNK == r[None, :] // HG_CHUNK)).astype(np.float32)
    assert all(offs[n] % HG_WK == 0 for n in ("r_f", "r_i", "r_q", "r_og"))
    col = lambda name: pl.BlockSpec((1, tb, HG_WK), lambda b, t, c=offs[name] // HG_WK: (b, t, c))
    full = lambda s: pl.BlockSpec(s, lambda b, t: (0,) * len(s))
    return pl.pallas_call(
        functools.partial(_hg_prompt_kernel, n_step=n_step, tb=tb),
        grid=(B, n_step),
        in_specs=[col("r_f"), col("r_i"), col("r_q"), col("r_og"), full((1, HG_WK)), full((1, HG_DV)),
                  full((tb, tb))],
        out_specs=[pl.BlockSpec((1, tb, HG_WV), lambda b, t: (b, t, 0)),
                   pl.BlockSpec((1, HG_HEADS, HG_DK, HG_DV), lambda b, t: (b, 0, 0, 0))],
        out_shape=[jax.ShapeDtypeStruct((B, T, HG_WV), BF16), jax.ShapeDtypeStruct((B, HG_HEADS, HG_DK, HG_DV), F32)],
        scratch_shapes=[pltpu.VMEM((HG_HEADS, HG_DV, HG_DK), F32), pltpu.VMEM((tb, HG_WK), F32),
                        pltpu.VMEM((tb, HG_WK), F32), pltpu.VMEM((tb, HG_WK), F32)],
        compiler_params=_cparams(("parallel", "arbitrary"), 32),
        name="hg_prompt",
    )(z, z, z, z, lb.reshape(1, HG_WK), out_gain.reshape(1, HG_DV), jnp.asarray(tril, BF16))


def _hg_decode_kernel(vec_ref, s_ref, o_ref, s_out_ref, *, bb):
    ii = _iota((HG_DK, HG_DK), 0)
    jj = _iota((HG_DK, HG_DK), 1)

    def body(b, _):
        for h in range(HG_HEADS):
            x = vec_ref[b, h]
            S = s_ref[b, h]
            k, q, v, lf = x[0:1], x[1:2], x[2:3], x[3:4]
            f = jnp.exp(lf)
            qs = _dot(jnp.broadcast_to(q * f, (8, HG_DK)).astype(BF16), S.astype(BF16))
            a = jnp.sum(q * k, axis=1, keepdims=True)
            o = qs[0:1] + a.astype(BF16).astype(F32) * v.astype(BF16).astype(F32)
            o_ref[b, h] = jnp.broadcast_to(o, (8, HG_DV))
            fcol = jnp.sum(jnp.where(ii == jj, f, 0.0), axis=1, keepdims=True)
            s_out_ref[b, h] = S * fcol + _dot_tn(_row0(x).astype(BF16),
                                                 _row0(jnp.broadcast_to(v, (8, HG_DV))).astype(BF16))
        return 0

    lax.fori_loop(0, bb, body, 0)


def _hg_decode(vec, S0, bb=8):
    B = vec.shape[0]
    spec_v = pl.BlockSpec((bb, HG_HEADS, 8, HG_DK), lambda i: (i, 0, 0, 0))
    spec_s = pl.BlockSpec((bb, HG_HEADS, HG_DK, HG_DV), lambda i: (i, 0, 0, 0))
    return pl.pallas_call(
        functools.partial(_hg_decode_kernel, bb=bb),
        grid=(B // bb,),
        in_specs=[spec_v, spec_s],
        out_specs=[spec_v, spec_s],
        out_shape=[jax.ShapeDtypeStruct(vec.shape, F32), jax.ShapeDtypeStruct(S0.shape, F32)],
        compiler_params=_cparams(("parallel",), 32),
        name="hg_decode",
    )(vec, S0)


def _head_rms(x, g):
    return x * lax.rsqrt(jnp.mean(x * x, axis=-1, keepdims=True) + NORM_EPS) * g


def _l2n(x):
    return x * lax.rsqrt(jnp.sum(x * x, axis=-1, keepdims=True) + L2_EPS)


def _rows8(rows):
    x = jnp.stack(rows, axis=-2)
    pad = [(0, 0)] * x.ndim
    pad[-2] = (0, 8 - len(rows))
    return jnp.pad(x, pad)


def _layer(x, l, prm, wts, past, page_table):
    B, T, D = x.shape
    M = B * T
    tm = min(M, ROW_TILE)
    tm_w = min(M, ROW_TILE_STREAMED)
    x2 = x.reshape(M, D)
    x2 = _ffn(x2, prm["ffn1_norm"][l], wts["ffn1_w_gu"][l], wts["ffn1_w_down"][l], tm_w)
    z = _inproj(x2, prm["mix_norm"][l], wts["w_in"][l], tm_w)
    sizes, _, offs, _ = _z_layout()
    pee, w1e, w2e = wts["cmp"][l]
    kg = prm["nsa_k_norm"][l]
    p = jax.nn.softmax(prm["hg_lb_logits"], axis=0)
    lb = (jnp.cumsum(p, axis=0) - p[0])[l]
    if past is None:
        z3 = z.reshape(B, T, -1)
        qt, cmp_new, slc_new, win_new, sk, svt, wk, wvt, gt = _nsa_prep(z3, prm["nsa_q_norm"][l], kg, offs)
        ck, cvt = _cmp_prompt(cmp_new, pee, w1e, w2e, kg[0])
        o_a = _nsa_prompt(qt, ck, cvt, sk, svt, wk, wvt, gt)
        o_d, dn_state = _dn_prompt(z3, prm["dn_conv_w"][l], prm["dn_A_log"][l], prm["dn_dt_bias"][l],
                                   prm["dn_out_norm"][l], offs)
        o_h, hg_state = _hg_prompt(z3, lb, prm["hg_out_norm"][l], offs)
        rows = lambda a: a.reshape(B, -1, 2, NSA_KV, HEAD_DIM)
        cmp_new, slc_new, win_state = rows(cmp_new), rows(slc_new), rows(win_new[:, T - min(WINDOW, T):])
        conv_state = z3[:, T - (CONV_W - 1):, offs["d_qkv"]:offs["d_qkv"] + DN_QKV]
    else:
        zs = {n: z[:, offs[n]:offs[n] + sizes[n]].reshape(B, T, sizes[n]) for n in Z_ORDER if n != "m_g"}
        (o_a, o_d, o_h), (cmp_new, slc_new, win_state, conv_state, dn_state, hg_state) = _decode_mixers(
            zs, l, prm, (pee, w1e, w2e), lb, past, page_table)

    x2 = _merge(x2, o_a.reshape(M, NSA_W), o_d.reshape(M, DN_V), o_h.reshape(M, HG_WV), z, wts["w_branch"][l],
                wts["w_out"][l], tm)
    x2 = _ffn(x2, prm["ffn2_norm"][l], wts["ffn2_w_gu"][l], wts["ffn2_w_down"][l], tm_w)
    return x2.reshape(B, T, D), (cmp_new, slc_new, win_state, conv_state, dn_state, hg_state)


def _decode_mixers(zs, l, prm, cmp_w, lb, past, page_table):
    B, T = zs["a_q"].shape[:2]
    pee, w1e, w2e = cmp_w
    kg = prm["nsa_k_norm"][l]
    q = _head_rms(zs["a_q"].reshape(B, T, NSA_HEADS, HEAD_DIM), prm["nsa_q_norm"][l]) * (HEAD_DIM ** -0.5)
    kv = zs["a_kv"].reshape(B, T, 3, 2, NSA_KV, HEAD_DIM)
    cmp_new = kv[:, :, 0]
    slc_new = jnp.stack([_head_rms(kv[:, :, 1, 0], kg[1]), kv[:, :, 1, 1]], axis=2)
    win_new = jnp.stack([_head_rms(kv[:, :, 2, 0], kg[2]), kv[:, :, 2, 1]], axis=2)
    gates = jax.nn.sigmoid(zs["a_g"].reshape(B, T, NSA_HEADS, 3))
    qd = q.reshape(B, NSA_KV, NSA_GROUP, HEAD_DIM)
    qd = jnp.pad(qd, ((0, 0), (0, 0), (0, 8 - NSA_GROUP), (0, 0))).astype(BF16)
    new_rows = _rows8([cmp_new.reshape(B, ROW_W), slc_new.reshape(B, ROW_W), win_new.reshape(B, ROW_W)])
    gd = gates.reshape(B, NSA_KV, NSA_GROUP, 3)
    gd = jnp.pad(gd, ((0, 0), (0, 0), (0, 8 - NSA_GROUP), (0, LANE - 3)))
    n_pool = past["cmp"].shape[0]
    o8 = _nsa_decode(page_table, qd, new_rows, gd, past["cmp"].reshape(n_pool, PAGE_SIZE, ROW_W),
                     past["slc"].reshape(n_pool, PAGE_SIZE, ROW_W),
                     past["win"].reshape(B, -1, ROW_W), pee, w1e, w2e, kg[0])
    o_a = o8[:, :, :NSA_GROUP].reshape(B, T, NSA_W)

    d_qkv = zs["d_qkv"]
    xx = jnp.concatenate([past["conv"], d_qkv], axis=1)
    cw = prm["dn_conv_w"][l]
    qkv = sum(xx[:, j:j + T] * cw[j] for j in range(CONV_W))
    conv_state = xx[:, -(CONV_W - 1):]
    dq, dk, dv = jnp.split(jax.nn.silu(qkv), [DN_QK, 2 * DN_QK], axis=-1)
    dq = _l2n(dq.reshape(B, T, DN_HEADS, DN_DK)) * (DN_DK ** -0.5)
    dk = _l2n(dk.reshape(B, T, DN_HEADS, DN_DK))
    beta = jax.nn.sigmoid(zs["d_b"])
    g = -jnp.exp(prm["dn_A_log"][l]) * jax.nn.softplus(zs["d_a"] + prm["dn_dt_bias"][l])
    lanes = lambda a: jnp.broadcast_to(a[:, 0, :, None], (B, DN_HEADS, DN_DK))
    vec = _rows8([dk[:, 0], dq[:, 0], dv.reshape(B, DN_HEADS, DN_DV), lanes(jnp.exp(g)), lanes(beta)])
    o8, dn_state = _dn_decode(vec, past["dn_S"])
    o_d = o8[:, :, 0].reshape(B, T, DN_V)
    o_d = _head_rms(o_d.reshape(B, T, DN_HEADS, DN_DV), prm["dn_out_norm"][l]) * jax.nn.silu(
        zs["d_z"].reshape(B, T, DN_HEADS, DN_DV))

    zf = zs["r_f"]
    logf = jnp.log(lb + (1.0 - lb) * jax.nn.sigmoid(zf))
    k_in = (1.0 - lb) * jax.nn.sigmoid(-zf)
    hq = jax.nn.silu(zs["r_q"]) * (HG_DK ** -0.5)
    hd = lambda a: a.reshape(B, HG_HEADS, HG_DK)
    vec = _rows8([hd(k_in), hd(hq), hd(zs["r_i"]), hd(logf)])
    o8, hg_state = _hg_decode(vec, past["hg_S"])
    o_h = o8[:, :, 0].reshape(B, T, HG_WV)
    o_h = _head_rms(o_h.reshape(B, T, HG_HEADS, HG_DV), prm["hg_out_norm"][l]) * jax.nn.sigmoid(
        zs["r_og"].reshape(B, T, HG_HEADS, HG_DV))
    return (o_a, o_d, o_h), (cmp_new, slc_new, win_new, conv_state, dn_state, hg_state)


def _trunk(x, prm, wts, caches, page_table):
    new = []
    for l in range(DEPTH):
        past = None
        if caches is not None:
            past = {k: v[l] for k, v in caches.items()}
        x, st = _layer(x, l, prm, wts, past, page_table)
        new.append(st)
    return x, [jnp.stack([s[i] for s in new], axis=0) for i in range(6)]


def kernel(x_prompt, x_sample, cache_cmp_kv, cache_slc_kv, cache_win_kv, state_dn_conv, state_dn_S, state_hg_S,
           page_table, ffn1_norm, ffn1_w_gu, ffn1_w_down, mix_norm, w_in, nsa_q_norm, nsa_k_norm, nsa_cmp_pe,
           nsa_cmp_w1, nsa_cmp_w2, dn_conv_w, dn_A_log, dn_dt_bias, dn_out_norm, hg_lb_logits, hg_out_norm,
           w_branch, w_out, ffn2_norm, ffn2_w_gu, ffn2_w_down):
    prm = dict(ffn1_norm=ffn1_norm, mix_norm=mix_norm, nsa_q_norm=nsa_q_norm, nsa_k_norm=nsa_k_norm,
               dn_conv_w=dn_conv_w, dn_A_log=dn_A_log, dn_dt_bias=dn_dt_bias, dn_out_norm=dn_out_norm,
               hg_lb_logits=hg_lb_logits, hg_out_norm=hg_out_norm, ffn2_norm=ffn2_norm)
    bf = lambda w: w.astype(BF16)
    wts = dict(ffn1_w_gu=bf(ffn1_w_gu), ffn1_w_down=bf(ffn1_w_down), ffn2_w_gu=bf(ffn2_w_gu),
               ffn2_w_down=bf(ffn2_w_down), w_branch=bf(w_branch), w_out=bf(w_out),
               w_in=jnp.stack([_permute_w_in(w_in[l]) for l in range(DEPTH)]),
               cmp=[_cmp_weights(nsa_cmp_pe[l], nsa_cmp_w1[l], nsa_cmp_w2[l]) for l in range(DEPTH)])
    y_p, (p_cmp, p_slc, p_win, p_conv, p_dn, p_hg) = _trunk(x_prompt, prm, wts, None, None)
    caches = dict(cmp=cache_cmp_kv, slc=cache_slc_kv, win=cache_win_kv, conv=state_dn_conv, dn_S=state_dn_S,
                  hg_S=state_hg_S)
    y_s, (s_cmp, s_slc, s_win, s_conv, s_dn, s_hg) = _trunk(x_sample, prm, wts, caches, page_table)
    return (y_p, y_s, p_cmp, s_cmp, p_slc, s_slc, p_win, s_win, p_conv, s_conv, p_dn, s_dn, p_hg, s_hg)
```

```python
import functools

import jax
import jax.numpy as jnp
import numpy as np
from jax import lax
from jax.experimental import pallas as pl
from jax.experimental.pallas import tpu as pltpu

F32 = jnp.float32
BF16 = jnp.bfloat16

D_MODEL = 1024
DEPTH = 2
PAST_LEN = 2048
PAGE_SIZE = 128
HEAD_DIM = 64
NSA_HEADS = 8
NSA_KV = 2
NSA_GROUP = NSA_HEADS // NSA_KV
CMP_LEN = 32
CMP_STRIDE = 16
CMP_HIDDEN = 128
SLC_BLOCK = 64
N_SEL = 16
WINDOW = 512
Q_BLOCK = 128
DN_HEADS = 4
DN_DK = 128
DN_DV = 128
DN_CHUNK = 64
CONV_W = 4
HG_HEADS = 4
HG_DK = 128
HG_DV = 128
HG_CHUNK = 16
D_FF = 2816
NORM_EPS = 1e-6
L2_EPS = 1e-6
NEG_BIG = -1e30
SEL_BIG = 1e9

NSA_W = NSA_HEADS * HEAD_DIM
KV_W = NSA_KV * HEAD_DIM
ROW_W = 2 * KV_W
DN_QK = DN_HEADS * DN_DK
DN_V = DN_HEADS * DN_DV
DN_QKV = 2 * DN_QK + DN_V
HG_WK = HG_HEADS * HG_DK
HG_WV = HG_HEADS * HG_DV
MIX_W = NSA_W + DN_V + HG_WV
IN_SPLITS = (NSA_W, 6 * KV_W, 3 * NSA_HEADS, DN_QKV, DN_HEADS, DN_HEADS, DN_V, HG_WK, HG_WV, HG_WK, HG_WV, 3 * D_MODEL)
IN_NAMES = ("a_q", "a_kv", "a_g", "d_qkv", "d_b", "d_a", "d_z", "r_f", "r_i", "r_q", "r_og", "m_g")
Z_ORDER = ("m_g", "d_qkv", "d_z", "r_f", "r_i", "r_q", "r_og", "a_q", "a_kv", "a_g", "d_b", "d_a")
Z_TN = 512
CHUNK_W = CMP_STRIDE * ROW_W
CMP_HID_W = 2 * NSA_KV * CMP_HIDDEN

V7X_VMEM_BYTES = 64 * 1024 * 1024
LANE = 128
ROW_TILE = 512
ROW_TILE_STREAMED = 1024


def _cparams(sem, vmem_mb):
    assert vmem_mb * 1024 * 1024 < V7X_VMEM_BYTES
    return pltpu.CompilerParams(dimension_semantics=sem, vmem_limit_bytes=vmem_mb * 1024 * 1024)


def _dot(a, b):
    return jnp.dot(a, b, preferred_element_type=F32)


def _dot_nt(a, b):
    return lax.dot_general(a, b, (((1,), (1,)), ((), ())), preferred_element_type=F32)


def _dot_tn(a, b):
    return lax.dot_general(a, b, (((0,), (0,)), ((), ())), preferred_element_type=F32)


def _split2(a):
    hi = a.astype(BF16)
    lo = (a - hi.astype(F32)).astype(BF16)
    return hi, lo


def _dot3(a, b):
    ah, al = _split2(a)
    bh, bl = _split2(b)
    return _dot(ah, bh) + (_dot(ah, bl) + _dot(al, bh))


def _rms(x, g):
    return x * lax.rsqrt(jnp.mean(x * x, axis=-1, keepdims=True) + NORM_EPS) * g


def _silu(x):
    return x * jax.nn.sigmoid(x)


def _iota(shape, dim):
    return lax.broadcasted_iota(jnp.int32, shape, dim)


def _alibi_slope(head_in_group, g):
    out = jnp.full(head_in_group.shape, 2.0 ** -(NSA_GROUP * g + NSA_GROUP), F32)
    for j in range(NSA_GROUP - 2, -1, -1):
        out = jnp.where(head_in_group == j, 2.0 ** -(NSA_GROUP * g + j + 1), out)
    return out


def _ffn_kernel(x_ref, g_ref, wg_ref, wu_ref, wd_ref, o_ref, xn_ref, acc_ref, *, nf):
    j = pl.program_id(1)

    @pl.when(j == 0)
    def _():
        xn_ref[...] = _rms(x_ref[...], g_ref[...]).astype(BF16)
        acc_ref[...] = jnp.zeros_like(acc_ref)

    xn = xn_ref[...]
    a = _silu(_dot(xn, wg_ref[...])) * _dot(xn, wu_ref[...])
    acc_ref[...] += _dot(a.astype(BF16), wd_ref[...])

    @pl.when(j == nf - 1)
    def _():
        o_ref[...] = x_ref[...] + 0.5 * acc_ref[...]


def _ffn(x, gain, w_gu, w_down, tm, tf=256):
    M, D = x.shape
    F = w_down.shape[0]
    nf = F // tf
    return pl.pallas_call(
        functools.partial(_ffn_kernel, nf=nf),
        grid=(M // tm, nf),
        in_specs=[
            pl.BlockSpec((tm, D), lambda i, j: (i, 0)),
            pl.BlockSpec((1, D), lambda i, j: (0, 0)),
            pl.BlockSpec((D, tf), lambda i, j: (0, j)),
            pl.BlockSpec((D, tf), lambda i, j: (0, j + nf)),
            pl.BlockSpec((tf, D), lambda i, j: (j, 0)),
        ],
        out_specs=pl.BlockSpec((tm, D), lambda i, j: (i, 0)),
        out_shape=jax.ShapeDtypeStruct((M, D), F32),
        scratch_shapes=[pltpu.VMEM((tm, D), BF16), pltpu.VMEM((tm, D), F32)],
        compiler_params=_cparams(("parallel", "arbitrary"), 40),
        name="ffn",
    )(x, gain.reshape(1, D), w_gu, w_gu, w_down)


def _inproj_kernel(x_ref, g_ref, w_ref, z_ref, xn_ref):
    @pl.when(pl.program_id(1) == 0)
    def _():
        xn_ref[...] = _rms(x_ref[...], g_ref[...]).astype(BF16)

    z_ref[...] = _dot(xn_ref[...], w_ref[...])


def _inproj(x, gain, w, tm):
    M, D = x.shape
    N = w.shape[1]
    return pl.pallas_call(
        _inproj_kernel,
        grid=(M // tm, N // Z_TN),
        in_specs=[
            pl.BlockSpec((tm, D), lambda i, j: (i, 0)),
            pl.BlockSpec((1, D), lambda i, j: (0, 0)),
            pl.BlockSpec((D, Z_TN), lambda i, j: (0, j)),
        ],
        out_specs=pl.BlockSpec((tm, Z_TN), lambda i, j: (i, j)),
        out_shape=jax.ShapeDtypeStruct((M, N), F32),
        scratch_shapes=[pltpu.VMEM((tm, D), BF16)],
        compiler_params=_cparams(("parallel", "arbitrary"), 40),
        name="inproj",
    )(x, gain.reshape(1, D), w)


def _z_layout():
    sizes = dict(zip(IN_NAMES, IN_SPLITS))
    src = dict(zip(IN_NAMES, np.cumsum((0,) + IN_SPLITS[:-1]).tolist()))
    offs, o = {}, 0
    for n in Z_ORDER:
        offs[n] = o
        o += sizes[n]
    total = -(-o // Z_TN) * Z_TN
    return sizes, src, offs, total


def _permute_w_in(w_in):
    sizes, src, _, total = _z_layout()
    cols = [w_in[:, src[n]:src[n] + sizes[n]] for n in Z_ORDER]
    w = jnp.concatenate(cols, axis=1)
    return jnp.pad(w, ((0, 0), (0, total - w.shape[1]))).astype(BF16)


def _merge_kernel(x_ref, oa_ref, od_ref, oh_ref, g0_ref, g1_ref, g2_ref, wb_ref, wo_ref, o_ref):
    m = jax.nn.sigmoid(g0_ref[...]) * _dot(oa_ref[...].astype(BF16), wb_ref[0:NSA_W, :])
    m += jax.nn.sigmoid(g1_ref[...]) * _dot(od_ref[...].astype(BF16), wb_ref[NSA_W:NSA_W + DN_V, :])
    m += jax.nn.sigmoid(g2_ref[...]) * _dot(oh_ref[...].astype(BF16), wb_ref[NSA_W + DN_V:MIX_W, :])
    o_ref[...] = x_ref[...] + _dot(m.astype(BF16), wo_ref[...])


def _merge(x, o_a, o_d, o_h, z, w_branch, w_out, tm):
    M, D = x.shape
    row = lambda w: pl.BlockSpec((tm, w), lambda i: (i, 0))
    return pl.pallas_call(
        _merge_kernel,
        grid=(M // tm,),
        in_specs=[
            row(D), row(NSA_W), row(DN_V), row(HG_WV),
            pl.BlockSpec((tm, D), lambda i: (i, 0)),
            pl.BlockSpec((tm, D), lambda i: (i, 1)),
            pl.BlockSpec((tm, D), lambda i: (i, 2)),
            pl.BlockSpec((MIX_W, D), lambda i: (0, 0)),
            pl.BlockSpec((D, D), lambda i: (0, 0)),
        ],
        out_specs=row(D),
        out_shape=jax.ShapeDtypeStruct((M, D), F32),
        compiler_params=_cparams(("parallel",), 40),
        name="merge",
    )(x, o_a, o_d, o_h, z, z, z, w_branch, w_out)


def _cmp_weights(pe, w1, w2):
    n_part = CMP_LEN // CMP_STRIDE
    eye = jnp.eye(NSA_KV, dtype=F32)
    eye2 = jnp.eye(2, dtype=F32)
    w1r = w1.reshape(2, n_part, CMP_STRIDE, HEAD_DIM, CMP_HIDDEN)
    w1e = jnp.einsum("kmrdh,kK,gG->mrkgdKGh", w1r, eye2, eye).reshape(n_part, CHUNK_W, CMP_HID_W)
    w2e = jnp.einsum("khd,kK,gG->kghKGd", w2, eye2, eye).reshape(CMP_HID_W, ROW_W)
    per = pe.reshape(2, n_part, CMP_STRIDE, HEAD_DIM).transpose(1, 2, 0, 3)
    pee = jnp.broadcast_to(per[:, :, :, None, :], (n_part, CMP_STRIDE, 2, NSA_KV, HEAD_DIM)).reshape(n_part, CHUNK_W)
    return pee, w1e.astype(BF16), w2e.astype(BF16)


def _compress(x, xnext_row, pe_ref, w1_ref, w2_ref, kg_ref):
    n = x.shape[0]
    p0 = _dot((x + pe_ref[0:1, :]).astype(BF16), w1_ref[0])
    p1 = _dot((x + pe_ref[1:2, :]).astype(BF16), w1_ref[1])
    p1s = pltpu.roll(p1, n - 1, 0)
    if xnext_row is not None:
        p1n = _dot((xnext_row + pe_ref[1:2, :]).astype(BF16), w1_ref[1])
        p1s = jnp.where(_iota((n, 1), 0) == n - 1, p1n[0:1, :], p1s)
    comp = _dot(_silu(p0 + p1s).astype(BF16), w2_ref[...])
    cks = []
    for g in range(NSA_KV):
        kc = comp[:, g * HEAD_DIM:(g + 1) * HEAD_DIM]
        cks.append(_rms(kc, kg_ref[...]))
    return cks, comp[:, KV_W:]


def _cmp_prompt_kernel(x_ref, pe_ref, w1_ref, w2_ref, kg_ref, ck_ref, cvt_ref):
    cks, cv = _compress(x_ref[0], None, pe_ref, w1_ref, w2_ref, kg_ref)
    for g in range(NSA_KV):
        ck_ref[0, g] = cks[g].astype(BF16)
    cvt = cv.T
    for g in range(NSA_KV):
        cvt_ref[0, g] = cvt[g * HEAD_DIM:(g + 1) * HEAD_DIM, :].astype(BF16)


def _cmp_prompt(cmp_rows, pee, w1e, w2e, kg0):
    B, T = cmp_rows.shape[:2]
    nc = T // CMP_STRIDE
    x = cmp_rows.reshape(B, nc, CHUNK_W)
    full = lambda s: pl.BlockSpec(s, lambda b: (0,) * len(s))
    return pl.pallas_call(
        _cmp_prompt_kernel,
        grid=(B,),
        in_specs=[pl.BlockSpec((1, nc, CHUNK_W), lambda b: (b, 0, 0)), full(pee.shape), full(w1e.shape),
                  full(w2e.shape), full((1, HEAD_DIM))],
        out_specs=[pl.BlockSpec((1, NSA_KV, nc, HEAD_DIM), lambda b: (b, 0, 0, 0)),
                   pl.BlockSpec((1, NSA_KV, HEAD_DIM, nc), lambda b: (b, 0, 0, 0))],
        out_shape=[jax.ShapeDtypeStruct((B, NSA_KV, nc, HEAD_DIM), BF16),
                   jax.ShapeDtypeStruct((B, NSA_KV, HEAD_DIM, nc), BF16)],
        compiler_params=_cparams(("parallel",), 48),
        name="nsa_compress",
    )(x, pee, w1e, w2e, kg0.reshape(1, HEAD_DIM))


def _softmax_first(s, vt):
    m = jnp.max(s, axis=0, keepdims=True)
    e = jnp.exp(s - m)
    return m, jnp.sum(e, axis=0, keepdims=True), _dot(vt, e.astype(BF16))


def _softmax_step(s, vt, carry):
    m, l, acc = carry
    m_new = jnp.maximum(m, jnp.max(s, axis=0, keepdims=True))
    alpha = jnp.exp(m - m_new)
    e = jnp.exp(s - m_new)
    l = alpha * l + jnp.sum(e, axis=0, keepdims=True)
    acc = alpha * acc + _dot(vt, e.astype(BF16))
    return m_new, l, acc


KC_W = 2 * LANE - HEAD_DIM
KC_POS = 64
KC_POS_RADIX = 128
KW = HEAD_DIM + KC_W


def _key_features(T):
    k = np.arange(T)
    f = np.zeros((T, KW), np.float32)
    f[k, HEAD_DIM + k // SLC_BLOCK] = 1.0
    f[:, HEAD_DIM + KC_POS] = k // KC_POS_RADIX
    f[:, HEAD_DIM + KC_POS + 1] = k % KC_POS_RADIX
    return f


def _nsa_prompt_kernel(qt_ref, ck_ref, cvt_ref, sk_ref, svt_ref, wk_ref, wvt_ref, gt_ref, ovl_ref, o_ref,
                       *, n_cmp, n_blk):
    qb = pl.program_id(1)
    b0 = qb * Q_BLOCK
    QW = NSA_GROUP * Q_BLOCK
    TK = Q_BLOCK
    lane = _iota((1, QW), 1)
    tq = lane % Q_BLOCK
    q_pos = b0 + tq
    head = lane // Q_BLOCK
    qp1 = b0 + _iota((1, Q_BLOCK), 1)
    bj = _iota((n_blk, 1), 0)
    krel = _iota((TK, 1), 0) - tq
    frow = _iota((KC_W, 1), 0)
    k_diag = pl.multiple_of(b0, TK)
    gt = gt_ref[0, 0]

    def gate(g, branch):
        return jnp.concatenate([gt[(NSA_GROUP * g + j) * 3 + branch:(NSA_GROUP * g + j) * 3 + branch + 1, :]
                                for j in range(NSA_GROUP)], axis=1)

    qc_slcs, qc_wins, o_cmps = [], [], []
    for g in range(NSA_KV):
        slope = _alibi_slope(head, g)
        qgt = jnp.concatenate([qt_ref[0, NSA_GROUP * g + j] for j in range(NSA_GROUP)], axis=1)
        pos_rows = jnp.where(frow == KC_POS, slope * float(KC_POS_RADIX), jnp.where(frow == KC_POS + 1, slope, 0.0))

        dist = q_pos - (_iota((n_cmp, 1), 0) * CMP_STRIDE + (CMP_LEN - 1))
        mask = dist >= 0
        s = jnp.where(mask, _dot(ck_ref[0, g], qgt) - slope * dist.astype(F32), NEG_BIG)
        m = jnp.max(s, axis=0, keepdims=True)
        e = jnp.where(mask, jnp.exp(s - m), 0.0)
        p = (e / jnp.maximum(jnp.sum(e, axis=0, keepdims=True), 1e-30)).astype(BF16)
        o_cmps.append(_dot(cvt_ref[0, g], p))

        imp = _dot(ovl_ref[...], p[:, 0:Q_BLOCK])
        for j in range(1, NSA_GROUP):
            imp += _dot(ovl_ref[...], p[:, j * Q_BLOCK:(j + 1) * Q_BLOCK])
        cur = qp1 // SLC_BLOCK
        forced = (bj == 0) | (bj == cur) | (bj == cur - 1)
        imp = jnp.where(forced, SEL_BIG, jnp.where(bj * SLC_BLOCK <= qp1, imp, -SEL_BIG))
        tiles = [imp[8 * v:8 * v + 8, :] for v in range(n_blk // 8)]
        ranks = [jnp.zeros((8, Q_BLOCK), F32) for _ in tiles]
        for i in range(n_blk):
            row = imp[i:i + 1, :]
            for v, tile in enumerate(tiles):
                if 8 * v > i:
                    beats = row >= tile
                elif 8 * v + 8 <= i:
                    beats = row > tile
                else:
                    beats = (row > tile) | ((row == tile) & (bj[8 * v:8 * v + 8] > i))
                ranks[v] = ranks[v] + jnp.where(beats, 1.0, 0.0)
        rank = jnp.concatenate(ranks, axis=0)
        selb = jnp.where(rank < float(min(N_SEL, n_blk)), 0.0, NEG_BIG)
        selb = jnp.concatenate([selb] * NSA_GROUP, axis=1)
        if n_blk < KC_W:
            selb = jnp.concatenate([selb, jnp.zeros((KC_W - n_blk, QW), F32)], axis=0)
        qc_slcs.append(jnp.concatenate([qgt, (selb + pos_rows).astype(BF16)], axis=0))
        qc_wins.append(jnp.concatenate([qgt, pos_rows.astype(BF16)], axis=0))

    def scores(k_ref, g, k0, n, qcs):
        return _dot(k_ref[0, g, pl.ds(k0, n), :], qcs[g])

    WS = WINDOW + Q_BLOCK
    ws = pl.multiple_of(jnp.maximum(b0 - WINDOW, 0), TK)
    off = b0 - ws
    drel = tq - _iota((WS, 1), 0)
    in_win = (drel >= -off) & (drel < WINDOW - off)
    o_wins = []
    for g in range(NSA_KV):
        s = jnp.where(in_win, scores(wk_ref, g, ws, WS, qc_wins), NEG_BIG)
        _, l, acc = _softmax_first(s, wvt_ref[0, g, :, pl.ds(ws, WS)])
        o_wins.append(acc / jnp.maximum(l, 1e-30))

    carry = []
    for g in range(NSA_KV):
        s = jnp.where(krel <= 0, scores(sk_ref, g, k_diag, TK, qc_slcs), NEG_BIG)
        carry.append(_softmax_first(s, svt_ref[0, g, :, pl.ds(k_diag, TK)]))

    def slc_body(n):
        def body(i, carry):
            k0 = pl.multiple_of(i * n, n)
            G = range(NSA_KV)
            ss = [scores(sk_ref, g, k0, n, qc_slcs) for g in G]
            ms = [jnp.maximum(carry[g][0], jnp.max(ss[g], axis=0, keepdims=True)) for g in G]
            es = [jnp.exp(ss[g] - ms[g]) for g in G]
            pvs = [_dot(svt_ref[0, g, :, pl.ds(k0, n)], es[g].astype(BF16)) for g in G]
            out = []
            for g in G:
                m, l, acc = carry[g]
                alpha = jnp.exp(m - ms[g])
                out.append((ms[g], alpha * l + jnp.sum(es[g], axis=0, keepdims=True), alpha * acc + pvs[g]))
            return tuple(out)
        return body

    pairs = qb // 2
    carry = lax.fori_loop(0, pairs, slc_body(2 * TK), tuple(carry))
    carry = lax.fori_loop(2 * pairs, qb, slc_body(TK), carry)

    outs = []
    for g in range(NSA_KV):
        _, l, acc = carry[g]
        o_slc = acc / jnp.maximum(l, 1e-30)
        outs.append(gate(g, 0) * o_cmps[g] + gate(g, 1) * o_slc + gate(g, 2) * o_wins[g])

    for g in range(NSA_KV):
        for jp in range(NSA_GROUP // 2):
            pair = jnp.concatenate([outs[g][:, (2 * jp) * Q_BLOCK:(2 * jp + 1) * Q_BLOCK],
                                    outs[g][:, (2 * jp + 1) * Q_BLOCK:(2 * jp + 2) * Q_BLOCK]], axis=0)
            c0 = (NSA_GROUP * g + 2 * jp) * HEAD_DIM
            o_ref[0, :, c0:c0 + 2 * HEAD_DIM] = pair.T.astype(o_ref.dtype)


def _overlap(n_cmp, n_blk):
    ci = np.arange(n_cmp)[:, None] * CMP_STRIDE
    bj = np.arange(n_blk)[None, :]
    return ((ci < (bj + 1) * SLC_BLOCK) & (ci + CMP_LEN > bj * SLC_BLOCK)).astype(np.float32)


def _nsa_prompt(qt, ck, cvt, sk, svt, wk, wvt, gates):
    B, _, _, T = qt.shape
    nq = T // Q_BLOCK
    n_cmp = ck.shape[2]
    n_blk = T // SLC_BLOCK
    assert n_blk <= KC_POS and T <= KC_POS_RADIX * 256 and T >= WINDOW + Q_BLOCK
    ovl = jnp.asarray(_overlap(n_cmp, n_blk).T, BF16)
    per_b = lambda s: pl.BlockSpec((1,) + s, lambda b, i: (b,) + (0,) * len(s))
    return pl.pallas_call(
        functools.partial(_nsa_prompt_kernel, n_cmp=n_cmp, n_blk=n_blk),
        grid=(B, nq),
        in_specs=[
            pl.BlockSpec((1, NSA_HEADS, HEAD_DIM, Q_BLOCK), lambda b, i: (b, 0, 0, i)),
            per_b((NSA_KV, n_cmp, HEAD_DIM)), per_b((NSA_KV, HEAD_DIM, n_cmp)),
            per_b((NSA_KV, T, KW)), per_b((NSA_KV, HEAD_DIM, T)),
            per_b((NSA_KV, T, KW)), per_b((NSA_KV, HEAD_DIM, T)),
            pl.BlockSpec((1, 1, GATE_ROWS, Q_BLOCK), lambda b, i: (b, i, 0, 0)),
            pl.BlockSpec((n_blk, n_cmp), lambda b, i: (0, 0)),
        ],
        out_specs=pl.BlockSpec((1, Q_BLOCK, NSA_W), lambda b, i: (b, i, 0)),
        out_shape=jax.ShapeDtypeStruct((B, T, NSA_W), BF16),
        compiler_params=_cparams(("parallel", "arbitrary"), 48),
        name="nsa_prompt",
    )(qt, ck, cvt, sk, svt, wk, wvt, gates, ovl)


GATE_ROWS = 32


def _pair_rms(x, gain2, ones_bd):
    hi, lo = _split2(x * x)
    ms = (_dot(hi, ones_bd) + _dot(lo, ones_bd)) * (1.0 / HEAD_DIM)
    return x * lax.rsqrt(ms + NORM_EPS) * gain2


def _nsa_prep_kernel(zq_ref, zc0_ref, zc1_ref, zs0_ref, zs1_ref, zw0_ref, zw1_ref, zg_ref, qg_ref, kg_ref, kf_ref,
                     qt_ref, cmp_ref, slc_ref, win_ref, sk_ref, svt_ref, wk_ref, wvt_ref, gt_ref):
    lane = _iota((1, LANE), 1)
    ones_bd = jnp.where(_iota((LANE, LANE), 0) // HEAD_DIM == _iota((LANE, LANE), 1) // HEAD_DIM, 1.0, 0.0).astype(BF16)
    for hp in range(NSA_HEADS // 2):
        qn = _pair_rms(zq_ref[0, :, hp * LANE:(hp + 1) * LANE], qg_ref[...], ones_bd) * (HEAD_DIM ** -0.5)
        qnt = qn.T
        qt_ref[0, 2 * hp] = qnt[0:HEAD_DIM].astype(BF16)
        qt_ref[0, 2 * hp + 1] = qnt[HEAD_DIM:].astype(BF16)
    cmp_ref[0, :, 0:KV_W] = zc0_ref[0]
    cmp_ref[0, :, KV_W:] = zc1_ref[0]
    feat = kf_ref[...]
    for i, (zk_ref, zv_ref, rows_ref, ka_ref, vt_ref) in enumerate(
            ((zs0_ref, zs1_ref, slc_ref, sk_ref, svt_ref), (zw0_ref, zw1_ref, win_ref, wk_ref, wvt_ref))):
        k = _pair_rms(zk_ref[0], kg_ref[i:i + 1, :], ones_bd)
        v = zv_ref[0]
        rows_ref[0, :, 0:KV_W] = k
        rows_ref[0, :, KV_W:] = v
        for g in range(NSA_KV):
            kg = k if g == 0 else pltpu.roll(k, HEAD_DIM, 1)
            ka_ref[0, g, :, 0:LANE] = jnp.where(lane < HEAD_DIM, kg.astype(BF16), feat[:, 0:LANE])
            ka_ref[0, g, :, LANE:] = feat[:, LANE:]
        vt = v.T
        for g in range(NSA_KV):
            vt_ref[0, g] = vt[g * HEAD_DIM:(g + 1) * HEAD_DIM].astype(BF16)
    gt_ref[0, 0] = jax.nn.sigmoid(zg_ref[0]).T[0:GATE_ROWS]


def _nsa_prep(z, q_gain, k_gains, offs):
    B, T, _ = z.shape
    nq = T // Q_BLOCK
    assert offs["a_q"] % NSA_W == 0 and offs["a_kv"] % LANE == 0 and offs["a_g"] % LANE == 0
    kv0 = offs["a_kv"] // LANE
    zcol = lambda c: pl.BlockSpec((1, Q_BLOCK, LANE), lambda b, i: (b, i, c))
    full = lambda s: pl.BlockSpec(s, lambda b, i: (0,) * len(s))
    rows = pl.BlockSpec((1, Q_BLOCK, ROW_W), lambda b, i: (b, i, 0))
    keys = pl.BlockSpec((1, NSA_KV, Q_BLOCK, KW), lambda b, i: (b, 0, i, 0))
    vals = pl.BlockSpec((1, NSA_KV, HEAD_DIM, Q_BLOCK), lambda b, i: (b, 0, 0, i))
    two = lambda gain: jnp.concatenate([gain, gain], axis=-1)
    return pl.pallas_call(
        _nsa_prep_kernel,
        grid=(B, nq),
        in_specs=[pl.BlockSpec((1, Q_BLOCK, NSA_W), lambda b, i, c=offs["a_q"] // NSA_W: (b, i, c))]
        + [zcol(kv0 + j) for j in range(6)] + [zcol(offs["a_g"] // LANE)]
        + [full((1, LANE)), full((2, LANE)), pl.BlockSpec((Q_BLOCK, KW), lambda b, i: (i, 0))],
        out_specs=[pl.BlockSpec((1, NSA_HEADS, HEAD_DIM, Q_BLOCK), lambda b, i: (b, 0, 0, i)), rows, rows, rows,
                   keys, vals, keys, vals, pl.BlockSpec((1, 1, GATE_ROWS, Q_BLOCK), lambda b, i: (b, i, 0, 0))],
        out_shape=[jax.ShapeDtypeStruct((B, NSA_HEADS, HEAD_DIM, T), BF16)]
        + [jax.ShapeDtypeStruct((B, T, ROW_W), F32)] * 3
        + [jax.ShapeDtypeStruct((B, NSA_KV, T, KW), BF16), jax.ShapeDtypeStruct((B, NSA_KV, HEAD_DIM, T), BF16)] * 2
        + [jax.ShapeDtypeStruct((B, nq, GATE_ROWS, Q_BLOCK), F32)],
        compiler_params=_cparams(("parallel", "parallel"), 32),
        name="nsa_prep",
    )(z, z, z, z, z, z, z, z, two(q_gain).reshape(1, LANE), two(k_gains[1:3]), jnp.asarray(_key_features(T), BF16))


def _nsa_decode_kernel(pt_ref, q_ref, new_ref, gt_ref, pe_ref, w1_ref, w2_ref, kg_ref, ovl_ref, exp_ref, *rest,
                       n_pages):
    del pt_ref
    cmp_pages = rest[:n_pages]
    slc_pages = rest[n_pages:2 * n_pages]
    win_ref, o_ref, rows_ref, x_ref = rest[2 * n_pages:2 * n_pages + 4]
    n_past = n_pages * PAGE_SIZE
    n_cmp = n_past // CMP_STRIDE
    q_pos = n_past
    new = new_ref[0]
    halves = ROW_W // LANE
    for p, r in enumerate(cmp_pages):
        for c in range(halves):
            rows_ref[c, p * PAGE_SIZE:(p + 1) * PAGE_SIZE, :] = r[0, 0, c * LANE:(c + 1) * LANE, :].T
    for r in range(CMP_STRIDE):
        for c in range(halves):
            x_ref[:, r * ROW_W + c * LANE:r * ROW_W + (c + 1) * LANE] = rows_ref.at[c][
                pl.ds(r, n_cmp, stride=CMP_STRIDE), :]
    xnew = jnp.concatenate([new[:, 0:ROW_W], jnp.zeros((8, CHUNK_W - ROW_W), F32)], axis=1)
    cks, cv = _compress(x_ref[...], xnew, pe_ref, w1_ref, w2_ref, kg_ref)

    jrow = _iota((8, 1), 0)
    lanes = _iota((1, LANE), 1)
    n_blk = (n_past + 1 + SLC_BLOCK - 1) // SLC_BLOCK
    cur = q_pos // SLC_BLOCK
    ii = _iota((LANE, LANE), 0)
    jj = _iota((LANE, LANE), 1)
    for g in range(NSA_KV):
        slope = _alibi_slope(jrow, g)
        qg = q_ref[0, g]
        lo, hi = g * HEAD_DIM, (g + 1) * HEAD_DIM

        dist = q_pos - (lanes * CMP_STRIDE + (CMP_LEN - 1))
        mask = dist >= 0
        s = jnp.where(mask, _dot_nt(qg, cks[g].astype(BF16)) - slope * dist.astype(F32), NEG_BIG)
        m = jnp.max(s, axis=1, keepdims=True)
        e = jnp.where(mask, jnp.exp(s - m), 0.0)
        p = (e / jnp.maximum(jnp.sum(e, axis=1, keepdims=True), 1e-30)).astype(BF16)
        o_cmp = _dot(p, cv[:, lo:hi].astype(BF16))

        imp = jnp.sum(jnp.where(jrow < NSA_GROUP, _dot(p, ovl_ref[...]), 0.0), axis=0, keepdims=True)
        forced = (lanes == 0) | (lanes == cur) | (lanes == cur - 1)
        imp = jnp.where(forced, SEL_BIG, jnp.where(lanes * SLC_BLOCK <= q_pos, imp, -SEL_BIG))
        imp = jnp.where(lanes < n_blk, imp, -3e38)
        impr = jnp.broadcast_to(imp, (LANE, LANE))
        impc = jnp.sum(jnp.where(ii == jj, impr, 0.0), axis=1, keepdims=True)
        beats = jnp.where(impc > impr, 1.0, jnp.where((impc == impr) & (ii < jj), 1.0, 0.0))
        rank = jnp.sum(beats, axis=0, keepdims=True)
        sel = jnp.where(rank < float(min(N_SEL, n_blk)), 1.0, 0.0)
        selk = _dot(jnp.broadcast_to(sel, (8, LANE)).astype(BF16), exp_ref[...])

        def attend(kt, vt, mask, dist, knew, vnew, mask_new):
            s = jnp.where(mask, _dot(qg, kt) - slope * dist.astype(F32), NEG_BIG)
            s_new = jnp.sum(qg.astype(F32) * knew.astype(BF16).astype(F32), axis=1, keepdims=True)
            if mask_new is not None:
                s_new = jnp.where(mask_new, s_new, NEG_BIG)
            m = jnp.maximum(jnp.max(s, axis=1, keepdims=True), s_new)
            e = jnp.where(mask, jnp.exp(s - m), 0.0)
            e_new = jnp.exp(s_new - m)
            if mask_new is not None:
                e_new = jnp.where(mask_new, e_new, 0.0)
            l = jnp.sum(e, axis=1, keepdims=True) + e_new
            o = _dot_nt(e.astype(BF16), vt) + e_new.astype(BF16).astype(F32) * vnew.astype(BF16).astype(F32)
            return o / jnp.maximum(l, 1e-30)

        kpos = _iota((1, n_past), 1)
        skt = jnp.concatenate([r[0, 0, lo:hi, :] for r in slc_pages], axis=1).astype(BF16)
        svt = jnp.concatenate([r[0, 0, KV_W + lo:KV_W + hi, :] for r in slc_pages], axis=1).astype(BF16)
        o_slc = attend(skt, svt, selk > 0.5, q_pos - kpos, new[1:2, lo:hi], new[1:2, KV_W + lo:KV_W + hi],
                       sel[:, cur:cur + 1] > 0.5)

        n_win = win_ref.shape[3]
        wpos = (q_pos - n_win) + _iota((1, n_win), 1)
        dist_w = q_pos - wpos
        o_win = attend(win_ref[0, 0, lo:hi, :].astype(BF16), win_ref[0, 0, KV_W + lo:KV_W + hi, :].astype(BF16),
                       (dist_w < WINDOW) & (wpos >= 0), dist_w, new[2:3, lo:hi], new[2:3, KV_W + lo:KV_W + hi],
                       None)

        gt = gt_ref[0, g]
        o_ref[0, g] = gt[:, 0:1] * o_cmp + gt[:, 1:2] * o_slc + gt[:, 2:3] * o_win


def _rows_minor(cache):
    lead = cache.ndim - 4
    perm = tuple(range(lead)) + (lead + 1, lead + 2, lead + 3, lead)
    return cache.transpose(perm).reshape(cache.shape[:lead] + (ROW_W, cache.shape[lead]))


def _nsa_decode(page_table, l, q, new_rows, gates, cmp_t, slc_t, win_t, pee, w1e, w2e, kg0):
    B, n_pages = page_table.shape
    n_past = n_pages * PAGE_SIZE
    n_cmp = n_past // CMP_STRIDE
    n_win = win_t.shape[3]
    assert n_cmp == LANE and n_win <= n_past
    n_blk = (n_past + 1 + SLC_BLOCK - 1) // SLC_BLOCK
    ovl = np.zeros((n_cmp, LANE), np.float32)
    ovl[:, :n_blk] = _overlap(n_cmp, n_blk)
    expand = (np.arange(n_past)[None, :] // SLC_BLOCK == np.arange(LANE)[:, None]).astype(np.float32)
    full = lambda s: pl.BlockSpec(s, lambda b, pt: (0,) * len(s))
    per_b = lambda s: pl.BlockSpec((1,) + s, lambda b, pt: (b,) + (0,) * len(s))
    page = lambda p: pl.BlockSpec((1, 1, ROW_W, PAGE_SIZE), lambda b, pt: (l, pt[b, p], 0, 0))

    in_specs = [per_b((NSA_KV, 8, HEAD_DIM)), per_b((8, ROW_W)), per_b((NSA_KV, 8, LANE)), full(pee.shape),
                full(w1e.shape), full(w2e.shape), full((1, HEAD_DIM)), full(ovl.shape), full(expand.shape)]
    in_specs += [page(p) for p in range(n_pages)] * 2
    in_specs += [pl.BlockSpec((1, 1, ROW_W, n_win), lambda b, pt: (l, b, 0, 0))]
    return pl.pallas_call(
        functools.partial(_nsa_decode_kernel, n_pages=n_pages),
        grid_spec=pltpu.PrefetchScalarGridSpec(
            num_scalar_prefetch=1, grid=(B,), in_specs=in_specs,
            out_specs=pl.BlockSpec((1, NSA_KV, 8, HEAD_DIM), lambda b, pt: (b, 0, 0, 0)),
            scratch_shapes=[pltpu.VMEM((ROW_W // LANE, n_past, LANE), F32), pltpu.VMEM((n_cmp, CHUNK_W), F32)]),
        out_shape=jax.ShapeDtypeStruct((B, NSA_KV, 8, HEAD_DIM), F32),
        compiler_params=_cparams(("arbitrary",), 48),
        name="nsa_decode",
    )(page_table, q, new_rows, gates, pee, w1e, w2e, kg0.reshape(1, HEAD_DIM), jnp.asarray(ovl, BF16),
      jnp.asarray(expand, BF16), *([cmp_t] * n_pages), *([slc_t] * n_pages), win_t)


DN_TB = 2 * DN_CHUNK
HALO = 8


def _softplus(x):
    return jnp.maximum(x, 0.0) + jnp.log(1.0 + jnp.exp(-jnp.abs(x)))


def _dn_prompt_kernel(zq_ref, zk_ref, zv_ref, zz_ref, zs_ref, cw_ref, a_ref, dtb_ref, gain_ref, o_ref, s_out_ref,
                      s_ref, buf_ref, *, n_step, col_b, col_a):
    t = pl.program_id(1)
    TB, C = DN_TB, DN_CHUNK

    @pl.when(t == 0)
    def _():
        s_ref[...] = jnp.zeros_like(s_ref)
        buf_ref[0:HALO, :] = jnp.zeros((HALO, DN_QKV), F32)

    @pl.when(t > 0)
    def _():
        buf_ref[0:HALO, :] = buf_ref[TB:TB + HALO, :]

    buf_ref[HALO:HALO + TB, 0:DN_QK] = zq_ref[0]
    buf_ref[HALO:HALO + TB, DN_QK:2 * DN_QK] = zk_ref[0]
    buf_ref[HALO:HALO + TB, 2 * DN_QK:] = zv_ref[0]
    cw = cw_ref[...]
    first = HALO - (CONV_W - 1)
    y = buf_ref[pl.ds(first, TB), :] * cw[0:1, :]
    for j in range(1, CONV_W):
        y += buf_ref[pl.ds(first + j, TB), :] * cw[j:j + 1, :]
    act = _silu(y)

    zs = zs_ref[0]
    beta_all = jax.nn.sigmoid(zs)
    g_all = -a_ref[...] * _softplus(zs + dtb_ref[...])
    zz = zz_ref[0]
    gain = gain_ref[...]

    ii = _iota((C, C), 0)
    jj = _iota((C, C), 1)
    incl = ii >= jj
    eye = (ii == jj).astype(F32)
    n_sub = TB // C
    ch = []
    for sc in range(n_sub):
        rows = slice(sc * C, (sc + 1) * C)
        for h in range(DN_HEADS):
            cs = slice(h * DN_DK, (h + 1) * DN_DK)
            q = _l2n(act[rows, cs]) * (DN_DK ** -0.5)
            k = _l2n(act[rows, DN_QK + h * DN_DK:DN_QK + (h + 1) * DN_DK])
            v = act[rows, 2 * DN_QK + h * DN_DV:2 * DN_QK + (h + 1) * DN_DV]
            gcol = g_all[rows, col_a + h:col_a + h + 1]
            bcol = beta_all[rows, col_b + h:col_b + h + 1]
            grow = jnp.sum(jnp.where(ii <= jj, gcol, 0.0), axis=0, keepdims=True)
            gcum = jnp.sum(jnp.where(ii == jj, grow, 0.0), axis=1, keepdims=True)
            decay = jnp.where(incl, jnp.exp(jnp.where(incl, gcum - grow, 0.0)), 0.0)
            kb = k.astype(BF16)
            ch.append(dict(q=q, k=k, v=v, kb=kb, bcol=bcol, gcum=gcum, decay=decay, sc=sc, h=h, rows=rows, cs=cs))
    for c in ch:
        c["npow"] = -jnp.where(ii > jj, c["bcol"] * _dot_nt(c["kb"], c["kb"]) * c["decay"], 0.0)
        c["tinv"] = eye + c["npow"]
    for _ in range(int(np.log2(C)) - 1):
        for c in ch:
            nb = c["npow"].astype(BF16)
            c["npow"] = _dot(nb, nb)
        for c in ch:
            c["tinv"] = c["tinv"] + _dot(c["tinv"].astype(BF16), c["npow"].astype(BF16))
    for c in ch:
        tb = c["tinv"].astype(BF16)
        c["eg"] = jnp.exp(c["gcum"])
        c["u"] = _dot(tb, (c["v"] * c["bcol"]).astype(BF16))
        c["w"] = _dot(tb, (c["k"] * (c["bcol"] * c["eg"])).astype(BF16)).astype(BF16)
        c["attn"] = (_dot_nt(c["q"].astype(BF16), c["kb"]) * c["decay"]).astype(BF16)
    for sc in range(n_sub):
        cur = [c for c in ch if c["sc"] == sc]
        Ss = [s_ref[c["h"]] for c in cur]
        Sbs = [S.astype(BF16) for S in Ss]
        vns = [c["u"] - _dot(c["w"], Sb) for c, Sb in zip(cur, Sbs)]
        for c, S, Sb, v_new in zip(cur, Ss, Sbs, vns):
            vnb = v_new.astype(BF16)
            o = _dot((c["q"] * c["eg"]).astype(BF16), Sb) + _dot(c["attn"], vnb)
            g_last = c["gcum"][C - 1:C, :]
            s_ref[c["h"]] = S * jnp.exp(g_last) + _dot_tn((c["k"] * jnp.exp(g_last - c["gcum"])).astype(BF16), vnb)
            o_ref[0, c["rows"], c["cs"]] = (_rms(o, gain) * _silu(zz[c["rows"], c["cs"]])).astype(o_ref.dtype)

    @pl.when(t == n_step - 1)
    def _():
        s_out_ref[0] = s_ref[...]


def _dn_prompt(z, conv_w, a_log, dt_bias, out_gain, offs):
    B, T, _ = z.shape
    n_step = T // DN_TB
    wide = lambda name, k=0: pl.BlockSpec((1, DN_TB, DN_QK), lambda b, t, c=offs[name] // DN_QK + k: (b, t, c))
    assert offs["d_qkv"] % DN_QK == 0 and offs["d_z"] % DN_V == 0 and offs["a_g"] % LANE == 0
    col_b, col_a = offs["d_b"] - offs["a_g"], offs["d_a"] - offs["a_g"]
    lanes = lambda vals, col: jnp.zeros((1, LANE), F32).at[0, col:col + DN_HEADS].set(vals)
    full = lambda s: pl.BlockSpec(s, lambda b, t: (0,) * len(s))
    return pl.pallas_call(
        functools.partial(_dn_prompt_kernel, n_step=n_step, col_b=col_b, col_a=col_a),
        grid=(B, n_step),
        in_specs=[wide("d_qkv", 0), wide("d_qkv", 1), wide("d_qkv", 2), wide("d_z"),
                  pl.BlockSpec((1, DN_TB, LANE), lambda b, t, c=offs["a_g"] // LANE: (b, t, c)),
                  full((CONV_W, DN_QKV)), full((1, LANE)), full((1, LANE)), full((1, DN_DV))],
        out_specs=[pl.BlockSpec((1, DN_TB, DN_V), lambda b, t: (b, t, 0)),
                   pl.BlockSpec((1, DN_HEADS, DN_DK, DN_DV), lambda b, t: (b, 0, 0, 0))],
        out_shape=[jax.ShapeDtypeStruct((B, T, DN_V), BF16), jax.ShapeDtypeStruct((B, DN_HEADS, DN_DK, DN_DV), F32)],
        scratch_shapes=[pltpu.VMEM((DN_HEADS, DN_DK, DN_DV), F32), pltpu.VMEM((HALO + DN_TB, DN_QKV), F32)],
        compiler_params=_cparams(("parallel", "arbitrary"), 32),
        name="dn_prompt",
    )(z, z, z, z, z, conv_w, lanes(jnp.exp(a_log), col_a), lanes(dt_bias, col_a), out_gain.reshape(1, DN_DV))


def _row0(x):
    return jnp.where(_iota(x.shape, 0) == 0, x, 0.0)


def _dn_decode_kernel(vec_ref, s_ref, o_ref, s_out_ref, *, bb):
    def body(b, _):
        for h in range(DN_HEADS):
            x = vec_ref[b, h]
            S = s_ref[b, h]
            xs = _dot(x.astype(BF16), S.astype(BF16))
            k, q, v, eg, beta = x[0:1], x[1:2], x[2:3], x[3:4], x[4:5]
            v_new = beta * (v - eg * xs[0:1])
            kb = k.astype(BF16).astype(F32)
            qk = jnp.sum(q.astype(BF16).astype(F32) * kb, axis=1, keepdims=True)
            o = eg * xs[1:2] + qk.astype(BF16).astype(F32) * v_new.astype(BF16).astype(F32)
            o_ref[b, h] = jnp.broadcast_to(o, (8, DN_DV))
            s_out_ref[b, h] = S * eg[:, 0:1] + _dot_tn(_row0(x).astype(BF16),
                                                      _row0(jnp.broadcast_to(v_new, (8, DN_DV))).astype(BF16))
        return 0

    lax.fori_loop(0, bb, body, 0)


def _state_step(kernel_fn, name, vec, S_all, l, bb=8):
    B, H = vec.shape[:2]
    spec_v = pl.BlockSpec((bb, H, 8, vec.shape[3]), lambda i: (i, 0, 0, 0))
    spec_s = pl.BlockSpec((bb, H) + S_all.shape[3:], lambda i: (i, 0, 0, 0))
    spec_sl = pl.BlockSpec((None, bb, H) + S_all.shape[3:], lambda i: (l, i, 0, 0, 0))
    return pl.pallas_call(
        functools.partial(kernel_fn, bb=bb),
        grid=(B // bb,),
        in_specs=[spec_v, spec_sl],
        out_specs=[spec_v, spec_s],
        out_shape=[jax.ShapeDtypeStruct(vec.shape, F32), jax.ShapeDtypeStruct(S_all.shape[1:], F32)],
        compiler_params=_cparams(("parallel",), 32),
        name=name,
    )(vec, S_all)


def _dn_decode(vec, S_all, l):
    return _state_step(_dn_decode_kernel, "dn_decode", vec, S_all, l)


HG_TB = 128
HG_LEVELS = tuple(HG_TB >> (i + 1) for i in range(int(np.log2(HG_TB))))


def _hg_tables():
    r = np.arange(HG_TB)
    tril = (r[:, None] >= r[None, :]).astype(np.float32)
    mats = [tril] + [tril[(r // (2 * w)) * 2 * w + w - 1] for w in HG_LEVELS]
    level = np.full((HG_TB, HG_TB), -1, np.int32)
    for i, w in enumerate(HG_LEVELS):
        same_block = r[:, None] // (2 * w) == r[None, :] // (2 * w)
        split = (r[:, None] % (2 * w) >= w) & (r[None, :] % (2 * w) < w)
        level[same_block & split] = i
    return np.concatenate(mats, axis=0), level


def _hg_prompt_kernel(zf_ref, zi_ref, zq_ref, zo_ref, lb_ref, gain_ref, sel_ref, lvl_ref, o_ref, s_out_ref, st_ref,
                      *, n_step):
    t = pl.program_id(1)
    TB = HG_TB

    @pl.when(t == 0)
    def _():
        st_ref[...] = jnp.zeros_like(st_ref)

    zf = zf_ref[0]
    lb = lb_ref[...]
    lf = jnp.log(lb + (1.0 - lb) * jax.nn.sigmoid(zf))
    k = (1.0 - lb) * jax.nn.sigmoid(-zf)
    q = _silu(zq_ref[0]) * (HG_DK ** -0.5)
    vb = zi_ref[0].astype(BF16)
    gain = gain_ref[...]
    hi = lf.astype(BF16)
    r1 = lf - hi.astype(F32)
    mid = r1.astype(BF16)
    lo = (r1 - mid.astype(F32)).astype(BF16)
    sel = sel_ref[...]
    gg = _dot(sel, hi) + (_dot(sel, mid) + _dot(sel, lo))
    G = gg[0:TB]
    row = _iota((TB, 1), 0)
    qts, kts = [], []
    for i, w in enumerate(HG_LEVELS):
        d = G - gg[(i + 1) * TB:(i + 2) * TB]
        right = (row % (2 * w)) >= w
        qts.append(jnp.where(right, q * jnp.exp(jnp.minimum(d, 0.0)), 0.0).astype(BF16))
        kts.append(jnp.where(right, 0.0, k * jnp.exp(jnp.minimum(-d, 0.0))).astype(BF16))
    lvl = lvl_ref[...]
    eye = _iota((TB, TB), 0) == _iota((TB, TB), 1)
    qe = (q * jnp.exp(G)).astype(BF16)
    g_last = G[TB - 1:TB, :]
    kd = (k * jnp.exp(g_last - G)).astype(BF16)
    eg_last = jnp.exp(g_last)
    qk = q * k
    for h in range(HG_HEADS):
        cs = slice(h * HG_DK, (h + 1) * HG_DK)
        a = jnp.where(eye, jnp.sum(qk[:, cs], axis=1, keepdims=True), 0.0)
        for i in range(len(HG_LEVELS)):
            a = a + jnp.where(lvl == i, _dot_nt(qts[i][:, cs], kts[i][:, cs]), 0.0)
        st = st_ref[h]
        o = _dot_nt(qe[:, cs], st.astype(BF16)) + _dot(a.astype(BF16), vb[:, cs])
        o_ref[0, :, cs] = (_rms(o, gain) * jax.nn.sigmoid(zo_ref[0, :, cs])).astype(o_ref.dtype)
        st_ref[h] = st * eg_last[:, cs] + _dot_tn(vb[:, cs], kd[:, cs])

    @pl.when(t == n_step - 1)
    def _():
        for h in range(HG_HEADS):
            s_out_ref[0, h] = st_ref[h].T


def _hg_prompt(z, lb, out_gain, offs):
    B, T, _ = z.shape
    tb = HG_TB
    n_step = T // tb
    sel, level = _hg_tables()
    assert all(offs[n] % HG_WK == 0 for n in ("r_f", "r_i", "r_q", "r_og"))
    col = lambda name: pl.BlockSpec((1, tb, HG_WK), lambda b, t, c=offs[name] // HG_WK: (b, t, c))
    full = lambda s: pl.BlockSpec(s, lambda b, t: (0,) * len(s))
    return pl.pallas_call(
        functools.partial(_hg_prompt_kernel, n_step=n_step),
        grid=(B, n_step),
        in_specs=[col("r_f"), col("r_i"), col("r_q"), col("r_og"), full((1, HG_WK)), full((1, HG_DV)),
                  full(sel.shape), full(level.shape)],
        out_specs=[pl.BlockSpec((1, tb, HG_WV), lambda b, t: (b, t, 0)),
                   pl.BlockSpec((1, HG_HEADS, HG_DK, HG_DV), lambda b, t: (b, 0, 0, 0))],
        out_shape=[jax.ShapeDtypeStruct((B, T, HG_WV), BF16), jax.ShapeDtypeStruct((B, HG_HEADS, HG_DK, HG_DV), F32)],
        scratch_shapes=[pltpu.VMEM((HG_HEADS, HG_DV, HG_DK), F32)],
        compiler_params=_cparams(("parallel", "arbitrary"), 32),
        name="hg_prompt",
    )(z, z, z, z, lb.reshape(1, HG_WK), out_gain.reshape(1, HG_DV), jnp.asarray(sel, BF16), jnp.asarray(level))


def _hg_decode_kernel(vec_ref, s_ref, o_ref, s_out_ref, *, bb):
    ii = _iota((HG_DK, HG_DK), 0)
    jj = _iota((HG_DK, HG_DK), 1)

    def body(b, _):
        for h in range(HG_HEADS):
            x = vec_ref[b, h]
            S = s_ref[b, h]
            k, q, v, lf = x[0:1], x[1:2], x[2:3], x[3:4]
            f = jnp.exp(lf)
            qs = _dot(jnp.broadcast_to(q * f, (8, HG_DK)).astype(BF16), S.astype(BF16))
            a = jnp.sum(q * k, axis=1, keepdims=True)
            o = qs[0:1] + a.astype(BF16).astype(F32) * v.astype(BF16).astype(F32)
            o_ref[b, h] = jnp.broadcast_to(o, (8, HG_DV))
            fcol = jnp.sum(jnp.where(ii == jj, f, 0.0), axis=1, keepdims=True)
            s_out_ref[b, h] = S * fcol + _dot_tn(_row0(x).astype(BF16),
                                                 _row0(jnp.broadcast_to(v, (8, HG_DV))).astype(BF16))
        return 0

    lax.fori_loop(0, bb, body, 0)


def _hg_decode(vec, S_all, l):
    return _state_step(_hg_decode_kernel, "hg_decode", vec, S_all, l)


def _head_rms(x, g):
    return x * lax.rsqrt(jnp.mean(x * x, axis=-1, keepdims=True) + NORM_EPS) * g


def _l2n(x):
    return x * lax.rsqrt(jnp.sum(x * x, axis=-1, keepdims=True) + L2_EPS)


def _rows8(rows):
    x = jnp.stack(rows, axis=-2)
    pad = [(0, 0)] * x.ndim
    pad[-2] = (0, 8 - len(rows))
    return jnp.pad(x, pad)


def _layer(x, l, prm, wts, past, page_table):
    B, T, D = x.shape
    M = B * T
    tm = min(M, ROW_TILE)
    tm_w = min(M, ROW_TILE_STREAMED)
    x2 = x.reshape(M, D)
    x2 = _ffn(x2, prm["ffn1_norm"][l], wts["ffn1_w_gu"][l], wts["ffn1_w_down"][l], tm_w)
    z = _inproj(x2, prm["mix_norm"][l], wts["w_in"][l], tm_w)
    sizes, _, offs, _ = _z_layout()
    pee, w1e, w2e = wts["cmp"][l]
    kg = prm["nsa_k_norm"][l]
    p = jax.nn.softmax(prm["hg_lb_logits"], axis=0)
    lb = (jnp.cumsum(p, axis=0) - p[0])[l]
    if past is None:
        z3 = z.reshape(B, T, -1)
        qt, cmp_new, slc_new, win_new, sk, svt, wk, wvt, gt = _nsa_prep(z3, prm["nsa_q_norm"][l], kg, offs)
        ck, cvt = _cmp_prompt(cmp_new, pee, w1e, w2e, kg[0])
        o_a = _nsa_prompt(qt, ck, cvt, sk, svt, wk, wvt, gt)
        o_d, dn_state = _dn_prompt(z3, prm["dn_conv_w"][l], prm["dn_A_log"][l], prm["dn_dt_bias"][l],
                                   prm["dn_out_norm"][l], offs)
        o_h, hg_state = _hg_prompt(z3, lb, prm["hg_out_norm"][l], offs)
        rows = lambda a: a.reshape(B, -1, 2, NSA_KV, HEAD_DIM)
        cmp_new, slc_new, win_state = rows(cmp_new), rows(slc_new), rows(win_new[:, T - min(WINDOW, T):])
        conv_state = z3[:, T - (CONV_W - 1):, offs["d_qkv"]:offs["d_qkv"] + DN_QKV]
    else:
        zs = {n: z[:, offs[n]:offs[n] + sizes[n]].reshape(B, T, sizes[n]) for n in Z_ORDER if n != "m_g"}
        (o_a, o_d, o_h), (cmp_new, slc_new, win_state, conv_state, dn_state, hg_state) = _decode_mixers(
            zs, l, prm, (pee, w1e, w2e), lb, past, page_table)

    x2 = _merge(x2, o_a.reshape(M, NSA_W), o_d.reshape(M, DN_V), o_h.reshape(M, HG_WV), z, wts["w_branch"][l],
                wts["w_out"][l], tm)
    x2 = _ffn(x2, prm["ffn2_norm"][l], wts["ffn2_w_gu"][l], wts["ffn2_w_down"][l], tm_w)
    return x2.reshape(B, T, D), (cmp_new, slc_new, win_state, conv_state, dn_state, hg_state)


def _decode_mixers(zs, l, prm, cmp_w, lb, past, page_table):
    B, T = zs["a_q"].shape[:2]
    pee, w1e, w2e = cmp_w
    kg = prm["nsa_k_norm"][l]
    q = _head_rms(zs["a_q"].reshape(B, T, NSA_HEADS, HEAD_DIM), prm["nsa_q_norm"][l]) * (HEAD_DIM ** -0.5)
    kv = zs["a_kv"].reshape(B, T, 3, 2, NSA_KV, HEAD_DIM)
    cmp_new = kv[:, :, 0]
    slc_new = jnp.stack([_head_rms(kv[:, :, 1, 0], kg[1]), kv[:, :, 1, 1]], axis=2)
    win_new = jnp.stack([_head_rms(kv[:, :, 2, 0], kg[2]), kv[:, :, 2, 1]], axis=2)
    gates = jax.nn.sigmoid(zs["a_g"].reshape(B, T, NSA_HEADS, 3))
    qd = q.reshape(B, NSA_KV, NSA_GROUP, HEAD_DIM)
    qd = jnp.pad(qd, ((0, 0), (0, 0), (0, 8 - NSA_GROUP), (0, 0))).astype(BF16)
    new_rows = _rows8([cmp_new.reshape(B, ROW_W), slc_new.reshape(B, ROW_W), win_new.reshape(B, ROW_W)])
    gd = gates.reshape(B, NSA_KV, NSA_GROUP, 3)
    gd = jnp.pad(gd, ((0, 0), (0, 0), (0, 8 - NSA_GROUP), (0, LANE - 3)))
    o8 = _nsa_decode(page_table, l, qd, new_rows, gd, past["cmp"], past["slc"], past["win"], pee, w1e, w2e, kg[0])
    o_a = o8[:, :, :NSA_GROUP].reshape(B, T, NSA_W)

    d_qkv = zs["d_qkv"]
    xx = jnp.concatenate([past["conv"][l], d_qkv], axis=1)
    cw = prm["dn_conv_w"][l]
    qkv = sum(xx[:, j:j + T] * cw[j] for j in range(CONV_W))
    conv_state = xx[:, -(CONV_W - 1):]
    dq, dk, dv = jnp.split(jax.nn.silu(qkv), [DN_QK, 2 * DN_QK], axis=-1)
    dq = _l2n(dq.reshape(B, T, DN_HEADS, DN_DK)) * (DN_DK ** -0.5)
    dk = _l2n(dk.reshape(B, T, DN_HEADS, DN_DK))
    beta = jax.nn.sigmoid(zs["d_b"])
    g = -jnp.exp(prm["dn_A_log"][l]) * jax.nn.softplus(zs["d_a"] + prm["dn_dt_bias"][l])
    lanes = lambda a: jnp.broadcast_to(a[:, 0, :, None], (B, DN_HEADS, DN_DK))
    vec = _rows8([dk[:, 0], dq[:, 0], dv.reshape(B, DN_HEADS, DN_DV), lanes(jnp.exp(g)), lanes(beta)])
    o8, dn_state = _dn_decode(vec, past["dn_S"], l)
    o_d = o8[:, :, 0].reshape(B, T, DN_V)
    o_d = _head_rms(o_d.reshape(B, T, DN_HEADS, DN_DV), prm["dn_out_norm"][l]) * jax.nn.silu(
        zs["d_z"].reshape(B, T, DN_HEADS, DN_DV))

    zf = zs["r_f"]
    logf = jnp.log(lb + (1.0 - lb) * jax.nn.sigmoid(zf))
    k_in = (1.0 - lb) * jax.nn.sigmoid(-zf)
    hq = jax.nn.silu(zs["r_q"]) * (HG_DK ** -0.5)
    hd = lambda a: a.reshape(B, HG_HEADS, HG_DK)
    vec = _rows8([hd(k_in), hd(hq), hd(zs["r_i"]), hd(logf)])
    o8, hg_state = _hg_decode(vec, past["hg_S"], l)
    o_h = o8[:, :, 0].reshape(B, T, HG_WV)
    o_h = _head_rms(o_h.reshape(B, T, HG_HEADS, HG_DV), prm["hg_out_norm"][l]) * jax.nn.sigmoid(
        zs["r_og"].reshape(B, T, HG_HEADS, HG_DV))
    return (o_a, o_d, o_h), (cmp_new, slc_new, win_new, conv_state, dn_state, hg_state)


def _trunk(x, prm, wts, caches, page_table):
    new = []
    for l in range(DEPTH):
        x, st = _layer(x, l, prm, wts, caches, page_table)
        new.append(st)
    return x, [jnp.stack([s[i] for s in new], axis=0) for i in range(6)]


def kernel(x_prompt, x_sample, cache_cmp_kv, cache_slc_kv, cache_win_kv, state_dn_conv, state_dn_S, state_hg_S,
           page_table, ffn1_norm, ffn1_w_gu, ffn1_w_down, mix_norm, w_in, nsa_q_norm, nsa_k_norm, nsa_cmp_pe,
           nsa_cmp_w1, nsa_cmp_w2, dn_conv_w, dn_A_log, dn_dt_bias, dn_out_norm, hg_lb_logits, hg_out_norm,
           w_branch, w_out, ffn2_norm, ffn2_w_gu, ffn2_w_down):
    prm = dict(ffn1_norm=ffn1_norm, mix_norm=mix_norm, nsa_q_norm=nsa_q_norm, nsa_k_norm=nsa_k_norm,
               dn_conv_w=dn_conv_w, dn_A_log=dn_A_log, dn_dt_bias=dn_dt_bias, dn_out_norm=dn_out_norm,
               hg_lb_logits=hg_lb_logits, hg_out_norm=hg_out_norm, ffn2_norm=ffn2_norm)
    bf = lambda w: w.astype(BF16)
    wts = dict(ffn1_w_gu=bf(ffn1_w_gu), ffn1_w_down=bf(ffn1_w_down), ffn2_w_gu=bf(ffn2_w_gu),
               ffn2_w_down=bf(ffn2_w_down), w_branch=bf(w_branch), w_out=bf(w_out),
               w_in=jnp.stack([_permute_w_in(w_in[l]) for l in range(DEPTH)]),
               cmp=[_cmp_weights(nsa_cmp_pe[l], nsa_cmp_w1[l], nsa_cmp_w2[l]) for l in range(DEPTH)])
    y_p, (p_cmp, p_slc, p_win, p_conv, p_dn, p_hg) = _trunk(x_prompt, prm, wts, None, None)
    caches = dict(cmp=_rows_minor(cache_cmp_kv), slc=_rows_minor(cache_slc_kv), win=_rows_minor(cache_win_kv),
                  conv=state_dn_conv, dn_S=state_dn_S, hg_S=state_hg_S)
    y_s, (s_cmp, s_slc, s_win, s_conv, s_dn, s_hg) = _trunk(x_sample, prm, wts, caches, page_table)
    return (y_p, y_s, p_cmp, s_cmp, p_slc, s_slc, p_win, s_win, p_conv, s_conv, p_dn, s_dn, p_hg, s_hg)
```

```python
import functools

import jax
import jax.numpy as jnp
import numpy as np
from jax import lax
from jax.experimental import pallas as pl
from jax.experimental.pallas import tpu as pltpu

F32 = jnp.float32
BF16 = jnp.bfloat16

D_MODEL = 1024
DEPTH = 2
PAST_LEN = 2048
PAGE_SIZE = 128
HEAD_DIM = 64
NSA_HEADS = 8
NSA_KV = 2
NSA_GROUP = NSA_HEADS // NSA_KV
CMP_LEN = 32
CMP_STRIDE = 16
CMP_HIDDEN = 128
SLC_BLOCK = 64
N_SEL = 16
WINDOW = 512
Q_BLOCK = 128
DN_HEADS = 4
DN_DK = 128
DN_DV = 128
DN_CHUNK = 64
CONV_W = 4
HG_HEADS = 4
HG_DK = 128
HG_DV = 128
HG_CHUNK = 16
D_FF = 2816
NORM_EPS = 1e-6
L2_EPS = 1e-6
NEG_BIG = -1e30
SEL_BIG = 1e9

NSA_W = NSA_HEADS * HEAD_DIM
KV_W = NSA_KV * HEAD_DIM
ROW_W = 2 * KV_W
DN_QK = DN_HEADS * DN_DK
DN_V = DN_HEADS * DN_DV
DN_QKV = 2 * DN_QK + DN_V
HG_WK = HG_HEADS * HG_DK
HG_WV = HG_HEADS * HG_DV
MIX_W = NSA_W + DN_V + HG_WV
IN_SPLITS = (NSA_W, 6 * KV_W, 3 * NSA_HEADS, DN_QKV, DN_HEADS, DN_HEADS, DN_V, HG_WK, HG_WV, HG_WK, HG_WV, 3 * D_MODEL)
IN_NAMES = ("a_q", "a_kv", "a_g", "d_qkv", "d_b", "d_a", "d_z", "r_f", "r_i", "r_q", "r_og", "m_g")
Z_ORDER = ("m_g", "d_qkv", "d_z", "r_f", "r_i", "r_q", "r_og", "a_q", "a_kv", "a_g", "d_b", "d_a")
Z_TN = 512
CHUNK_W = CMP_STRIDE * ROW_W
CMP_HID_W = 2 * NSA_KV * CMP_HIDDEN

V7X_VMEM_BYTES = 64 * 1024 * 1024
LANE = 128
ROW_TILE = 512
ROW_TILE_STREAMED = 1024


def _cparams(sem, vmem_mb):
    assert vmem_mb * 1024 * 1024 < V7X_VMEM_BYTES
    return pltpu.CompilerParams(dimension_semantics=sem, vmem_limit_bytes=vmem_mb * 1024 * 1024)


def _dot(a, b):
    return jnp.dot(a, b, preferred_element_type=F32)


def _dot_nt(a, b):
    return lax.dot_general(a, b, (((1,), (1,)), ((), ())), preferred_element_type=F32)


def _dot_tn(a, b):
    return lax.dot_general(a, b, (((0,), (0,)), ((), ())), preferred_element_type=F32)


def _split2(a):
    hi = a.astype(BF16)
    lo = (a - hi.astype(F32)).astype(BF16)
    return hi, lo


def _dot3(a, b):
    ah, al = _split2(a)
    bh, bl = _split2(b)
    return _dot(ah, bh) + (_dot(ah, bl) + _dot(al, bh))


def _rms(x, g):
    return x * lax.rsqrt(jnp.mean(x * x, axis=-1, keepdims=True) + NORM_EPS) * g


def _silu(x):
    return x * jax.nn.sigmoid(x)


def _iota(shape, dim):
    return lax.broadcasted_iota(jnp.int32, shape, dim)


def _alibi_slope(head_in_group, g):
    out = jnp.full(head_in_group.shape, 2.0 ** -(NSA_GROUP * g + NSA_GROUP), F32)
    for j in range(NSA_GROUP - 2, -1, -1):
        out = jnp.where(head_in_group == j, 2.0 ** -(NSA_GROUP * g + j + 1), out)
    return out


def _ffn_kernel(x_ref, g_ref, wg_ref, wu_ref, wd_ref, o_ref, xn_ref, acc_ref, *, nf):
    j = pl.program_id(1)

    @pl.when(j == 0)
    def _():
        xn_ref[...] = _rms(x_ref[...], g_ref[...]).astype(BF16)
        acc_ref[...] = jnp.zeros_like(acc_ref)

    xn = xn_ref[...]
    a = _silu(_dot(xn, wg_ref[...])) * _dot(xn, wu_ref[...])
    acc_ref[...] += _dot(a.astype(BF16), wd_ref[...])

    @pl.when(j == nf - 1)
    def _():
        o_ref[...] = x_ref[...] + 0.5 * acc_ref[...]


def _ffn(x, gain, w_gu, w_down, tm, tf=256):
    M, D = x.shape
    F = w_down.shape[0]
    nf = F // tf
    return pl.pallas_call(
        functools.partial(_ffn_kernel, nf=nf),
        grid=(M // tm, nf),
        in_specs=[
            pl.BlockSpec((tm, D), lambda i, j: (i, 0)),
            pl.BlockSpec((1, D), lambda i, j: (0, 0)),
            pl.BlockSpec((D, tf), lambda i, j: (0, j)),
            pl.BlockSpec((D, tf), lambda i, j: (0, j + nf)),
            pl.BlockSpec((tf, D), lambda i, j: (j, 0)),
        ],
        out_specs=pl.BlockSpec((tm, D), lambda i, j: (i, 0)),
        out_shape=jax.ShapeDtypeStruct((M, D), F32),
        scratch_shapes=[pltpu.VMEM((tm, D), BF16), pltpu.VMEM((tm, D), F32)],
        compiler_params=_cparams(("parallel", "arbitrary"), 40),
        name="ffn",
    )(x, gain.reshape(1, D), w_gu, w_gu, w_down)


def _inproj_kernel(x_ref, g_ref, w_ref, z_ref, xn_ref):
    @pl.when(pl.program_id(1) == 0)
    def _():
        xn_ref[...] = _rms(x_ref[...], g_ref[...]).astype(BF16)

    z_ref[...] = _dot(xn_ref[...], w_ref[...])


def _inproj(x, gain, w, tm):
    M, D = x.shape
    N = w.shape[1]
    return pl.pallas_call(
        _inproj_kernel,
        grid=(M // tm, N // Z_TN),
        in_specs=[
            pl.BlockSpec((tm, D), lambda i, j: (i, 0)),
            pl.BlockSpec((1, D), lambda i, j: (0, 0)),
            pl.BlockSpec((D, Z_TN), lambda i, j: (0, j)),
        ],
        out_specs=pl.BlockSpec((tm, Z_TN), lambda i, j: (i, j)),
        out_shape=jax.ShapeDtypeStruct((M, N), F32),
        scratch_shapes=[pltpu.VMEM((tm, D), BF16)],
        compiler_params=_cparams(("parallel", "arbitrary"), 40),
        name="inproj",
    )(x, gain.reshape(1, D), w)


def _z_layout():
    sizes = dict(zip(IN_NAMES, IN_SPLITS))
    src = dict(zip(IN_NAMES, np.cumsum((0,) + IN_SPLITS[:-1]).tolist()))
    offs, o = {}, 0
    for n in Z_ORDER:
        offs[n] = o
        o += sizes[n]
    total = -(-o // Z_TN) * Z_TN
    return sizes, src, offs, total


def _permute_w_in(w_in):
    sizes, src, _, total = _z_layout()
    cols = [w_in[:, src[n]:src[n] + sizes[n]] for n in Z_ORDER]
    w = jnp.concatenate(cols, axis=1)
    return jnp.pad(w, ((0, 0), (0, total - w.shape[1]))).astype(BF16)


def _merge_kernel(x_ref, oa_ref, od_ref, oh_ref, g0_ref, g1_ref, g2_ref, wb_ref, wo_ref, o_ref):
    m = jax.nn.sigmoid(g0_ref[...]) * _dot(oa_ref[...].astype(BF16), wb_ref[0:NSA_W, :])
    m += jax.nn.sigmoid(g1_ref[...]) * _dot(od_ref[...].astype(BF16), wb_ref[NSA_W:NSA_W + DN_V, :])
    m += jax.nn.sigmoid(g2_ref[...]) * _dot(oh_ref[...].astype(BF16), wb_ref[NSA_W + DN_V:MIX_W, :])
    o_ref[...] = x_ref[...] + _dot(m.astype(BF16), wo_ref[...])


def _merge(x, o_a, o_d, o_h, z, w_branch, w_out, tm):
    M, D = x.shape
    row = lambda w: pl.BlockSpec((tm, w), lambda i: (i, 0))
    return pl.pallas_call(
        _merge_kernel,
        grid=(M // tm,),
        in_specs=[
            row(D), row(NSA_W), row(DN_V), row(HG_WV),
            pl.BlockSpec((tm, D), lambda i: (i, 0)),
            pl.BlockSpec((tm, D), lambda i: (i, 1)),
            pl.BlockSpec((tm, D), lambda i: (i, 2)),
            pl.BlockSpec((MIX_W, D), lambda i: (0, 0)),
            pl.BlockSpec((D, D), lambda i: (0, 0)),
        ],
        out_specs=row(D),
        out_shape=jax.ShapeDtypeStruct((M, D), F32),
        compiler_params=_cparams(("parallel",), 40),
        name="merge",
    )(x, o_a, o_d, o_h, z, z, z, w_branch, w_out)


def _cmp_weights(pe, w1, w2):
    n_part = CMP_LEN // CMP_STRIDE
    eye = jnp.eye(NSA_KV, dtype=F32)
    eye2 = jnp.eye(2, dtype=F32)
    w1r = w1.reshape(2, n_part, CMP_STRIDE, HEAD_DIM, CMP_HIDDEN)
    w1e = jnp.einsum("kmrdh,kK,gG->mrkgdKGh", w1r, eye2, eye).reshape(n_part, CHUNK_W, CMP_HID_W)
    w2e = jnp.einsum("khd,kK,gG->kghKGd", w2, eye2, eye).reshape(CMP_HID_W, ROW_W)
    per = pe.reshape(2, n_part, CMP_STRIDE, HEAD_DIM).transpose(1, 2, 0, 3)
    pee = jnp.broadcast_to(per[:, :, :, None, :], (n_part, CMP_STRIDE, 2, NSA_KV, HEAD_DIM)).reshape(n_part, CHUNK_W)
    return pee, w1e.astype(BF16), w2e.astype(BF16)


def _compress(x, xnext_rows, pe_ref, w1_ref, w2_ref, kg_ref, n_seq=1):
    rows = x.shape[0]
    n = rows // n_seq
    p0 = _dot((x + pe_ref[0:1, :]).astype(BF16), w1_ref[0])
    p1 = _dot((x + pe_ref[1:2, :]).astype(BF16), w1_ref[1])
    p1s = pltpu.roll(p1, rows - 1, 0)
    if xnext_rows is not None:
        p1n = _dot((xnext_rows + pe_ref[1:2, :]).astype(BF16), w1_ref[1])
        for s in range(n_seq):
            p1s = jnp.where(_iota((rows, 1), 0) == (s + 1) * n - 1, p1n[s:s + 1, :], p1s)
    comp = _dot(_silu(p0 + p1s).astype(BF16), w2_ref[...])
    cks = []
    for g in range(NSA_KV):
        kc = comp[:, g * HEAD_DIM:(g + 1) * HEAD_DIM]
        cks.append(_rms(kc, kg_ref[...]))
    return cks, comp[:, KV_W:]


def _cmp_prompt_kernel(x_ref, pe_ref, w1_ref, w2_ref, kg_ref, ck_ref, cvt_ref):
    cks, cv = _compress(x_ref[0], None, pe_ref, w1_ref, w2_ref, kg_ref)
    for g in range(NSA_KV):
        ck_ref[0, g] = cks[g].astype(BF16)
    cvt = cv.T
    for g in range(NSA_KV):
        cvt_ref[0, g] = cvt[g * HEAD_DIM:(g + 1) * HEAD_DIM, :].astype(BF16)


def _cmp_prompt(cmp_rows, pee, w1e, w2e, kg0):
    B, T = cmp_rows.shape[:2]
    nc = T // CMP_STRIDE
    x = cmp_rows.reshape(B, nc, CHUNK_W)
    full = lambda s: pl.BlockSpec(s, lambda b: (0,) * len(s))
    return pl.pallas_call(
        _cmp_prompt_kernel,
        grid=(B,),
        in_specs=[pl.BlockSpec((1, nc, CHUNK_W), lambda b: (b, 0, 0)), full(pee.shape), full(w1e.shape),
                  full(w2e.shape), full((1, HEAD_DIM))],
        out_specs=[pl.BlockSpec((1, NSA_KV, nc, HEAD_DIM), lambda b: (b, 0, 0, 0)),
                   pl.BlockSpec((1, NSA_KV, HEAD_DIM, nc), lambda b: (b, 0, 0, 0))],
        out_shape=[jax.ShapeDtypeStruct((B, NSA_KV, nc, HEAD_DIM), BF16),
                   jax.ShapeDtypeStruct((B, NSA_KV, HEAD_DIM, nc), BF16)],
        compiler_params=_cparams(("parallel",), 48),
        name="nsa_compress",
    )(x, pee, w1e, w2e, kg0.reshape(1, HEAD_DIM))


def _softmax_first(s, vt):
    m = jnp.max(s, axis=0, keepdims=True)
    e = jnp.exp(s - m)
    return m, jnp.sum(e, axis=0, keepdims=True), _dot(vt, e.astype(BF16))


def _softmax_step(s, vt, carry):
    m, l, acc = carry
    m_new = jnp.maximum(m, jnp.max(s, axis=0, keepdims=True))
    alpha = jnp.exp(m - m_new)
    e = jnp.exp(s - m_new)
    l = alpha * l + jnp.sum(e, axis=0, keepdims=True)
    acc = alpha * acc + _dot(vt, e.astype(BF16))
    return m_new, l, acc


KC_W = 2 * LANE - HEAD_DIM
KC_POS = 64
KC_POS_RADIX = 128
KW = HEAD_DIM + KC_W


def _key_features(T):
    k = np.arange(T)
    f = np.zeros((T, KW), np.float32)
    f[k, HEAD_DIM + k // SLC_BLOCK] = 1.0
    f[:, HEAD_DIM + KC_POS] = k // KC_POS_RADIX
    f[:, HEAD_DIM + KC_POS + 1] = k % KC_POS_RADIX
    return f


def _nsa_prompt_kernel(qt_ref, ck_ref, cvt_ref, sk_ref, svt_ref, wk_ref, wvt_ref, gt_ref, ovl_ref, o_ref,
                       *, n_cmp, n_blk):
    qb = pl.program_id(1)
    b0 = qb * Q_BLOCK
    QW = NSA_GROUP * Q_BLOCK
    TK = Q_BLOCK
    lane = _iota((1, QW), 1)
    tq = lane % Q_BLOCK
    q_pos = b0 + tq
    head = lane // Q_BLOCK
    qp1 = b0 + _iota((1, Q_BLOCK), 1)
    bj = _iota((n_blk, 1), 0)
    krel = _iota((TK, 1), 0) - tq
    frow = _iota((KC_W, 1), 0)
    k_diag = pl.multiple_of(b0, TK)
    gt = gt_ref[0, 0]

    def gate(g, branch):
        return jnp.concatenate([gt[(NSA_GROUP * g + j) * 3 + branch:(NSA_GROUP * g + j) * 3 + branch + 1, :]
                                for j in range(NSA_GROUP)], axis=1)

    qc_slcs, qc_wins, o_cmps = [], [], []
    for g in range(NSA_KV):
        slope = _alibi_slope(head, g)
        qgt = jnp.concatenate([qt_ref[0, NSA_GROUP * g + j] for j in range(NSA_GROUP)], axis=1)
        pos_rows = jnp.where(frow == KC_POS, slope * float(KC_POS_RADIX), jnp.where(frow == KC_POS + 1, slope, 0.0))

        dist = q_pos - (_iota((n_cmp, 1), 0) * CMP_STRIDE + (CMP_LEN - 1))
        mask = dist >= 0
        s = jnp.where(mask, _dot(ck_ref[0, g], qgt) - slope * dist.astype(F32), NEG_BIG)
        m = jnp.max(s, axis=0, keepdims=True)
        e = jnp.where(mask, jnp.exp(s - m), 0.0)
        p = (e / jnp.maximum(jnp.sum(e, axis=0, keepdims=True), 1e-30)).astype(BF16)
        o_cmps.append(_dot(cvt_ref[0, g], p))

        imp = _dot(ovl_ref[...], p[:, 0:Q_BLOCK])
        for j in range(1, NSA_GROUP):
            imp += _dot(ovl_ref[...], p[:, j * Q_BLOCK:(j + 1) * Q_BLOCK])
        cur = qp1 // SLC_BLOCK
        forced = (bj == 0) | (bj == cur) | (bj == cur - 1)
        imp = jnp.where(forced, SEL_BIG, jnp.where(bj * SLC_BLOCK <= qp1, imp, -SEL_BIG))
        tiles = [imp[8 * v:8 * v + 8, :] for v in range(n_blk // 8)]
        ranks = [jnp.zeros((8, Q_BLOCK), F32) for _ in tiles]
        for i in range(n_blk):
            row = imp[i:i + 1, :]
            for v, tile in enumerate(tiles):
                if 8 * v > i:
                    beats = row >= tile
                elif 8 * v + 8 <= i:
                    beats = row > tile
                else:
                    beats = (row > tile) | ((row == tile) & (bj[8 * v:8 * v + 8] > i))
                ranks[v] = ranks[v] + jnp.where(beats, 1.0, 0.0)
        rank = jnp.concatenate(ranks, axis=0)
        selb = jnp.where(rank < float(min(N_SEL, n_blk)), 0.0, NEG_BIG)
        selb = jnp.concatenate([selb] * NSA_GROUP, axis=1)
        if n_blk < KC_W:
            selb = jnp.concatenate([selb, jnp.zeros((KC_W - n_blk, QW), F32)], axis=0)
        qc_slcs.append(jnp.concatenate([qgt, (selb + pos_rows).astype(BF16)], axis=0))
        qc_wins.append(jnp.concatenate([qgt, pos_rows.astype(BF16)], axis=0))

    def scores(k_ref, g, k0, n, qcs):
        return _dot(k_ref[0, g, pl.ds(k0, n), :], qcs[g])

    WS = WINDOW + Q_BLOCK
    ws = pl.multiple_of(jnp.maximum(b0 - WINDOW, 0), TK)
    off = b0 - ws
    drel = tq - _iota((WS, 1), 0)
    in_win = (drel >= -off) & (drel < WINDOW - off)
    o_wins = []
    for g in range(NSA_KV):
        s = jnp.where(in_win, scores(wk_ref, g, ws, WS, qc_wins), NEG_BIG)
        _, l, acc = _softmax_first(s, wvt_ref[0, g, :, pl.ds(ws, WS)])
        o_wins.append(acc / jnp.maximum(l, 1e-30))

    carry = []
    for g in range(NSA_KV):
        s = jnp.where(krel <= 0, scores(sk_ref, g, k_diag, TK, qc_slcs), NEG_BIG)
        carry.append(_softmax_first(s, svt_ref[0, g, :, pl.ds(k_diag, TK)]))

    def slc_body(n):
        def body(i, carry):
            k0 = pl.multiple_of(i * n, n)
            G = range(NSA_KV)
            ss = [scores(sk_ref, g, k0, n, qc_slcs) for g in G]
            ms = [jnp.maximum(carry[g][0], jnp.max(ss[g], axis=0, keepdims=True)) for g in G]
            es = [jnp.exp(ss[g] - ms[g]) for g in G]
            pvs = [_dot(svt_ref[0, g, :, pl.ds(k0, n)], es[g].astype(BF16)) for g in G]
            out = []
            for g in G:
                m, l, acc = carry[g]
                alpha = jnp.exp(m - ms[g])
                out.append((ms[g], alpha * l + jnp.sum(es[g], axis=0, keepdims=True), alpha * acc + pvs[g]))
            return tuple(out)
        return body

    carry = tuple(carry)
    done = 0
    for width in (4, 2, 1):
        trips = (qb - done) // width
        carry = lax.fori_loop(done // width, done // width + trips, slc_body(width * TK), carry)
        done = done + trips * width

    outs = []
    for g in range(NSA_KV):
        _, l, acc = carry[g]
        o_slc = acc / jnp.maximum(l, 1e-30)
        outs.append(gate(g, 0) * o_cmps[g] + gate(g, 1) * o_slc + gate(g, 2) * o_wins[g])

    for g in range(NSA_KV):
        for jp in range(NSA_GROUP // 2):
            pair = jnp.concatenate([outs[g][:, (2 * jp) * Q_BLOCK:(2 * jp + 1) * Q_BLOCK],
                                    outs[g][:, (2 * jp + 1) * Q_BLOCK:(2 * jp + 2) * Q_BLOCK]], axis=0)
            c0 = (NSA_GROUP * g + 2 * jp) * HEAD_DIM
            o_ref[0, :, c0:c0 + 2 * HEAD_DIM] = pair.T.astype(o_ref.dtype)


def _overlap(n_cmp, n_blk):
    ci = np.arange(n_cmp)[:, None] * CMP_STRIDE
    bj = np.arange(n_blk)[None, :]
    return ((ci < (bj + 1) * SLC_BLOCK) & (ci + CMP_LEN > bj * SLC_BLOCK)).astype(np.float32)


def _nsa_prompt(qt, ck, cvt, sk, svt, wk, wvt, gates):
    B, _, _, T = qt.shape
    nq = T // Q_BLOCK
    n_cmp = ck.shape[2]
    n_blk = T // SLC_BLOCK
    assert n_blk <= KC_POS and T <= KC_POS_RADIX * 256 and T >= WINDOW + Q_BLOCK
    ovl = jnp.asarray(_overlap(n_cmp, n_blk).T, BF16)
    per_b = lambda s: pl.BlockSpec((1,) + s, lambda b, i: (b,) + (0,) * len(s))
    return pl.pallas_call(
        functools.partial(_nsa_prompt_kernel, n_cmp=n_cmp, n_blk=n_blk),
        grid=(B, nq),
        in_specs=[
            pl.BlockSpec((1, NSA_HEADS, HEAD_DIM, Q_BLOCK), lambda b, i: (b, 0, 0, i)),
            per_b((NSA_KV, n_cmp, HEAD_DIM)), per_b((NSA_KV, HEAD_DIM, n_cmp)),
            per_b((NSA_KV, T, KW)), per_b((NSA_KV, HEAD_DIM, T)),
            per_b((NSA_KV, T, KW)), per_b((NSA_KV, HEAD_DIM, T)),
            pl.BlockSpec((1, 1, GATE_ROWS, Q_BLOCK), lambda b, i: (b, i, 0, 0)),
            pl.BlockSpec((n_blk, n_cmp), lambda b, i: (0, 0)),
        ],
        out_specs=pl.BlockSpec((1, Q_BLOCK, NSA_W), lambda b, i: (b, i, 0)),
        out_shape=jax.ShapeDtypeStruct((B, T, NSA_W), BF16),
        compiler_params=_cparams(("parallel", "arbitrary"), 48),
        name="nsa_prompt",
    )(qt, ck, cvt, sk, svt, wk, wvt, gates, ovl)


GATE_ROWS = 32


def _pair_rms(x, gain2, ones_bd):
    hi, lo = _split2(x * x)
    ms = (_dot(hi, ones_bd) + _dot(lo, ones_bd)) * (1.0 / HEAD_DIM)
    return x * lax.rsqrt(ms + NORM_EPS) * gain2


def _nsa_prep_kernel(zq_ref, zc0_ref, zc1_ref, zs0_ref, zs1_ref, zw0_ref, zw1_ref, zg_ref, qg_ref, kg_ref, kf_ref,
                     qt_ref, cmp_ref, slc_ref, win_ref, sk_ref, svt_ref, wk_ref, wvt_ref, gt_ref):
    lane = _iota((1, LANE), 1)
    ones_bd = jnp.where(_iota((LANE, LANE), 0) // HEAD_DIM == _iota((LANE, LANE), 1) // HEAD_DIM, 1.0, 0.0).astype(BF16)
    for hp in range(NSA_HEADS // 2):
        qn = _pair_rms(zq_ref[0, :, hp * LANE:(hp + 1) * LANE], qg_ref[...], ones_bd) * (HEAD_DIM ** -0.5)
        qnt = qn.T
        qt_ref[0, 2 * hp] = qnt[0:HEAD_DIM].astype(BF16)
        qt_ref[0, 2 * hp + 1] = qnt[HEAD_DIM:].astype(BF16)
    cmp_ref[0, :, 0:KV_W] = zc0_ref[0]
    cmp_ref[0, :, KV_W:] = zc1_ref[0]
    feat = kf_ref[...]
    for i, (zk_ref, zv_ref, rows_ref, ka_ref, vt_ref) in enumerate(
            ((zs0_ref, zs1_ref, slc_ref, sk_ref, svt_ref), (zw0_ref, zw1_ref, win_ref, wk_ref, wvt_ref))):
        k = _pair_rms(zk_ref[0], kg_ref[i:i + 1, :], ones_bd)
        v = zv_ref[0]
        rows_ref[0, :, 0:KV_W] = k
        rows_ref[0, :, KV_W:] = v
        for g in range(NSA_KV):
            kg = k if g == 0 else pltpu.roll(k, HEAD_DIM, 1)
            ka_ref[0, g, :, 0:LANE] = jnp.where(lane < HEAD_DIM, kg.astype(BF16), feat[:, 0:LANE])
            ka_ref[0, g, :, LANE:] = feat[:, LANE:]
        vt = v.T
        for g in range(NSA_KV):
            vt_ref[0, g] = vt[g * HEAD_DIM:(g + 1) * HEAD_DIM].astype(BF16)
    gt_ref[0, 0] = jax.nn.sigmoid(zg_ref[0]).T[0:GATE_ROWS]


def _nsa_prep(z, q_gain, k_gains, offs):
    B, T, _ = z.shape
    nq = T // Q_BLOCK
    assert offs["a_q"] % NSA_W == 0 and offs["a_kv"] % LANE == 0 and offs["a_g"] % LANE == 0
    kv0 = offs["a_kv"] // LANE
    zcol = lambda c: pl.BlockSpec((1, Q_BLOCK, LANE), lambda b, i: (b, i, c))
    full = lambda s: pl.BlockSpec(s, lambda b, i: (0,) * len(s))
    rows = pl.BlockSpec((1, Q_BLOCK, ROW_W), lambda b, i: (b, i, 0))
    keys = pl.BlockSpec((1, NSA_KV, Q_BLOCK, KW), lambda b, i: (b, 0, i, 0))
    vals = pl.BlockSpec((1, NSA_KV, HEAD_DIM, Q_BLOCK), lambda b, i: (b, 0, 0, i))
    two = lambda gain: jnp.concatenate([gain, gain], axis=-1)
    return pl.pallas_call(
        _nsa_prep_kernel,
        grid=(B, nq),
        in_specs=[pl.BlockSpec((1, Q_BLOCK, NSA_W), lambda b, i, c=offs["a_q"] // NSA_W: (b, i, c))]
        + [zcol(kv0 + j) for j in range(6)] + [zcol(offs["a_g"] // LANE)]
        + [full((1, LANE)), full((2, LANE)), pl.BlockSpec((Q_BLOCK, KW), lambda b, i: (i, 0))],
        out_specs=[pl.BlockSpec((1, NSA_HEADS, HEAD_DIM, Q_BLOCK), lambda b, i: (b, 0, 0, i)), rows, rows, rows,
                   keys, vals, keys, vals, pl.BlockSpec((1, 1, GATE_ROWS, Q_BLOCK), lambda b, i: (b, i, 0, 0))],
        out_shape=[jax.ShapeDtypeStruct((B, NSA_HEADS, HEAD_DIM, T), BF16)]
        + [jax.ShapeDtypeStruct((B, T, ROW_W), F32)] * 3
        + [jax.ShapeDtypeStruct((B, NSA_KV, T, KW), BF16), jax.ShapeDtypeStruct((B, NSA_KV, HEAD_DIM, T), BF16)] * 2
        + [jax.ShapeDtypeStruct((B, nq, GATE_ROWS, Q_BLOCK), F32)],
        compiler_params=_cparams(("parallel", "parallel"), 32),
        name="nsa_prep",
    )(z, z, z, z, z, z, z, z, two(q_gain).reshape(1, LANE), two(k_gains[1:3]), jnp.asarray(_key_features(T), BF16))


NSA_DECODE_SEQS = 2

def _nsa_decode_kernel(pt_ref, q_ref, new_ref, gt_ref, pe_ref, w1_ref, w2_ref, kg_ref, ovl_ref, exp_ref, *rest,
                       n_pages, n_seq):
    del pt_ref
    all_cmp_pages = rest[:n_seq * n_pages]
    all_slc_pages = rest[n_seq * n_pages:2 * n_seq * n_pages]
    win_ref, o_ref, rows_ref, x_ref = rest[2 * n_seq * n_pages:2 * n_seq * n_pages + 4]
    n_past = n_pages * PAGE_SIZE
    n_cmp = n_past // CMP_STRIDE
    q_pos = n_past
    halves = ROW_W // LANE
    for p, r in enumerate(all_cmp_pages):
        for c in range(halves):
            rows_ref[c, p * PAGE_SIZE:(p + 1) * PAGE_SIZE, :] = r[0, 0, c * LANE:(c + 1) * LANE, :].T
    for r in range(CMP_STRIDE):
        for c in range(halves):
            x_ref[:, r * ROW_W + c * LANE:r * ROW_W + (c + 1) * LANE] = rows_ref.at[c][
                pl.ds(r, n_seq * n_cmp, stride=CMP_STRIDE), :]
    jrow = _iota((8, 1), 0)
    xnew = jnp.zeros((8, ROW_W), F32)
    for sq in range(n_seq):
        xnew = jnp.where(jrow == sq, new_ref[sq][0:1, 0:ROW_W], xnew)
    xnew = jnp.concatenate([xnew, jnp.zeros((8, CHUNK_W - ROW_W), F32)], axis=1)
    all_cks, all_cv = _compress(x_ref[...], xnew, pe_ref, w1_ref, w2_ref, kg_ref, n_seq)

    lanes = _iota((1, LANE), 1)
    n_blk = (n_past + 1 + SLC_BLOCK - 1) // SLC_BLOCK
    cur = q_pos // SLC_BLOCK
    ii = _iota((LANE, LANE), 0)
    jj = _iota((LANE, LANE), 1)
    n_win = win_ref.shape[3]
    kpos = _iota((1, n_past), 1)
    wpos = (q_pos - n_win) + _iota((1, n_win), 1)
    dist_w = q_pos - wpos
    mask_w = (dist_w < WINDOW) & (wpos >= 0)
    dist_c = q_pos - (lanes * CMP_STRIDE + (CMP_LEN - 1))
    mask_c = dist_c >= 0
    forced = (lanes == 0) | (lanes == cur) | (lanes == cur - 1)

    ch = []
    for sq in range(n_seq):
        for g in range(NSA_KV):
            lo = g * HEAD_DIM
            rows = slice(sq * n_cmp, (sq + 1) * n_cmp)
            ch.append(dict(sq=sq, g=g, lo=lo, hi=lo + HEAD_DIM, qg=q_ref[sq, g], slope=_alibi_slope(jrow, g),
                           ck=all_cks[g][rows].astype(BF16), cv=all_cv[rows, lo:lo + HEAD_DIM].astype(BF16),
                           new=new_ref[sq], pages=all_slc_pages[sq * n_pages:(sq + 1) * n_pages]))

    def new_key(c, row):
        return c["new"][row:row + 1, c["lo"]:c["hi"]], c["new"][row:row + 1, KV_W + c["lo"]:KV_W + c["hi"]]

    for c in ch:
        c["s"] = jnp.where(mask_c, _dot_nt(c["qg"], c["ck"]) - c["slope"] * dist_c.astype(F32), NEG_BIG)
    for c in ch:
        e = jnp.where(mask_c, jnp.exp(c["s"] - jnp.max(c["s"], axis=1, keepdims=True)), 0.0)
        c["p"] = (e / jnp.maximum(jnp.sum(e, axis=1, keepdims=True), 1e-30)).astype(BF16)
    for c in ch:
        c["o_cmp"] = _dot(c["p"], c["cv"])
        c["imp"] = _dot(c["p"], ovl_ref[...])
    for c in ch:
        imp = jnp.sum(jnp.where(jrow < NSA_GROUP, c["imp"], 0.0), axis=0, keepdims=True)
        imp = jnp.where(forced, SEL_BIG, jnp.where(lanes * SLC_BLOCK <= q_pos, imp, -SEL_BIG))
        imp = jnp.where(lanes < n_blk, imp, -3e38)
        impr = jnp.broadcast_to(imp, (LANE, LANE))
        impc = jnp.sum(jnp.where(ii == jj, impr, 0.0), axis=1, keepdims=True)
        beats = jnp.where(impc > impr, 1.0, jnp.where((impc == impr) & (ii < jj), 1.0, 0.0))
        rank = jnp.sum(beats, axis=0, keepdims=True)
        c["sel"] = jnp.where(rank < float(min(N_SEL, n_blk)), 1.0, 0.0)
    for c in ch:
        c["selk"] = _dot(jnp.broadcast_to(c["sel"], (8, LANE)).astype(BF16), exp_ref[...])

    def attend(scores, vts, masks, dists, new_rows, new_masks):
        outs = []
        ss, s_news = [], []
        for c, s, mask, dist, row, mask_new in zip(ch, scores, masks, dists, new_rows, new_masks):
            ss.append(jnp.where(mask, s - c["slope"] * dist.astype(F32), NEG_BIG))
            knew, _ = new_key(c, row)
            s_new = jnp.sum(c["qg"].astype(F32) * knew.astype(BF16).astype(F32), axis=1, keepdims=True)
            s_news.append(s_new if mask_new is None else jnp.where(mask_new, s_new, NEG_BIG))
        ms = [jnp.maximum(jnp.max(s, axis=1, keepdims=True), s_new) for s, s_new in zip(ss, s_news)]
        es = [jnp.where(mask, jnp.exp(s - m), 0.0) for s, m, mask in zip(ss, ms, masks)]
        pvs = [_dot_nt(e.astype(BF16), vt) for e, vt in zip(es, vts)]
        for c, e, m, s_new, pv, row, mask_new in zip(ch, es, ms, s_news, pvs, new_rows, new_masks):
            e_new = jnp.exp(s_new - m)
            if mask_new is not None:
                e_new = jnp.where(mask_new, e_new, 0.0)
            _, vnew = new_key(c, row)
            l = jnp.sum(e, axis=1, keepdims=True) + e_new
            o = pv + e_new.astype(BF16).astype(F32) * vnew.astype(BF16).astype(F32)
            outs.append(o / jnp.maximum(l, 1e-30))
        return outs

    def paged(c, off):
        return jnp.concatenate([r[0, 0, off + c["lo"]:off + c["hi"], :] for r in c["pages"]], axis=1).astype(BF16)

    n_ch = len(ch)
    o_slc = attend([_dot(c["qg"], paged(c, 0)) for c in ch], [paged(c, KV_W) for c in ch],
                   [c["selk"] > 0.5 for c in ch], [q_pos - kpos] * n_ch, [1] * n_ch,
                   [c["sel"][:, cur:cur + 1] > 0.5 for c in ch])
    o_win = attend([_dot(c["qg"], win_ref[0, c["sq"], c["lo"]:c["hi"], :].astype(BF16)) for c in ch],
                   [win_ref[0, c["sq"], KV_W + c["lo"]:KV_W + c["hi"], :].astype(BF16) for c in ch],
                   [mask_w] * n_ch, [dist_w] * n_ch, [2] * n_ch, [None] * n_ch)
    for c, o_s, o_w in zip(ch, o_slc, o_win):
        gt = gt_ref[c["sq"], c["g"]]
        o_ref[c["sq"], c["g"]] = gt[:, 0:1] * c["o_cmp"] + gt[:, 1:2] * o_s + gt[:, 2:3] * o_w


def _rows_minor(cache):
    lead = cache.ndim - 4
    perm = tuple(range(lead)) + (lead + 1, lead + 2, lead + 3, lead)
    return cache.transpose(perm).reshape(cache.shape[:lead] + (ROW_W, cache.shape[lead]))


def _nsa_decode(page_table, l, q, new_rows, gates, cmp_t, slc_t, win_t, pee, w1e, w2e, kg0):
    B, n_pages = page_table.shape
    n_past = n_pages * PAGE_SIZE
    n_cmp = n_past // CMP_STRIDE
    n_win = win_t.shape[3]
    assert n_cmp == LANE and n_win <= n_past
    n_blk = (n_past + 1 + SLC_BLOCK - 1) // SLC_BLOCK
    ovl = np.zeros((n_cmp, LANE), np.float32)
    ovl[:, :n_blk] = _overlap(n_cmp, n_blk)
    expand = (np.arange(n_past)[None, :] // SLC_BLOCK == np.arange(LANE)[:, None]).astype(np.float32)
    S = NSA_DECODE_SEQS
    assert B % S == 0
    full = lambda s: pl.BlockSpec(s, lambda i, pt: (0,) * len(s))
    per_b = lambda s: pl.BlockSpec((S,) + s, lambda i, pt: (i,) + (0,) * len(s))
    page = lambda sq, p: pl.BlockSpec((1, 1, ROW_W, PAGE_SIZE), lambda i, pt: (l, pt[S * i + sq, p], 0, 0))
    pages = [page(sq, p) for sq in range(S) for p in range(n_pages)]

    in_specs = [per_b((NSA_KV, 8, HEAD_DIM)), per_b((8, ROW_W)), per_b((NSA_KV, 8, LANE)), full(pee.shape),
                full(w1e.shape), full(w2e.shape), full((1, HEAD_DIM)), full(ovl.shape), full(expand.shape)]
    in_specs += pages * 2
    in_specs += [pl.BlockSpec((1, S, ROW_W, n_win), lambda i, pt: (l, i, 0, 0))]
    return pl.pallas_call(
        functools.partial(_nsa_decode_kernel, n_pages=n_pages, n_seq=S),
        grid_spec=pltpu.PrefetchScalarGridSpec(
            num_scalar_prefetch=1, grid=(B // S,), in_specs=in_specs,
            out_specs=pl.BlockSpec((S, NSA_KV, 8, HEAD_DIM), lambda i, pt: (i, 0, 0, 0)),
            scratch_shapes=[pltpu.VMEM((ROW_W // LANE, S * n_past, LANE), F32),
                            pltpu.VMEM((S * n_cmp, CHUNK_W), F32)]),
        out_shape=jax.ShapeDtypeStruct((B, NSA_KV, 8, HEAD_DIM), F32),
        compiler_params=_cparams(("arbitrary",), 56),
        name="nsa_decode",
    )(page_table, q, new_rows, gates, pee, w1e, w2e, kg0.reshape(1, HEAD_DIM), jnp.asarray(ovl, BF16),
      jnp.asarray(expand, BF16), *([cmp_t] * (S * n_pages)), *([slc_t] * (S * n_pages)), win_t)


DN_TB = 2 * DN_CHUNK
HALO = 8


def _softplus(x):
    return jnp.maximum(x, 0.0) + jnp.log(1.0 + jnp.exp(-jnp.abs(x)))


def _dn_prompt_kernel(zq_ref, zk_ref, zv_ref, zz_ref, zs_ref, cw_ref, a_ref, dtb_ref, gain_ref, o_ref, s_out_ref,
                      s_ref, buf_ref, *, n_step, col_b, col_a):
    t = pl.program_id(1)
    TB, C = DN_TB, DN_CHUNK

    @pl.when(t == 0)
    def _():
        s_ref[...] = jnp.zeros_like(s_ref)
        buf_ref[0:HALO, :] = jnp.zeros((HALO, DN_QKV), F32)

    @pl.when(t > 0)
    def _():
        buf_ref[0:HALO, :] = buf_ref[TB:TB + HALO, :]

    buf_ref[HALO:HALO + TB, 0:DN_QK] = zq_ref[0]
    buf_ref[HALO:HALO + TB, DN_QK:2 * DN_QK] = zk_ref[0]
    buf_ref[HALO:HALO + TB, 2 * DN_QK:] = zv_ref[0]
    cw = cw_ref[...]
    first = HALO - (CONV_W - 1)
    y = buf_ref[pl.ds(first, TB), :] * cw[0:1, :]
    for j in range(1, CONV_W):
        y += buf_ref[pl.ds(first + j, TB), :] * cw[j:j + 1, :]
    act = _silu(y)

    zs = zs_ref[0]
    beta_all = jax.nn.sigmoid(zs)
    g_all = -a_ref[...] * _softplus(zs + dtb_ref[...])
    zz = zz_ref[0]
    gain = gain_ref[...]

    ii = _iota((C, C), 0)
    jj = _iota((C, C), 1)
    incl = ii >= jj
    eye = (ii == jj).astype(F32)
    n_sub = TB // C
    ch = []
    for sc in range(n_sub):
        rows = slice(sc * C, (sc + 1) * C)
        for h in range(DN_HEADS):
            cs = slice(h * DN_DK, (h + 1) * DN_DK)
            q = _l2n(act[rows, cs]) * (DN_DK ** -0.5)
            k = _l2n(act[rows, DN_QK + h * DN_DK:DN_QK + (h + 1) * DN_DK])
            v = act[rows, 2 * DN_QK + h * DN_DV:2 * DN_QK + (h + 1) * DN_DV]
            gcol = g_all[rows, col_a + h:col_a + h + 1]
            bcol = beta_all[rows, col_b + h:col_b + h + 1]
            grow = jnp.sum(jnp.where(ii <= jj, gcol, 0.0), axis=0, keepdims=True)
            gcum = jnp.sum(jnp.where(ii == jj, grow, 0.0), axis=1, keepdims=True)
            decay = jnp.where(incl, jnp.exp(jnp.where(incl, gcum - grow, 0.0)), 0.0)
            kb = k.astype(BF16)
            ch.append(dict(q=q, k=k, v=v, kb=kb, bcol=bcol, gcum=gcum, decay=decay, sc=sc, h=h, rows=rows, cs=cs))
    for c in ch:
        c["npow"] = -jnp.where(ii > jj, c["bcol"] * _dot_nt(c["kb"], c["kb"]) * c["decay"], 0.0)
        c["tinv"] = eye + c["npow"]
    for _ in range(int(np.log2(C)) - 1):
        for c in ch:
            nb = c["npow"].astype(BF16)
            c["npow"] = _dot(nb, nb)
        for c in ch:
            c["tinv"] = c["tinv"] + _dot(c["tinv"].astype(BF16), c["npow"].astype(BF16))
    for c in ch:
        tb = c["tinv"].astype(BF16)
        c["eg"] = jnp.exp(c["gcum"])
        c["u"] = _dot(tb, (c["v"] * c["bcol"]).astype(BF16))
        c["w"] = _dot(tb, (c["k"] * (c["bcol"] * c["eg"])).astype(BF16)).astype(BF16)
        c["attn"] = (_dot_nt(c["q"].astype(BF16), c["kb"]) * c["decay"]).astype(BF16)
    for sc in range(n_sub):
        cur = [c for c in ch if c["sc"] == sc]
        Ss = [s_ref[c["h"]] for c in cur]
        Sbs = [S.astype(BF16) for S in Ss]
        vns = [c["u"] - _dot(c["w"], Sb) for c, Sb in zip(cur, Sbs)]
        for c, S, Sb, v_new in zip(cur, Ss, Sbs, vns):
            vnb = v_new.astype(BF16)
            o = _dot((c["q"] * c["eg"]).astype(BF16), Sb) + _dot(c["attn"], vnb)
            g_last = c["gcum"][C - 1:C, :]
            s_ref[c["h"]] = S * jnp.exp(g_last) + _dot_tn((c["k"] * jnp.exp(g_last - c["gcum"])).astype(BF16), vnb)
            o_ref[0, c["rows"], c["cs"]] = (_rms(o, gain) * _silu(zz[c["rows"], c["cs"]])).astype(o_ref.dtype)

    @pl.when(t == n_step - 1)
    def _():
        s_out_ref[0] = s_ref[...]


def _dn_prompt(z, conv_w, a_log, dt_bias, out_gain, offs):
    B, T, _ = z.shape
    n_step = T // DN_TB
    wide = lambda name, k=0: pl.BlockSpec((1, DN_TB, DN_QK), lambda b, t, c=offs[name] // DN_QK + k: (b, t, c))
    assert offs["d_qkv"] % DN_QK == 0 and offs["d_z"] % DN_V == 0 and offs["a_g"] % LANE == 0
    col_b, col_a = offs["d_b"] - offs["a_g"], offs["d_a"] - offs["a_g"]
    lanes = lambda vals, col: jnp.zeros((1, LANE), F32).at[0, col:col + DN_HEADS].set(vals)
    full = lambda s: pl.BlockSpec(s, lambda b, t: (0,) * len(s))
    return pl.pallas_call(
        functools.partial(_dn_prompt_kernel, n_step=n_step, col_b=col_b, col_a=col_a),
        grid=(B, n_step),
        in_specs=[wide("d_qkv", 0), wide("d_qkv", 1), wide("d_qkv", 2), wide("d_z"),
                  pl.BlockSpec((1, DN_TB, LANE), lambda b, t, c=offs["a_g"] // LANE: (b, t, c)),
                  full((CONV_W, DN_QKV)), full((1, LANE)), full((1, LANE)), full((1, DN_DV))],
        out_specs=[pl.BlockSpec((1, DN_TB, DN_V), lambda b, t: (b, t, 0)),
                   pl.BlockSpec((1, DN_HEADS, DN_DK, DN_DV), lambda b, t: (b, 0, 0, 0))],
        out_shape=[jax.ShapeDtypeStruct((B, T, DN_V), BF16), jax.ShapeDtypeStruct((B, DN_HEADS, DN_DK, DN_DV), F32)],
        scratch_shapes=[pltpu.VMEM((DN_HEADS, DN_DK, DN_DV), F32), pltpu.VMEM((HALO + DN_TB, DN_QKV), F32)],
        compiler_params=_cparams(("parallel", "arbitrary"), 32),
        name="dn_prompt",
    )(z, z, z, z, z, conv_w, lanes(jnp.exp(a_log), col_a), lanes(dt_bias, col_a), out_gain.reshape(1, DN_DV))


def _row0(x):
    return jnp.where(_iota(x.shape, 0) == 0, x, 0.0)


def _dn_decode_kernel(vec_ref, s_ref, o_ref, s_out_ref, *, bb):
    def body(b, _):
        for h in range(DN_HEADS):
            x = vec_ref[b, h]
            S = s_ref[b, h]
            xs = _dot(x.astype(BF16), S.astype(BF16))
            k, q, v, eg, beta = x[0:1], x[1:2], x[2:3], x[3:4], x[4:5]
            v_new = beta * (v - eg * xs[0:1])
            kb = k.astype(BF16).astype(F32)
            qk = jnp.sum(q.astype(BF16).astype(F32) * kb, axis=1, keepdims=True)
            o = eg * xs[1:2] + qk.astype(BF16).astype(F32) * v_new.astype(BF16).astype(F32)
            o_ref[b, h] = jnp.broadcast_to(o, (8, DN_DV))
            s_out_ref[b, h] = S * eg[:, 0:1] + _dot_tn(_row0(x).astype(BF16),
                                                      _row0(jnp.broadcast_to(v_new, (8, DN_DV))).astype(BF16))
        return 0

    lax.fori_loop(0, bb, body, 0)


def _state_step(kernel_fn, name, vec, S_all, l, bb=8):
    B, H = vec.shape[:2]
    spec_v = pl.BlockSpec((bb, H, 8, vec.shape[3]), lambda i: (i, 0, 0, 0))
    spec_s = pl.BlockSpec((bb, H) + S_all.shape[3:], lambda i: (i, 0, 0, 0))
    spec_sl = pl.BlockSpec((None, bb, H) + S_all.shape[3:], lambda i: (l, i, 0, 0, 0))
    return pl.pallas_call(
        functools.partial(kernel_fn, bb=bb),
        grid=(B // bb,),
        in_specs=[spec_v, spec_sl],
        out_specs=[spec_v, spec_s],
        out_shape=[jax.ShapeDtypeStruct(vec.shape, F32), jax.ShapeDtypeStruct(S_all.shape[1:], F32)],
        compiler_params=_cparams(("parallel",), 32),
        name=name,
    )(vec, S_all)


def _dn_decode(vec, S_all, l):
    return _state_step(_dn_decode_kernel, "dn_decode", vec, S_all, l)


HG_TB = 128
HG_LEVELS = tuple(HG_TB >> (i + 1) for i in range(int(np.log2(HG_TB))))


def _hg_tables():
    r = np.arange(HG_TB)
    tril = (r[:, None] >= r[None, :]).astype(np.float32)
    mats = [tril] + [tril[(r // (2 * w)) * 2 * w + w - 1] for w in HG_LEVELS]
    level = np.full((HG_TB, HG_TB), -1, np.int32)
    for i, w in enumerate(HG_LEVELS):
        same_block = r[:, None] // (2 * w) == r[None, :] // (2 * w)
        split = (r[:, None] % (2 * w) >= w) & (r[None, :] % (2 * w) < w)
        level[same_block & split] = i
    return np.concatenate(mats, axis=0), level


def _hg_prompt_kernel(zf_ref, zi_ref, zq_ref, zo_ref, lb_ref, gain_ref, sel_ref, lvl_ref, o_ref, s_out_ref, st_ref,
                      *, n_step):
    t = pl.program_id(1)
    TB = HG_TB

    @pl.when(t == 0)
    def _():
        st_ref[...] = jnp.zeros_like(st_ref)

    zf = zf_ref[0]
    lb = lb_ref[...]
    lf = jnp.log(lb + (1.0 - lb) * jax.nn.sigmoid(zf))
    k = (1.0 - lb) * jax.nn.sigmoid(-zf)
    q = _silu(zq_ref[0]) * (HG_DK ** -0.5)
    vb = zi_ref[0].astype(BF16)
    gain = gain_ref[...]
    hi = lf.astype(BF16)
    r1 = lf - hi.astype(F32)
    mid = r1.astype(BF16)
    lo = (r1 - mid.astype(F32)).astype(BF16)
    sel = sel_ref[...]
    gg = _dot(sel, hi) + (_dot(sel, mid) + _dot(sel, lo))
    G = gg[0:TB]
    row = _iota((TB, 1), 0)
    qts, kts = [], []
    for i, w in enumerate(HG_LEVELS):
        d = G - gg[(i + 1) * TB:(i + 2) * TB]
        right = (row % (2 * w)) >= w
        qts.append(jnp.where(right, q * jnp.exp(jnp.minimum(d, 0.0)), 0.0).astype(BF16))
        kts.append(jnp.where(right, 0.0, k * jnp.exp(jnp.minimum(-d, 0.0))).astype(BF16))
    lvl = lvl_ref[...]
    eye = _iota((TB, TB), 0) == _iota((TB, TB), 1)
    qe = (q * jnp.exp(G)).astype(BF16)
    g_last = G[TB - 1:TB, :]
    kd = (k * jnp.exp(g_last - G)).astype(BF16)
    eg_last = jnp.exp(g_last)
    qk = q * k
    for h in range(HG_HEADS):
        cs = slice(h * HG_DK, (h + 1) * HG_DK)
        a = jnp.where(eye, jnp.sum(qk[:, cs], axis=1, keepdims=True), 0.0)
        for i in range(len(HG_LEVELS)):
            a = a + jnp.where(lvl == i, _dot_nt(qts[i][:, cs], kts[i][:, cs]), 0.0)
        st = st_ref[h]
        o = _dot_nt(qe[:, cs], st.astype(BF16)) + _dot(a.astype(BF16), vb[:, cs])
        o_ref[0, :, cs] = (_rms(o, gain) * jax.nn.sigmoid(zo_ref[0, :, cs])).astype(o_ref.dtype)
        st_ref[h] = st * eg_last[:, cs] + _dot_tn(vb[:, cs], kd[:, cs])

    @pl.when(t == n_step - 1)
    def _():
        for h in range(HG_HEADS):
            s_out_ref[0, h] = st_ref[h].T


def _hg_prompt(z, lb, out_gain, offs):
    B, T, _ = z.shape
    tb = HG_TB
    n_step = T // tb
    sel, level = _hg_tables()
    assert all(offs[n] % HG_WK == 0 for n in ("r_f", "r_i", "r_q", "r_og"))
    col = lambda name: pl.BlockSpec((1, tb, HG_WK), lambda b, t, c=offs[name] // HG_WK: (b, t, c))
    full = lambda s: pl.BlockSpec(s, lambda b, t: (0,) * len(s))
    return pl.pallas_call(
        functools.partial(_hg_prompt_kernel, n_step=n_step),
        grid=(B, n_step),
        in_specs=[col("r_f"), col("r_i"), col("r_q"), col("r_og"), full((1, HG_WK)), full((1, HG_DV)),
                  full(sel.shape), full(level.shape)],
        out_specs=[pl.BlockSpec((1, tb, HG_WV), lambda b, t: (b, t, 0)),
                   pl.BlockSpec((1, HG_HEADS, HG_DK, HG_DV), lambda b, t: (b, 0, 0, 0))],
        out_shape=[jax.ShapeDtypeStruct((B, T, HG_WV), BF16), jax.ShapeDtypeStruct((B, HG_HEADS, HG_DK, HG_DV), F32)],
        scratch_shapes=[pltpu.VMEM((HG_HEADS, HG_DV, HG_DK), F32)],
        compiler_params=_cparams(("parallel", "arbitrary"), 32),
        name="hg_prompt",
    )(z, z, z, z, lb.reshape(1, HG_WK), out_gain.reshape(1, HG_DV), jnp.asarray(sel, BF16), jnp.asarray(level))


def _hg_decode_kernel(vec_ref, s_ref, o_ref, s_out_ref, *, bb):
    ii = _iota((HG_DK, HG_DK), 0)
    jj = _iota((HG_DK, HG_DK), 1)

    def body(b, _):
        for h in range(HG_HEADS):
            x = vec_ref[b, h]
            S = s_ref[b, h]
            k, q, v, lf = x[0:1], x[1:2], x[2:3], x[3:4]
            f = jnp.exp(lf)
            qs = _dot(jnp.broadcast_to(q * f, (8, HG_DK)).astype(BF16), S.astype(BF16))
            a = jnp.sum(q * k, axis=1, keepdims=True)
            o = qs[0:1] + a.astype(BF16).astype(F32) * v.astype(BF16).astype(F32)
            o_ref[b, h] = jnp.broadcast_to(o, (8, HG_DV))
            fcol = jnp.sum(jnp.where(ii == jj, f, 0.0), axis=1, keepdims=True)
            s_out_ref[b, h] = S * fcol + _dot_tn(_row0(x).astype(BF16),
                                                 _row0(jnp.broadcast_to(v, (8, HG_DV))).astype(BF16))
        return 0

    lax.fori_loop(0, bb, body, 0)


def _hg_decode(vec, S_all, l):
    return _state_step(_hg_decode_kernel, "hg_decode", vec, S_all, l)


def _head_rms(x, g):
    return x * lax.rsqrt(jnp.mean(x * x, axis=-1, keepdims=True) + NORM_EPS) * g


def _l2n(x):
    return x * lax.rsqrt(jnp.sum(x * x, axis=-1, keepdims=True) + L2_EPS)


def _rows8(rows):
    x = jnp.stack(rows, axis=-2)
    pad = [(0, 0)] * x.ndim
    pad[-2] = (0, 8 - len(rows))
    return jnp.pad(x, pad)


def _layer(x, l, prm, wts, past, page_table):
    B, T, D = x.shape
    M = B * T
    tm = min(M, ROW_TILE)
    tm_w = min(M, ROW_TILE_STREAMED)
    x2 = x.reshape(M, D)
    x2 = _ffn(x2, prm["ffn1_norm"][l], wts["ffn1_w_gu"][l], wts["ffn1_w_down"][l], tm_w)
    z = _inproj(x2, prm["mix_norm"][l], wts["w_in"][l], tm_w)
    sizes, _, offs, _ = _z_layout()
    pee, w1e, w2e = wts["cmp"][l]
    kg = prm["nsa_k_norm"][l]
    p = jax.nn.softmax(prm["hg_lb_logits"], axis=0)
    lb = (jnp.cumsum(p, axis=0) - p[0])[l]
    if past is None:
        z3 = z.reshape(B, T, -1)
        qt, cmp_new, slc_new, win_new, sk, svt, wk, wvt, gt = _nsa_prep(z3, prm["nsa_q_norm"][l], kg, offs)
        ck, cvt = _cmp_prompt(cmp_new, pee, w1e, w2e, kg[0])
        o_a = _nsa_prompt(qt, ck, cvt, sk, svt, wk, wvt, gt)
        o_d, dn_state = _dn_prompt(z3, prm["dn_conv_w"][l], prm["dn_A_log"][l], prm["dn_dt_bias"][l],
                                   prm["dn_out_norm"][l], offs)
        o_h, hg_state = _hg_prompt(z3, lb, prm["hg_out_norm"][l], offs)
        rows = lambda a: a.reshape(B, -1, 2, NSA_KV, HEAD_DIM)
        cmp_new, slc_new, win_state = rows(cmp_new), rows(slc_new), rows(win_new[:, T - min(WINDOW, T):])
        conv_state = z3[:, T - (CONV_W - 1):, offs["d_qkv"]:offs["d_qkv"] + DN_QKV]
    else:
        zs = {n: z[:, offs[n]:offs[n] + sizes[n]].reshape(B, T, sizes[n]) for n in Z_ORDER if n != "m_g"}
        (o_a, o_d, o_h), (cmp_new, slc_new, win_state, conv_state, dn_state, hg_state) = _decode_mixers(
            zs, l, prm, (pee, w1e, w2e), lb, past, page_table)

    x2 = _merge(x2, o_a.reshape(M, NSA_W), o_d.reshape(M, DN_V), o_h.reshape(M, HG_WV), z, wts["w_branch"][l],
                wts["w_out"][l], tm)
    x2 = _ffn(x2, prm["ffn2_norm"][l], wts["ffn2_w_gu"][l], wts["ffn2_w_down"][l], tm_w)
    return x2.reshape(B, T, D), (cmp_new, slc_new, win_state, conv_state, dn_state, hg_state)


def _decode_mixers(zs, l, prm, cmp_w, lb, past, page_table):
    B, T = zs["a_q"].shape[:2]
    pee, w1e, w2e = cmp_w
    kg = prm["nsa_k_norm"][l]
    q = _head_rms(zs["a_q"].reshape(B, T, NSA_HEADS, HEAD_DIM), prm["nsa_q_norm"][l]) * (HEAD_DIM ** -0.5)
    kv = zs["a_kv"].reshape(B, T, 3, 2, NSA_KV, HEAD_DIM)
    cmp_new = kv[:, :, 0]
    slc_new = jnp.stack([_head_rms(kv[:, :, 1, 0], kg[1]), kv[:, :, 1, 1]], axis=2)
    win_new = jnp.stack([_head_rms(kv[:, :, 2, 0], kg[2]), kv[:, :, 2, 1]], axis=2)
    gates = jax.nn.sigmoid(zs["a_g"].reshape(B, T, NSA_HEADS, 3))
    qd = q.reshape(B, NSA_KV, NSA_GROUP, HEAD_DIM)
    qd = jnp.pad(qd, ((0, 0), (0, 0), (0, 8 - NSA_GROUP), (0, 0))).astype(BF16)
    new_rows = _rows8([cmp_new.reshape(B, ROW_W), slc_new.reshape(B, ROW_W), win_new.reshape(B, ROW_W)])
    gd = gates.reshape(B, NSA_KV, NSA_GROUP, 3)
    gd = jnp.pad(gd, ((0, 0), (0, 0), (0, 8 - NSA_GROUP), (0, LANE - 3)))
    o8 = _nsa_decode(page_table, l, qd, new_rows, gd, past["cmp"], past["slc"], past["win"], pee, w1e, w2e, kg[0])
    o_a = o8[:, :, :NSA_GROUP].reshape(B, T, NSA_W)

    d_qkv = zs["d_qkv"]
    xx = jnp.concatenate([past["conv"][l], d_qkv], axis=1)
    cw = prm["dn_conv_w"][l]
    qkv = sum(xx[:, j:j + T] * cw[j] for j in range(CONV_W))
    conv_state = xx[:, -(CONV_W - 1):]
    dq, dk, dv = jnp.split(jax.nn.silu(qkv), [DN_QK, 2 * DN_QK], axis=-1)
    dq = _l2n(dq.reshape(B, T, DN_HEADS, DN_DK)) * (DN_DK ** -0.5)
    dk = _l2n(dk.reshape(B, T, DN_HEADS, DN_DK))
    beta = jax.nn.sigmoid(zs["d_b"])
    g = -jnp.exp(prm["dn_A_log"][l]) * jax.nn.softplus(zs["d_a"] + prm["dn_dt_bias"][l])
    lanes = lambda a: jnp.broadcast_to(a[:, 0, :, None], (B, DN_HEADS, DN_DK))
    vec = _rows8([dk[:, 0], dq[:, 0], dv.reshape(B, DN_HEADS, DN_DV), lanes(jnp.exp(g)), lanes(beta)])
    o8, dn_state = _dn_decode(vec, past["dn_S"], l)
    o_d = o8[:, :, 0].reshape(B, T, DN_V)
    o_d = _head_rms(o_d.reshape(B, T, DN_HEADS, DN_DV), prm["dn_out_norm"][l]) * jax.nn.silu(
        zs["d_z"].reshape(B, T, DN_HEADS, DN_DV))

    zf = zs["r_f"]
    logf = jnp.log(lb + (1.0 - lb) * jax.nn.sigmoid(zf))
    k_in = (1.0 - lb) * jax.nn.sigmoid(-zf)
    hq = jax.nn.silu(zs["r_q"]) * (HG_DK ** -0.5)
    hd = lambda a: a.reshape(B, HG_HEADS, HG_DK)
    vec = _rows8([hd(k_in), hd(hq), hd(zs["r_i"]), hd(logf)])
    o8, hg_state = _hg_decode(vec, past["hg_S"], l)
    o_h = o8[:, :, 0].reshape(B, T, HG_WV)
    o_h = _head_rms(o_h.reshape(B, T, HG_HEADS, HG_DV), prm["hg_out_norm"][l]) * jax.nn.sigmoid(
        zs["r_og"].reshape(B, T, HG_HEADS, HG_DV))
    return (o_a, o_d, o_h), (cmp_new, slc_new, win_new, conv_state, dn_state, hg_state)


def _trunk(x, prm, wts, caches, page_table):
    new = []
    for l in range(DEPTH):
        x, st = _layer(x, l, prm, wts, caches, page_table)
        new.append(st)
    return x, [jnp.stack([s[i] for s in new], axis=0) for i in range(6)]


def kernel(x_prompt, x_sample, cache_cmp_kv, cache_slc_kv, cache_win_kv, state_dn_conv, state_dn_S, state_hg_S,
           page_table, ffn1_norm, ffn1_w_gu, ffn1_w_down, mix_norm, w_in, nsa_q_norm, nsa_k_norm, nsa_cmp_pe,
           nsa_cmp_w1, nsa_cmp_w2, dn_conv_w, dn_A_log, dn_dt_bias, dn_out_norm, hg_lb_logits, hg_out_norm,
           w_branch, w_out, ffn2_norm, ffn2_w_gu, ffn2_w_down):
    prm = dict(ffn1_norm=ffn1_norm, mix_norm=mix_norm, nsa_q_norm=nsa_q_norm, nsa_k_norm=nsa_k_norm,
               dn_conv_w=dn_conv_w, dn_A_log=dn_A_log, dn_dt_bias=dn_dt_bias, dn_out_norm=dn_out_norm,
               hg_lb_logits=hg_lb_logits, hg_out_norm=hg_out_norm, ffn2_norm=ffn2_norm)
    bf = lambda w: w.astype(BF16)
    wts = dict(ffn1_w_gu=bf(ffn1_w_gu), ffn1_w_down=bf(ffn1_w_down), ffn2_w_gu=bf(ffn2_w_gu),
               ffn2_w_down=bf(ffn2_w_down), w_branch=bf(w_branch), w_out=bf(w_out),
               w_in=jnp.stack([_permute_w_in(w_in[l]) for l in range(DEPTH)]),
               cmp=[_cmp_weights(nsa_cmp_pe[l], nsa_cmp_w1[l], nsa_cmp_w2[l]) for l in range(DEPTH)])
    y_p, (p_cmp, p_slc, p_win, p_conv, p_dn, p_hg) = _trunk(x_prompt, prm, wts, None, None)
    caches = dict(cmp=_rows_minor(cache_cmp_kv), slc=_rows_minor(cache_slc_kv), win=_rows_minor(cache_win_kv),
                  conv=state_dn_conv, dn_S=state_dn_S, hg_S=state_hg_S)
    y_s, (s_cmp, s_slc, s_win, s_conv, s_dn, s_hg) = _trunk(x_sample, prm, wts, caches, page_table)
    return (y_p, y_s, p_cmp, s_cmp, p_slc, s_slc, p_win, s_win, p_conv, s_conv, p_dn, s_dn, p_hg, s_hg)
```

```python
import functools

import jax
import jax.numpy as jnp
import numpy as np
from jax import lax
from jax.experimental import pallas as pl
from jax.experimental.pallas import tpu as pltpu

F32 = jnp.float32
BF16 = jnp.bfloat16

D_MODEL = 1024
DEPTH = 2
PAST_LEN = 2048
PAGE_SIZE = 128
HEAD_DIM = 64
NSA_HEADS = 8
NSA_KV = 2
NSA_GROUP = NSA_HEADS // NSA_KV
CMP_LEN = 32
CMP_STRIDE = 16
CMP_HIDDEN = 128
SLC_BLOCK = 64
N_SEL = 16
WINDOW = 512
Q_BLOCK = 128
DN_HEADS = 4
DN_DK = 128
DN_DV = 128
DN_CHUNK = 64
CONV_W = 4
HG_HEADS = 4
HG_DK = 128
HG_DV = 128
HG_CHUNK = 16
D_FF = 2816
NORM_EPS = 1e-6
L2_EPS = 1e-6
NEG_BIG = -1e30
SEL_BIG = 1e9

NSA_W = NSA_HEADS * HEAD_DIM
KV_W = NSA_KV * HEAD_DIM
ROW_W = 2 * KV_W
DN_QK = DN_HEADS * DN_DK
DN_V = DN_HEADS * DN_DV
DN_QKV = 2 * DN_QK + DN_V
HG_WK = HG_HEADS * HG_DK
HG_WV = HG_HEADS * HG_DV
MIX_W = NSA_W + DN_V + HG_WV
IN_SPLITS = (NSA_W, 6 * KV_W, 3 * NSA_HEADS, DN_QKV, DN_HEADS, DN_HEADS, DN_V, HG_WK, HG_WV, HG_WK, HG_WV, 3 * D_MODEL)
IN_NAMES = ("a_q", "a_kv", "a_g", "d_qkv", "d_b", "d_a", "d_z", "r_f", "r_i", "r_q", "r_og", "m_g")
Z_ORDER = ("d_qkv", "d_z", "r_f", "r_i", "r_q", "r_og", "a_q", "a_kv", "a_g", "d_b", "d_a", "m_g")
Z_TN = 512
CHUNK_W = CMP_STRIDE * ROW_W
CMP_HID_W = 2 * NSA_KV * CMP_HIDDEN

V7X_VMEM_BYTES = 64 * 1024 * 1024
LANE = 128
ROW_TILE = 512
ROW_TILE_STREAMED = 1024
ROW_TILE_INPROJ = 2048


def _cparams(sem, vmem_mb):
    assert vmem_mb * 1024 * 1024 < V7X_VMEM_BYTES
    return pltpu.CompilerParams(dimension_semantics=sem, vmem_limit_bytes=vmem_mb * 1024 * 1024)


def _dot(a, b):
    return jnp.dot(a, b, preferred_element_type=F32)


def _dot_nt(a, b):
    return lax.dot_general(a, b, (((1,), (1,)), ((), ())), preferred_element_type=F32)


def _dot_tn(a, b):
    return lax.dot_general(a, b, (((0,), (0,)), ((), ())), preferred_element_type=F32)


def _split2(a):
    hi = a.astype(BF16)
    lo = (a - hi.astype(F32)).astype(BF16)
    return hi, lo


def _dot3(a, b):
    ah, al = _split2(a)
    bh, bl = _split2(b)
    return _dot(ah, bh) + (_dot(ah, bl) + _dot(al, bh))


def _rms(x, g):
    return x * lax.rsqrt(jnp.mean(x * x, axis=-1, keepdims=True) + NORM_EPS) * g


def _silu(x):
    return x * jax.nn.sigmoid(x)


def _iota(shape, dim):
    return lax.broadcasted_iota(jnp.int32, shape, dim)


def _alibi_slope(head_in_group, g):
    out = jnp.full(head_in_group.shape, 2.0 ** -(NSA_GROUP * g + NSA_GROUP), F32)
    for j in range(NSA_GROUP - 2, -1, -1):
        out = jnp.where(head_in_group == j, 2.0 ** -(NSA_GROUP * g + j + 1), out)
    return out


def _ffn_kernel(x_ref, g_ref, wg_ref, wu_ref, wd_ref, o_ref, xn_ref, acc_ref, *, nf):
    j = pl.program_id(1)

    @pl.when(j == 0)
    def _():
        xn_ref[...] = _rms(x_ref[...], g_ref[...]).astype(BF16)
        acc_ref[...] = jnp.zeros_like(acc_ref)

    xn = xn_ref[...]
    a = _silu(_dot(xn, wg_ref[...])) * _dot(xn, wu_ref[...])
    acc_ref[...] += _dot(a.astype(BF16), wd_ref[...])

    @pl.when(j == nf - 1)
    def _():
        o_ref[...] = x_ref[...] + 0.5 * acc_ref[...]


def _ffn(x, gain, w_gu, w_down, tm, tf=256):
    M, D = x.shape
    F = w_down.shape[0]
    nf = F // tf
    return pl.pallas_call(
        functools.partial(_ffn_kernel, nf=nf),
        grid=(M // tm, nf),
        in_specs=[
            pl.BlockSpec((tm, D), lambda i, j: (i, 0)),
            pl.BlockSpec((1, D), lambda i, j: (0, 0)),
            pl.BlockSpec((D, tf), lambda i, j: (0, j)),
            pl.BlockSpec((D, tf), lambda i, j: (0, j + nf)),
            pl.BlockSpec((tf, D), lambda i, j: (j, 0)),
        ],
        out_specs=pl.BlockSpec((tm, D), lambda i, j: (i, 0)),
        out_shape=jax.ShapeDtypeStruct((M, D), F32),
        scratch_shapes=[pltpu.VMEM((tm, D), BF16), pltpu.VMEM((tm, D), F32)],
        compiler_params=_cparams(("parallel", "arbitrary"), 40),
        name="ffn",
    )(x, gain.reshape(1, D), w_gu, w_gu, w_down)


def _inproj_kernel(x_ref, g_ref, w_ref, z_ref, gate_ref, xn_ref, *, nz):
    j = pl.program_id(1)

    @pl.when(j == 0)
    def _():
        xn_ref[...] = _rms(x_ref[...], g_ref[...]).astype(BF16)

    acc = _dot(xn_ref[...], w_ref[...])

    @pl.when(j < nz)
    def _():
        z_ref[...] = acc

    @pl.when(j >= nz)
    def _():
        gate_ref[...] = jax.nn.sigmoid(acc).astype(gate_ref.dtype)


def _inproj(x, gain, w, tm):
    M, D = x.shape
    N = w.shape[1]
    z_width = _z_layout()[2]["m_g"]
    nz = z_width // Z_TN
    return pl.pallas_call(
        functools.partial(_inproj_kernel, nz=nz),
        grid=(M // tm, N // Z_TN),
        in_specs=[
            pl.BlockSpec((tm, D), lambda i, j: (i, 0)),
            pl.BlockSpec((1, D), lambda i, j: (0, 0)),
            pl.BlockSpec((D, Z_TN), lambda i, j: (0, j)),
        ],
        out_specs=[pl.BlockSpec((tm, Z_TN), lambda i, j: (i, jnp.minimum(j, nz - 1))),
                   pl.BlockSpec((tm, Z_TN), lambda i, j: (i, jnp.maximum(j - nz, 0)))],
        out_shape=[jax.ShapeDtypeStruct((M, z_width), F32), jax.ShapeDtypeStruct((M, N - z_width), BF16)],
        scratch_shapes=[pltpu.VMEM((tm, D), BF16)],
        compiler_params=_cparams(("parallel", "arbitrary"), 48),
        name="inproj",
    )(x, gain.reshape(1, D), w)


def _z_layout():
    sizes = dict(zip(IN_NAMES, IN_SPLITS))
    src = dict(zip(IN_NAMES, np.cumsum((0,) + IN_SPLITS[:-1]).tolist()))
    offs, o = {}, 0
    for n in Z_ORDER:
        if n == "m_g":
            o = -(-o // Z_TN) * Z_TN
        offs[n] = o
        o += sizes[n]
    assert Z_ORDER[-1] == "m_g" and o % Z_TN == 0
    return sizes, src, offs, o


def _permute_w_in(w_in):
    sizes, src, offs, total = _z_layout()
    w = jnp.zeros((w_in.shape[0], total), w_in.dtype)
    for n in Z_ORDER:
        w = lax.dynamic_update_slice(w, w_in[:, src[n]:src[n] + sizes[n]], (0, offs[n]))
    return w.astype(BF16)


def _merge_kernel(x_ref, oa_ref, od_ref, oh_ref, g0_ref, g1_ref, g2_ref, wb_ref, wo_ref, o_ref):
    m = g0_ref[...].astype(F32) * _dot(oa_ref[...].astype(BF16), wb_ref[0:NSA_W, :])
    m += g1_ref[...].astype(F32) * _dot(od_ref[...].astype(BF16), wb_ref[NSA_W:NSA_W + DN_V, :])
    m += g2_ref[...].astype(F32) * _dot(oh_ref[...].astype(BF16), wb_ref[NSA_W + DN_V:MIX_W, :])
    o_ref[...] = x_ref[...] + _dot(m.astype(BF16), wo_ref[...])


def _merge(x, o_a, o_d, o_h, z, w_branch, w_out, tm):
    M, D = x.shape
    row = lambda w: pl.BlockSpec((tm, w), lambda i: (i, 0))
    return pl.pallas_call(
        _merge_kernel,
        grid=(M // tm,),
        in_specs=[
            row(D), row(NSA_W), row(DN_V), row(HG_WV),
            pl.BlockSpec((tm, D), lambda i: (i, 0)),
            pl.BlockSpec((tm, D), lambda i: (i, 1)),
            pl.BlockSpec((tm, D), lambda i: (i, 2)),
            pl.BlockSpec((MIX_W, D), lambda i: (0, 0)),
            pl.BlockSpec((D, D), lambda i: (0, 0)),
        ],
        out_specs=row(D),
        out_shape=jax.ShapeDtypeStruct((M, D), F32),
        compiler_params=_cparams(("parallel",), 40),
        name="merge",
    )(x, o_a, o_d, o_h, z, z, z, w_branch, w_out)


def _cmp_weights(pe, w1, w2):
    n_part = CMP_LEN // CMP_STRIDE
    eye = jnp.eye(NSA_KV, dtype=F32)
    eye2 = jnp.eye(2, dtype=F32)
    w1r = w1.reshape(2, n_part, CMP_STRIDE, HEAD_DIM, CMP_HIDDEN)
    w1e = jnp.einsum("kmrdh,kK,gG->mrkgdKGh", w1r, eye2, eye).reshape(n_part, CHUNK_W, CMP_HID_W)
    w2e = jnp.einsum("khd,kK,gG->kghKGd", w2, eye2, eye).reshape(CMP_HID_W, ROW_W)
    per = pe.reshape(2, n_part, CMP_STRIDE, HEAD_DIM).transpose(1, 2, 0, 3)
    pee = jnp.broadcast_to(per[:, :, :, None, :], (n_part, CMP_STRIDE, 2, NSA_KV, HEAD_DIM)).reshape(n_part, CHUNK_W)
    return pee, w1e.astype(BF16), w2e.astype(BF16)


def _compress(x, xnext_rows, pe_ref, w1_ref, w2_ref, kg_ref, n_seq=1):
    rows = x.shape[0]
    n = rows // n_seq
    p0 = _dot((x + pe_ref[0:1, :]).astype(BF16), w1_ref[0])
    p1 = _dot((x + pe_ref[1:2, :]).astype(BF16), w1_ref[1])
    p1s = pltpu.roll(p1, rows - 1, 0)
    if xnext_rows is not None:
        p1n = _dot((xnext_rows + pe_ref[1:2, :]).astype(BF16), w1_ref[1])
        for s in range(n_seq):
            p1s = jnp.where(_iota((rows, 1), 0) == (s + 1) * n - 1, p1n[s:s + 1, :], p1s)
    comp = _dot(_silu(p0 + p1s).astype(BF16), w2_ref[...])
    cks = []
    for g in range(NSA_KV):
        kc = comp[:, g * HEAD_DIM:(g + 1) * HEAD_DIM]
        cks.append(_rms(kc, kg_ref[...]))
    return cks, comp[:, KV_W:]


def _cmp_prompt_kernel(x_ref, pe_ref, w1_ref, w2_ref, kg_ref, ck_ref, cvt_ref):
    cks, cv = _compress(x_ref[0], None, pe_ref, w1_ref, w2_ref, kg_ref)
    for g in range(NSA_KV):
        ck_ref[0, g] = cks[g].astype(BF16)
    cvt = cv.T
    for g in range(NSA_KV):
        cvt_ref[0, g] = cvt[g * HEAD_DIM:(g + 1) * HEAD_DIM, :].astype(BF16)


def _cmp_prompt(cmp_rows, pee, w1e, w2e, kg0):
    B, T = cmp_rows.shape[:2]
    nc = T // CMP_STRIDE
    x = cmp_rows.reshape(B, nc, CHUNK_W)
    full = lambda s: pl.BlockSpec(s, lambda b: (0,) * len(s))
    return pl.pallas_call(
        _cmp_prompt_kernel,
        grid=(B,),
        in_specs=[pl.BlockSpec((1, nc, CHUNK_W), lambda b: (b, 0, 0)), full(pee.shape), full(w1e.shape),
                  full(w2e.shape), full((1, HEAD_DIM))],
        out_specs=[pl.BlockSpec((1, NSA_KV, nc, HEAD_DIM), lambda b: (b, 0, 0, 0)),
                   pl.BlockSpec((1, NSA_KV, HEAD_DIM, nc), lambda b: (b, 0, 0, 0))],
        out_shape=[jax.ShapeDtypeStruct((B, NSA_KV, nc, HEAD_DIM), BF16),
                   jax.ShapeDtypeStruct((B, NSA_KV, HEAD_DIM, nc), BF16)],
        compiler_params=_cparams(("parallel",), 48),
        name="nsa_compress",
    )(x, pee, w1e, w2e, kg0.reshape(1, HEAD_DIM))


def _softmax_first(s, vt):
    m = jnp.max(s, axis=0, keepdims=True)
    e = jnp.exp(s - m)
    return m, jnp.sum(e, axis=0, keepdims=True), _dot(vt, e.astype(BF16))


def _softmax_step(s, vt, carry):
    m, l, acc = carry
    m_new = jnp.maximum(m, jnp.max(s, axis=0, keepdims=True))
    alpha = jnp.exp(m - m_new)
    e = jnp.exp(s - m_new)
    l = alpha * l + jnp.sum(e, axis=0, keepdims=True)
    acc = alpha * acc + _dot(vt, e.astype(BF16))
    return m_new, l, acc


KC_W = 2 * LANE - HEAD_DIM
KC_POS = 64
KC_POS_RADIX = 128
KW = HEAD_DIM + KC_W


def _key_features(T):
    k = np.arange(T)
    f = np.zeros((T, KW), np.float32)
    f[k, HEAD_DIM + k // SLC_BLOCK] = 1.0
    f[:, HEAD_DIM + KC_POS] = k // KC_POS_RADIX
    f[:, HEAD_DIM + KC_POS + 1] = k % KC_POS_RADIX
    return f


def _nsa_prompt_kernel(qt_ref, ck_ref, cvt_ref, sk_ref, svt_ref, wk_ref, wvt_ref, gt_ref, ovl_ref, o_ref,
                       *, n_cmp, n_blk):
    qb = pl.program_id(1)
    b0 = qb * Q_BLOCK
    QW = NSA_GROUP * Q_BLOCK
    TK = Q_BLOCK
    lane = _iota((1, QW), 1)
    tq = lane % Q_BLOCK
    q_pos = b0 + tq
    head = lane // Q_BLOCK
    qp1 = b0 + _iota((1, Q_BLOCK), 1)
    bj = _iota((n_blk, 1), 0)
    krel = _iota((TK, 1), 0) - tq
    frow = _iota((KC_W, 1), 0)
    k_diag = pl.multiple_of(b0, TK)
    gt = gt_ref[0, 0]

    def gate(g, branch):
        return jnp.concatenate([gt[(NSA_GROUP * g + j) * 3 + branch:(NSA_GROUP * g + j) * 3 + branch + 1, :]
                                for j in range(NSA_GROUP)], axis=1)

    qc_slcs, qc_wins, o_cmps = [], [], []
    for g in range(NSA_KV):
        slope = _alibi_slope(head, g)
        qgt = jnp.concatenate([qt_ref[0, NSA_GROUP * g + j] for j in range(NSA_GROUP)], axis=1)
        pos_rows = jnp.where(frow == KC_POS, slope * float(KC_POS_RADIX), jnp.where(frow == KC_POS + 1, slope, 0.0))

        dist = q_pos - (_iota((n_cmp, 1), 0) * CMP_STRIDE + (CMP_LEN - 1))
        mask = dist >= 0
        s = jnp.where(mask, _dot(ck_ref[0, g], qgt) - slope * dist.astype(F32), NEG_BIG)
        m = jnp.max(s, axis=0, keepdims=True)
        e = jnp.where(mask, jnp.exp(s - m), 0.0)
        p = (e / jnp.maximum(jnp.sum(e, axis=0, keepdims=True), 1e-30)).astype(BF16)
        o_cmps.append(_dot(cvt_ref[0, g], p))

        imp = _dot(ovl_ref[...], p[:, 0:Q_BLOCK])
        for j in range(1, NSA_GROUP):
            imp += _dot(ovl_ref[...], p[:, j * Q_BLOCK:(j + 1) * Q_BLOCK])
        cur = qp1 // SLC_BLOCK
        forced = (bj == 0) | (bj == cur) | (bj == cur - 1)
        imp = jnp.where(forced, SEL_BIG, jnp.where(bj * SLC_BLOCK <= qp1, imp, -SEL_BIG))
        tiles = [imp[8 * v:8 * v + 8, :] for v in range(n_blk // 8)]
        ranks = [jnp.zeros((8, Q_BLOCK), F32) for _ in tiles]
        for i in range(n_blk):
            row = imp[i:i + 1, :]
            for v, tile in enumerate(tiles):
                if 8 * v > i:
                    beats = row >= tile
                elif 8 * v + 8 <= i:
                    beats = row > tile
                else:
                    beats = (row > tile) | ((row == tile) & (bj[8 * v:8 * v + 8] > i))
                ranks[v] = ranks[v] + jnp.where(beats, 1.0, 0.0)
        rank = jnp.concatenate(ranks, axis=0)
        selb = jnp.where(rank < float(min(N_SEL, n_blk)), 0.0, NEG_BIG)
        selb = jnp.concatenate([selb] * NSA_GROUP, axis=1)
        if n_blk < KC_W:
            selb = jnp.concatenate([selb, jnp.zeros((KC_W - n_blk, QW), F32)], axis=0)
        qc_slcs.append(jnp.concatenate([qgt, (selb + pos_rows).astype(BF16)], axis=0))
        qc_wins.append(jnp.concatenate([qgt, pos_rows.astype(BF16)], axis=0))

    def scores(k_ref, g, k0, n, qcs):
        return _dot(k_ref[0, g, pl.ds(k0, n), :], qcs[g])

    WS = WINDOW + Q_BLOCK
    ws = pl.multiple_of(jnp.maximum(b0 - WINDOW, 0), TK)
    off = b0 - ws
    drel = tq - _iota((WS, 1), 0)
    in_win = (drel >= -off) & (drel < WINDOW - off)
    o_wins = []
    for g in range(NSA_KV):
        s = jnp.where(in_win, scores(wk_ref, g, ws, WS, qc_wins), NEG_BIG)
        _, l, acc = _softmax_first(s, wvt_ref[0, g, :, pl.ds(ws, WS)])
        o_wins.append(acc / jnp.maximum(l, 1e-30))

    carry = []
    for g in range(NSA_KV):
        s = jnp.where(krel <= 0, scores(sk_ref, g, k_diag, TK, qc_slcs), NEG_BIG)
        carry.append(_softmax_first(s, svt_ref[0, g, :, pl.ds(k_diag, TK)]))

    def slc_body(n):
        def body(i, carry):
            k0 = pl.multiple_of(i * n, n)
            G = range(NSA_KV)
            ss = [scores(sk_ref, g, k0, n, qc_slcs) for g in G]
            ms = [jnp.maximum(carry[g][0], jnp.max(ss[g], axis=0, keepdims=True)) for g in G]
            es = [jnp.exp(ss[g] - ms[g]) for g in G]
            pvs = [_dot(svt_ref[0, g, :, pl.ds(k0, n)], es[g].astype(BF16)) for g in G]
            out = []
            for g in G:
                m, l, acc = carry[g]
                alpha = jnp.exp(m - ms[g])
                out.append((ms[g], alpha * l + jnp.sum(es[g], axis=0, keepdims=True), alpha * acc + pvs[g]))
            return tuple(out)
        return body

    carry = tuple(carry)
    done = 0
    for width in (4, 2, 1):
        trips = (qb - done) // width
        carry = lax.fori_loop(done // width, done // width + trips, slc_body(width * TK), carry)
        done = done + trips * width

    outs = []
    for g in range(NSA_KV):
        _, l, acc = carry[g]
        o_slc = acc / jnp.maximum(l, 1e-30)
        outs.append(gate(g, 0) * o_cmps[g] + gate(g, 1) * o_slc + gate(g, 2) * o_wins[g])

    for g in range(NSA_KV):
        for jp in range(NSA_GROUP // 2):
            pair = jnp.concatenate([outs[g][:, (2 * jp) * Q_BLOCK:(2 * jp + 1) * Q_BLOCK],
                                    outs[g][:, (2 * jp + 1) * Q_BLOCK:(2 * jp + 2) * Q_BLOCK]], axis=0)
            c0 = (NSA_GROUP * g + 2 * jp) * HEAD_DIM
            o_ref[0, :, c0:c0 + 2 * HEAD_DIM] = pair.T.astype(o_ref.dtype)


def _overlap(n_cmp, n_blk):
    ci = np.arange(n_cmp)[:, None] * CMP_STRIDE
    bj = np.arange(n_blk)[None, :]
    return ((ci < (bj + 1) * SLC_BLOCK) & (ci + CMP_LEN > bj * SLC_BLOCK)).astype(np.float32)


def _nsa_prompt(qt, ck, cvt, sk, svt, wk, wvt, gates):
    B, _, _, T = qt.shape
    nq = T // Q_BLOCK
    n_cmp = ck.shape[2]
    n_blk = T // SLC_BLOCK
    assert n_blk <= KC_POS and T <= KC_POS_RADIX * 256 and T >= WINDOW + Q_BLOCK
    ovl = jnp.asarray(_overlap(n_cmp, n_blk).T, BF16)
    per_b = lambda s: pl.BlockSpec((1,) + s, lambda b, i: (b,) + (0,) * len(s))
    return pl.pallas_call(
        functools.partial(_nsa_prompt_kernel, n_cmp=n_cmp, n_blk=n_blk),
        grid=(B, nq),
        in_specs=[
            pl.BlockSpec((1, NSA_HEADS, HEAD_DIM, Q_BLOCK), lambda b, i: (b, 0, 0, i)),
            per_b((NSA_KV, n_cmp, HEAD_DIM)), per_b((NSA_KV, HEAD_DIM, n_cmp)),
            per_b((NSA_KV, T, KW)), per_b((NSA_KV, HEAD_DIM, T)),
            per_b((NSA_KV, T, KW)), per_b((NSA_KV, HEAD_DIM, T)),
            pl.BlockSpec((1, 1, GATE_ROWS, Q_BLOCK), lambda b, i: (b, i, 0, 0)),
            pl.BlockSpec((n_blk, n_cmp), lambda b, i: (0, 0)),
        ],
        out_specs=pl.BlockSpec((1, Q_BLOCK, NSA_W), lambda b, i: (b, i, 0)),
        out_shape=jax.ShapeDtypeStruct((B, T, NSA_W), BF16),
        compiler_params=_cparams(("parallel", "arbitrary"), 48),
        name="nsa_prompt",
    )(qt, ck, cvt, sk, svt, wk, wvt, gates, ovl)


GATE_ROWS = 32


def _pair_rms(x, gain2, ones_bd):
    hi, lo = _split2(x * x)
    ms = (_dot(hi, ones_bd) + _dot(lo, ones_bd)) * (1.0 / HEAD_DIM)
    return x * lax.rsqrt(ms + NORM_EPS) * gain2


def _nsa_prep_kernel(zq_ref, zc0_ref, zc1_ref, zs0_ref, zs1_ref, zw0_ref, zw1_ref, zg_ref, qg_ref, kg_ref, kf_ref,
                     qt_ref, cmp_ref, slc_ref, win_ref, sk_ref, svt_ref, wk_ref, wvt_ref, gt_ref):
    lane = _iota((1, LANE), 1)
    ones_bd = jnp.where(_iota((LANE, LANE), 0) // HEAD_DIM == _iota((LANE, LANE), 1) // HEAD_DIM, 1.0, 0.0).astype(BF16)
    for hp in range(NSA_HEADS // 2):
        qn = _pair_rms(zq_ref[0, :, hp * LANE:(hp + 1) * LANE], qg_ref[...], ones_bd) * (HEAD_DIM ** -0.5)
        qnt = qn.T
        qt_ref[0, 2 * hp] = qnt[0:HEAD_DIM].astype(BF16)
        qt_ref[0, 2 * hp + 1] = qnt[HEAD_DIM:].astype(BF16)
    cmp_ref[0, :, 0:KV_W] = zc0_ref[0]
    cmp_ref[0, :, KV_W:] = zc1_ref[0]
    feat = kf_ref[...]
    for i, (zk_ref, zv_ref, rows_ref, ka_ref, vt_ref) in enumerate(
            ((zs0_ref, zs1_ref, slc_ref, sk_ref, svt_ref), (zw0_ref, zw1_ref, win_ref, wk_ref, wvt_ref))):
        k = _pair_rms(zk_ref[0], kg_ref[i:i + 1, :], ones_bd)
        v = zv_ref[0]
        rows_ref[0, :, 0:KV_W] = k
        rows_ref[0, :, KV_W:] = v
        for g in range(NSA_KV):
            kg = k if g == 0 else pltpu.roll(k, HEAD_DIM, 1)
            ka_ref[0, g, :, 0:LANE] = jnp.where(lane < HEAD_DIM, kg.astype(BF16), feat[:, 0:LANE])
            ka_ref[0, g, :, LANE:] = feat[:, LANE:]
        vt = v.T
        for g in range(NSA_KV):
            vt_ref[0, g] = vt[g * HEAD_DIM:(g + 1) * HEAD_DIM].astype(BF16)
    gt_ref[0, 0] = jax.nn.sigmoid(zg_ref[0]).T[0:GATE_ROWS]


def _nsa_prep(z, q_gain, k_gains, offs):
    B, T, _ = z.shape
    nq = T // Q_BLOCK
    assert offs["a_q"] % NSA_W == 0 and offs["a_kv"] % LANE == 0 and offs["a_g"] % LANE == 0
    kv0 = offs["a_kv"] // LANE
    zcol = lambda c: pl.BlockSpec((1, Q_BLOCK, LANE), lambda b, i: (b, i, c))
    full = lambda s: pl.BlockSpec(s, lambda b, i: (0,) * len(s))
    rows = pl.BlockSpec((1, Q_BLOCK, ROW_W), lambda b, i: (b, i, 0))
    keys = pl.BlockSpec((1, NSA_KV, Q_BLOCK, KW), lambda b, i: (b, 0, i, 0))
    vals = pl.BlockSpec((1, NSA_KV, HEAD_DIM, Q_BLOCK), lambda b, i: (b, 0, 0, i))
    two = lambda gain: jnp.concatenate([gain, gain], axis=-1)
    return pl.pallas_call(
        _nsa_prep_kernel,
        grid=(B, nq),
        in_specs=[pl.BlockSpec((1, Q_BLOCK, NSA_W), lambda b, i, c=offs["a_q"] // NSA_W: (b, i, c))]
        + [zcol(kv0 + j) for j in range(6)] + [zcol(offs["a_g"] // LANE)]
        + [full((1, LANE)), full((2, LANE)), pl.BlockSpec((Q_BLOCK, KW), lambda b, i: (i, 0))],
        out_specs=[pl.BlockSpec((1, NSA_HEADS, HEAD_DIM, Q_BLOCK), lambda b, i: (b, 0, 0, i)), rows, rows, rows,
                   keys, vals, keys, vals, pl.BlockSpec((1, 1, GATE_ROWS, Q_BLOCK), lambda b, i: (b, i, 0, 0))],
        out_shape=[jax.ShapeDtypeStruct((B, NSA_HEADS, HEAD_DIM, T), BF16)]
        + [jax.ShapeDtypeStruct((B, T, ROW_W), F32)] * 3
        + [jax.ShapeDtypeStruct((B, NSA_KV, T, KW), BF16), jax.ShapeDtypeStruct((B, NSA_KV, HEAD_DIM, T), BF16)] * 2
        + [jax.ShapeDtypeStruct((B, nq, GATE_ROWS, Q_BLOCK), F32)],
        compiler_params=_cparams(("parallel", "parallel"), 32),
        name="nsa_prep",
    )(z, z, z, z, z, z, z, z, two(q_gain).reshape(1, LANE), two(k_gains[1:3]), jnp.asarray(_key_features(T), BF16))


NSA_DECODE_SEQS = 2

def _nsa_decode_kernel(pt_ref, q_ref, new_ref, gt_ref, pe_ref, w1_ref, w2_ref, kg_ref, ovl_ref, exp_ref, *rest,
                       n_pages, n_seq):
    del pt_ref
    all_cmp_pages = rest[:n_seq * n_pages]
    all_slc_pages = rest[n_seq * n_pages:2 * n_seq * n_pages]
    win_ref, o_ref, rows_ref, x_ref = rest[2 * n_seq * n_pages:2 * n_seq * n_pages + 4]
    n_past = n_pages * PAGE_SIZE
    n_cmp = n_past // CMP_STRIDE
    q_pos = n_past
    halves = ROW_W // LANE
    for p, r in enumerate(all_cmp_pages):
        for c in range(halves):
            rows_ref[c, p * PAGE_SIZE:(p + 1) * PAGE_SIZE, :] = r[0, 0, c * LANE:(c + 1) * LANE, :].T
    for r in range(CMP_STRIDE):
        for c in range(halves):
            x_ref[:, r * ROW_W + c * LANE:r * ROW_W + (c + 1) * LANE] = rows_ref.at[c][
                pl.ds(r, n_seq * n_cmp, stride=CMP_STRIDE), :]
    jrow = _iota((8, 1), 0)
    xnew = jnp.zeros((8, ROW_W), F32)
    for sq in range(n_seq):
        xnew = jnp.where(jrow == sq, new_ref[sq][0:1, 0:ROW_W], xnew)
    xnew = jnp.concatenate([xnew, jnp.zeros((8, CHUNK_W - ROW_W), F32)], axis=1)
    all_cks, all_cv = _compress(x_ref[...], xnew, pe_ref, w1_ref, w2_ref, kg_ref, n_seq)

    lanes = _iota((1, LANE), 1)
    n_blk = (n_past + 1 + SLC_BLOCK - 1) // SLC_BLOCK
    cur = q_pos // SLC_BLOCK
    ii = _iota((LANE, LANE), 0)
    jj = _iota((LANE, LANE), 1)
    n_win = win_ref.shape[3]
    kpos = _iota((1, n_past), 1)
    wpos = (q_pos - n_win) + _iota((1, n_win), 1)
    dist_w = q_pos - wpos
    mask_w = (dist_w < WINDOW) & (wpos >= 0)
    dist_c = q_pos - (lanes * CMP_STRIDE + (CMP_LEN - 1))
    mask_c = dist_c >= 0
    forced = (lanes == 0) | (lanes == cur) | (lanes == cur - 1)

    ch = []
    for sq in range(n_seq):
        for g in range(NSA_KV):
            lo = g * HEAD_DIM
            rows = slice(sq * n_cmp, (sq + 1) * n_cmp)
            ch.append(dict(sq=sq, g=g, lo=lo, hi=lo + HEAD_DIM, qg=q_ref[sq, g], slope=_alibi_slope(jrow, g),
                           ck=all_cks[g][rows].astype(BF16), cv=all_cv[rows, lo:lo + HEAD_DIM].astype(BF16),
                           new=new_ref[sq], pages=all_slc_pages[sq * n_pages:(sq + 1) * n_pages]))

    def new_key(c, row):
        return c["new"][row:row + 1, c["lo"]:c["hi"]], c["new"][row:row + 1, KV_W + c["lo"]:KV_W + c["hi"]]

    for c in ch:
        c["s"] = jnp.where(mask_c, _dot_nt(c["qg"], c["ck"]) - c["slope"] * dist_c.astype(F32), NEG_BIG)
    for c in ch:
        e = jnp.where(mask_c, jnp.exp(c["s"] - jnp.max(c["s"], axis=1, keepdims=True)), 0.0)
        c["p"] = (e / jnp.maximum(jnp.sum(e, axis=1, keepdims=True), 1e-30)).astype(BF16)
    for c in ch:
        c["o_cmp"] = _dot(c["p"], c["cv"])
        c["imp"] = _dot(c["p"], ovl_ref[...])
    for c in ch:
        imp = jnp.sum(jnp.where(jrow < NSA_GROUP, c["imp"], 0.0), axis=0, keepdims=True)
        imp = jnp.where(forced, SEL_BIG, jnp.where(lanes * SLC_BLOCK <= q_pos, imp, -SEL_BIG))
        imp = jnp.where(lanes < n_blk, imp, -3e38)
        impr = jnp.broadcast_to(imp, (LANE, LANE))
        impc = jnp.sum(jnp.where(ii == jj, impr, 0.0), axis=1, keepdims=True)
        beats = jnp.where(impc > impr, 1.0, jnp.where((impc == impr) & (ii < jj), 1.0, 0.0))
        rank = jnp.sum(beats, axis=0, keepdims=True)
        c["sel"] = jnp.where(rank < float(min(N_SEL, n_blk)), 1.0, 0.0)
    for c in ch:
        c["selk"] = _dot(jnp.broadcast_to(c["sel"], (8, LANE)).astype(BF16), exp_ref[...])

    def attend(scores, vts, masks, dists, new_rows, new_masks):
        outs = []
        ss, s_news = [], []
        for c, s, mask, dist, row, mask_new in zip(ch, scores, masks, dists, new_rows, new_masks):
            ss.append(jnp.where(mask, s - c["slope"] * dist.astype(F32), NEG_BIG))
            knew, _ = new_key(c, row)
            s_new = jnp.sum(c["qg"].astype(F32) * knew.astype(BF16).astype(F32), axis=1, keepdims=True)
            s_news.append(s_new if mask_new is None else jnp.where(mask_new, s_new, NEG_BIG))
        ms = [jnp.maximum(jnp.max(s, axis=1, keepdims=True), s_new) for s, s_new in zip(ss, s_news)]
        es = [jnp.where(mask, jnp.exp(s - m), 0.0) for s, m, mask in zip(ss, ms, masks)]
        pvs = [_dot_nt(e.astype(BF16), vt) for e, vt in zip(es, vts)]
        for c, e, m, s_new, pv, row, mask_new in zip(ch, es, ms, s_news, pvs, new_rows, new_masks):
            e_new = jnp.exp(s_new - m)
            if mask_new is not None:
                e_new = jnp.where(mask_new, e_new, 0.0)
            _, vnew = new_key(c, row)
            l = jnp.sum(e, axis=1, keepdims=True) + e_new
            o = pv + e_new.astype(BF16).astype(F32) * vnew.astype(BF16).astype(F32)
            outs.append(o / jnp.maximum(l, 1e-30))
        return outs

    def paged(c, off):
        return jnp.concatenate([r[0, 0, off + c["lo"]:off + c["hi"], :] for r in c["pages"]], axis=1).astype(BF16)

    n_ch = len(ch)
    o_slc = attend([_dot(c["qg"], paged(c, 0)) for c in ch], [paged(c, KV_W) for c in ch],
                   [c["selk"] > 0.5 for c in ch], [q_pos - kpos] * n_ch, [1] * n_ch,
                   [c["sel"][:, cur:cur + 1] > 0.5 for c in ch])
    o_win = attend([_dot(c["qg"], win_ref[0, c["sq"], c["lo"]:c["hi"], :].astype(BF16)) for c in ch],
                   [win_ref[0, c["sq"], KV_W + c["lo"]:KV_W + c["hi"], :].astype(BF16) for c in ch],
                   [mask_w] * n_ch, [dist_w] * n_ch, [2] * n_ch, [None] * n_ch)
    for c, o_s, o_w in zip(ch, o_slc, o_win):
        gt = gt_ref[c["sq"], c["g"]]
        o_ref[c["sq"], c["g"]] = gt[:, 0:1] * c["o_cmp"] + gt[:, 1:2] * o_s + gt[:, 2:3] * o_w


def _rows_minor(cache):
    lead = cache.ndim - 4
    perm = tuple(range(lead)) + (lead + 1, lead + 2, lead + 3, lead)
    return cache.transpose(perm).reshape(cache.shape[:lead] + (ROW_W, cache.shape[lead]))


def _nsa_decode(page_table, l, q, new_rows, gates, cmp_t, slc_t, win_t, pee, w1e, w2e, kg0):
    B, n_pages = page_table.shape
    n_past = n_pages * PAGE_SIZE
    n_cmp = n_past // CMP_STRIDE
    n_win = win_t.shape[3]
    assert n_cmp == LANE and n_win <= n_past
    n_blk = (n_past + 1 + SLC_BLOCK - 1) // SLC_BLOCK
    ovl = np.zeros((n_cmp, LANE), np.float32)
    ovl[:, :n_blk] = _overlap(n_cmp, n_blk)
    expand = (np.arange(n_past)[None, :] // SLC_BLOCK == np.arange(LANE)[:, None]).astype(np.float32)
    S = NSA_DECODE_SEQS
    assert B % S == 0
    full = lambda s: pl.BlockSpec(s, lambda i, pt: (0,) * len(s))
    per_b = lambda s: pl.BlockSpec((S,) + s, lambda i, pt: (i,) + (0,) * len(s))
    page = lambda sq, p: pl.BlockSpec((1, 1, ROW_W, PAGE_SIZE), lambda i, pt: (l, pt[S * i + sq, p], 0, 0))
    pages = [page(sq, p) for sq in range(S) for p in range(n_pages)]

    in_specs = [per_b((NSA_KV, 8, HEAD_DIM)), per_b((8, ROW_W)), per_b((NSA_KV, 8, LANE)), full(pee.shape),
                full(w1e.shape), full(w2e.shape), full((1, HEAD_DIM)), full(ovl.shape), full(expand.shape)]
    in_specs += pages * 2
    in_specs += [pl.BlockSpec((1, S, ROW_W, n_win), lambda i, pt: (l, i, 0, 0))]
    return pl.pallas_call(
        functools.partial(_nsa_decode_kernel, n_pages=n_pages, n_seq=S),
        grid_spec=pltpu.PrefetchScalarGridSpec(
            num_scalar_prefetch=1, grid=(B // S,), in_specs=in_specs,
            out_specs=pl.BlockSpec((S, NSA_KV, 8, HEAD_DIM), lambda i, pt: (i, 0, 0, 0)),
            scratch_shapes=[pltpu.VMEM((ROW_W // LANE, S * n_past, LANE), F32),
                            pltpu.VMEM((S * n_cmp, CHUNK_W), F32)]),
        out_shape=jax.ShapeDtypeStruct((B, NSA_KV, 8, HEAD_DIM), F32),
        compiler_params=_cparams(("arbitrary",), 56),
        name="nsa_decode",
    )(page_table, q, new_rows, gates, pee, w1e, w2e, kg0.reshape(1, HEAD_DIM), jnp.asarray(ovl, BF16),
      jnp.asarray(expand, BF16), *([cmp_t] * (S * n_pages)), *([slc_t] * (S * n_pages)), win_t)


DN_TB = 2 * DN_CHUNK
HALO = 8


def _softplus(x):
    return jnp.maximum(x, 0.0) + jnp.log(1.0 + jnp.exp(-jnp.abs(x)))


def _dn_prompt_kernel(zq_ref, zk_ref, zv_ref, zz_ref, zs_ref, cw_ref, a_ref, dtb_ref, gain_ref, o_ref, s_out_ref,
                      s_ref, buf_ref, *, n_step, col_b, col_a):
    t = pl.program_id(1)
    TB, C = DN_TB, DN_CHUNK

    @pl.when(t == 0)
    def _():
        s_ref[...] = jnp.zeros_like(s_ref)
        buf_ref[0:HALO, :] = jnp.zeros((HALO, DN_QKV), F32)

    @pl.when(t > 0)
    def _():
        buf_ref[0:HALO, :] = buf_ref[TB:TB + HALO, :]

    buf_ref[HALO:HALO + TB, 0:DN_QK] = zq_ref[0]
    buf_ref[HALO:HALO + TB, DN_QK:2 * DN_QK] = zk_ref[0]
    buf_ref[HALO:HALO + TB, 2 * DN_QK:] = zv_ref[0]
    cw = cw_ref[...]
    first = HALO - (CONV_W - 1)
    y = buf_ref[pl.ds(first, TB), :] * cw[0:1, :]
    for j in range(1, CONV_W):
        y += buf_ref[pl.ds(first + j, TB), :] * cw[j:j + 1, :]
    act = _silu(y)

    zs = zs_ref[0]
    beta_all = jax.nn.sigmoid(zs)
    g_all = -a_ref[...] * _softplus(zs + dtb_ref[...])
    zz = zz_ref[0]
    gain = gain_ref[...]

    ii = _iota((C, C), 0)
    jj = _iota((C, C), 1)
    incl = ii >= jj
    eye = (ii == jj).astype(F32)
    n_sub = TB // C
    ch = []
    for sc in range(n_sub):
        rows = slice(sc * C, (sc + 1) * C)
        for h in range(DN_HEADS):
            cs = slice(h * DN_DK, (h + 1) * DN_DK)
            q = _l2n(act[rows, cs]) * (DN_DK ** -0.5)
            k = _l2n(act[rows, DN_QK + h * DN_DK:DN_QK + (h + 1) * DN_DK])
            v = act[rows, 2 * DN_QK + h * DN_DV:2 * DN_QK + (h + 1) * DN_DV]
            gcol = g_all[rows, col_a + h:col_a + h + 1]
            bcol = beta_all[rows, col_b + h:col_b + h + 1]
            grow = jnp.sum(jnp.where(ii <= jj, gcol, 0.0), axis=0, keepdims=True)
            gcum = jnp.sum(jnp.where(ii == jj, grow, 0.0), axis=1, keepdims=True)
            decay = jnp.where(incl, jnp.exp(jnp.where(incl, gcum - grow, 0.0)), 0.0)
            kb = k.astype(BF16)
            ch.append(dict(q=q, k=k, v=v, kb=kb, bcol=bcol, gcum=gcum, decay=decay, sc=sc, h=h, rows=rows, cs=cs))
    for c in ch:
        c["npow"] = -jnp.where(ii > jj, c["bcol"] * _dot_nt(c["kb"], c["kb"]) * c["decay"], 0.0)
        c["tinv"] = eye + c["npow"]
    for _ in range(int(np.log2(C)) - 1):
        for c in ch:
            nb = c["npow"].astype(BF16)
            c["npow"] = _dot(nb, nb)
        for c in ch:
            c["tinv"] = c["tinv"] + _dot(c["tinv"].astype(BF16), c["npow"].astype(BF16))
    for c in ch:
        tb = c["tinv"].astype(BF16)
        c["eg"] = jnp.exp(c["gcum"])
        c["u"] = _dot(tb, (c["v"] * c["bcol"]).astype(BF16))
        c["w"] = _dot(tb, (c["k"] * (c["bcol"] * c["eg"])).astype(BF16)).astype(BF16)
        c["attn"] = (_dot_nt(c["q"].astype(BF16), c["kb"]) * c["decay"]).astype(BF16)
    for sc in range(n_sub):
        cur = [c for c in ch if c["sc"] == sc]
        Ss = [s_ref[c["h"]] for c in cur]
        Sbs = [S.astype(BF16) for S in Ss]
        vns = [c["u"] - _dot(c["w"], Sb) for c, Sb in zip(cur, Sbs)]
        for c, S, Sb, v_new in zip(cur, Ss, Sbs, vns):
            vnb = v_new.astype(BF16)
            o = _dot((c["q"] * c["eg"]).astype(BF16), Sb) + _dot(c["attn"], vnb)
            g_last = c["gcum"][C - 1:C, :]
            s_ref[c["h"]] = S * jnp.exp(g_last) + _dot_tn((c["k"] * jnp.exp(g_last - c["gcum"])).astype(BF16), vnb)
            o_ref[0, c["rows"], c["cs"]] = (_rms(o, gain) * _silu(zz[c["rows"], c["cs"]])).astype(o_ref.dtype)

    @pl.when(t == n_step - 1)
    def _():
        s_out_ref[0] = s_ref[...]


def _dn_prompt(z, conv_w, a_log, dt_bias, out_gain, offs):
    B, T, _ = z.shape
    n_step = T // DN_TB
    wide = lambda name, k=0: pl.BlockSpec((1, DN_TB, DN_QK), lambda b, t, c=offs[name] // DN_QK + k: (b, t, c))
    assert offs["d_qkv"] % DN_QK == 0 and offs["d_z"] % DN_V == 0 and offs["a_g"] % LANE == 0
    col_b, col_a = offs["d_b"] - offs["a_g"], offs["d_a"] - offs["a_g"]
    lanes = lambda vals, col: jnp.zeros((1, LANE), F32).at[0, col:col + DN_HEADS].set(vals)
    full = lambda s: pl.BlockSpec(s, lambda b, t: (0,) * len(s))
    return pl.pallas_call(
        functools.partial(_dn_prompt_kernel, n_step=n_step, col_b=col_b, col_a=col_a),
        grid=(B, n_step),
        in_specs=[wide("d_qkv", 0), wide("d_qkv", 1), wide("d_qkv", 2), wide("d_z"),
                  pl.BlockSpec((1, DN_TB, LANE), lambda b, t, c=offs["a_g"] // LANE: (b, t, c)),
                  full((CONV_W, DN_QKV)), full((1, LANE)), full((1, LANE)), full((1, DN_DV))],
        out_specs=[pl.BlockSpec((1, DN_TB, DN_V), lambda b, t: (b, t, 0)),
                   pl.BlockSpec((1, DN_HEADS, DN_DK, DN_DV), lambda b, t: (b, 0, 0, 0))],
        out_shape=[jax.ShapeDtypeStruct((B, T, DN_V), BF16), jax.ShapeDtypeStruct((B, DN_HEADS, DN_DK, DN_DV), F32)],
        scratch_shapes=[pltpu.VMEM((DN_HEADS, DN_DK, DN_DV), F32), pltpu.VMEM((HALO + DN_TB, DN_QKV), F32)],
        compiler_params=_cparams(("parallel", "arbitrary"), 32),
        name="dn_prompt",
    )(z, z, z, z, z, conv_w, lanes(jnp.exp(a_log), col_a), lanes(dt_bias, col_a), out_gain.reshape(1, DN_DV))


def _row0(x):
    return jnp.where(_iota(x.shape, 0) == 0, x, 0.0)


def _dn_decode_kernel(vec_ref, s_ref, o_ref, s_out_ref, *, bb):
    def body(b, _):
        for h in range(DN_HEADS):
            x = vec_ref[b, h]
            S = s_ref[b, h]
            xs = _dot(x.astype(BF16), S.astype(BF16))
            k, q, v, eg, beta = x[0:1], x[1:2], x[2:3], x[3:4], x[4:5]
            v_new = beta * (v - eg * xs[0:1])
            kb = k.astype(BF16).astype(F32)
            qk = jnp.sum(q.astype(BF16).astype(F32) * kb, axis=1, keepdims=True)
            o = eg * xs[1:2] + qk.astype(BF16).astype(F32) * v_new.astype(BF16).astype(F32)
            o_ref[b, h] = jnp.broadcast_to(o, (8, DN_DV))
            s_out_ref[b, h] = S * eg[:, 0:1] + _dot_tn(_row0(x).astype(BF16),
                                                      _row0(jnp.broadcast_to(v_new, (8, DN_DV))).astype(BF16))
        return 0

    lax.fori_loop(0, bb, body, 0)


def _state_step(kernel_fn, name, vec, S_all, l, bb=8):
    B, H = vec.shape[:2]
    spec_v = pl.BlockSpec((bb, H, 8, vec.shape[3]), lambda i: (i, 0, 0, 0))
    spec_s = pl.BlockSpec((bb, H) + S_all.shape[3:], lambda i: (i, 0, 0, 0))
    spec_sl = pl.BlockSpec((None, bb, H) + S_all.shape[3:], lambda i: (l, i, 0, 0, 0))
    return pl.pallas_call(
        functools.partial(kernel_fn, bb=bb),
        grid=(B // bb,),
        in_specs=[spec_v, spec_sl],
        out_specs=[spec_v, spec_s],
        out_shape=[jax.ShapeDtypeStruct(vec.shape, F32), jax.ShapeDtypeStruct(S_all.shape[1:], F32)],
        compiler_params=_cparams(("parallel",), 32),
        name=name,
    )(vec, S_all)


def _dn_decode(vec, S_all, l):
    return _state_step(_dn_decode_kernel, "dn_decode", vec, S_all, l)


HG_TB = 128
HG_LEVELS = tuple(HG_TB >> (i + 1) for i in range(int(np.log2(HG_TB))))


def _hg_tables():
    r = np.arange(HG_TB)
    tril = (r[:, None] >= r[None, :]).astype(np.float32)
    mats = [tril] + [tril[(r // (2 * w)) * 2 * w + w - 1] for w in HG_LEVELS]
    level = np.full((HG_TB, HG_TB), -1, np.int32)
    for i, w in enumerate(HG_LEVELS):
        same_block = r[:, None] // (2 * w) == r[None, :] // (2 * w)
        split = (r[:, None] % (2 * w) >= w) & (r[None, :] % (2 * w) < w)
        level[same_block & split] = i
    return np.concatenate(mats, axis=0), level


def _hg_prompt_kernel(zf_ref, zi_ref, zq_ref, zo_ref, lb_ref, gain_ref, sel_ref, lvl_ref, o_ref, s_out_ref, st_ref,
                      *, n_step):
    t = pl.program_id(1)
    TB = HG_TB

    @pl.when(t == 0)
    def _():
        st_ref[...] = jnp.zeros_like(st_ref)

    zf = zf_ref[0]
    lb = lb_ref[...]
    lf = jnp.log(lb + (1.0 - lb) * jax.nn.sigmoid(zf))
    k = (1.0 - lb) * jax.nn.sigmoid(-zf)
    q = _silu(zq_ref[0]) * (HG_DK ** -0.5)
    vb = zi_ref[0].astype(BF16)
    gain = gain_ref[...]
    hi = lf.astype(BF16)
    r1 = lf - hi.astype(F32)
    mid = r1.astype(BF16)
    lo = (r1 - mid.astype(F32)).astype(BF16)
    sel = sel_ref[...]
    gg = _dot(sel, hi) + (_dot(sel, mid) + _dot(sel, lo))
    G = gg[0:TB]
    row = _iota((TB, 1), 0)
    qts, kts = [], []
    for i, w in enumerate(HG_LEVELS):
        d = G - gg[(i + 1) * TB:(i + 2) * TB]
        right = (row % (2 * w)) >= w
        qts.append(jnp.where(right, q * jnp.exp(jnp.minimum(d, 0.0)), 0.0).astype(BF16))
        kts.append(jnp.where(right, 0.0, k * jnp.exp(jnp.minimum(-d, 0.0))).astype(BF16))
    lvl = lvl_ref[...]
    eye = _iota((TB, TB), 0) == _iota((TB, TB), 1)
    qe = (q * jnp.exp(G)).astype(BF16)
    g_last = G[TB - 1:TB, :]
    kd = (k * jnp.exp(g_last - G)).astype(BF16)
    eg_last = jnp.exp(g_last)
    qk = q * k
    heads = [slice(h * HG_DK, (h + 1) * HG_DK) for h in range(HG_HEADS)]
    sts = [st_ref[h] for h in range(HG_HEADS)]
    o_inter = [_dot_nt(qe[:, cs], st.astype(BF16)) for cs, st in zip(heads, sts)]
    kv = [_dot_tn(vb[:, cs], kd[:, cs]) for cs in heads]
    a = [jnp.where(eye, jnp.sum(qk[:, cs], axis=1, keepdims=True), 0.0) for cs in heads]
    for i in range(len(HG_LEVELS)):
        parts = [_dot_nt(qts[i][:, cs], kts[i][:, cs]) for cs in heads]
        a = [a_h + jnp.where(lvl == i, p, 0.0) for a_h, p in zip(a, parts)]
    o_intra = [_dot(a_h.astype(BF16), vb[:, cs]) for a_h, cs in zip(a, heads)]
    for h, cs in enumerate(heads):
        o = o_inter[h] + o_intra[h]
        o_ref[0, :, cs] = (_rms(o, gain) * jax.nn.sigmoid(zo_ref[0, :, cs])).astype(o_ref.dtype)
        st_ref[h] = sts[h] * eg_last[:, cs] + kv[h]

    @pl.when(t == n_step - 1)
    def _():
        for h in range(HG_HEADS):
            s_out_ref[0, h] = st_ref[h].T


def _hg_prompt(z, lb, out_gain, offs):
    B, T, _ = z.shape
    tb = HG_TB
    n_step = T // tb
    sel, level = _hg_tables()
    assert all(offs[n] % HG_WK == 0 for n in ("r_f", "r_i", "r_q", "r_og"))
    col = lambda name: pl.BlockSpec((1, tb, HG_WK), lambda b, t, c=offs[name] // HG_WK: (b, t, c))
    full = lambda s: pl.BlockSpec(s, lambda b, t: (0,) * len(s))
    return pl.pallas_call(
        functools.partial(_hg_prompt_kernel, n_step=n_step),
        grid=(B, n_step),
        in_specs=[col("r_f"), col("r_i"), col("r_q"), col("r_og"), full((1, HG_WK)), full((1, HG_DV)),
                  full(sel.shape), full(level.shape)],
        out_specs=[pl.BlockSpec((1, tb, HG_WV), lambda b, t: (b, t, 0)),
                   pl.BlockSpec((1, HG_HEADS, HG_DK, HG_DV), lambda b, t: (b, 0, 0, 0))],
        out_shape=[jax.ShapeDtypeStruct((B, T, HG_WV), BF16), jax.ShapeDtypeStruct((B, HG_HEADS, HG_DK, HG_DV), F32)],
        scratch_shapes=[pltpu.VMEM((HG_HEADS, HG_DV, HG_DK), F32)],
        compiler_params=_cparams(("parallel", "arbitrary"), 32),
        name="hg_prompt",
    )(z, z, z, z, lb.reshape(1, HG_WK), out_gain.reshape(1, HG_DV), jnp.asarray(sel, BF16), jnp.asarray(level))


def _hg_decode_kernel(vec_ref, s_ref, o_ref, s_out_ref, *, bb):
    ii = _iota((HG_DK, HG_DK), 0)
    jj = _iota((HG_DK, HG_DK), 1)

    def body(b, _):
        for h in range(HG_HEADS):
            x = vec_ref[b, h]
            S = s_ref[b, h]
            k, q, v, lf = x[0:1], x[1:2], x[2:3], x[3:4]
            f = jnp.exp(lf)
            qs = _dot(jnp.broadcast_to(q * f, (8, HG_DK)).astype(BF16), S.astype(BF16))
            a = jnp.sum(q * k, axis=1, keepdims=True)
            o = qs[0:1] + a.astype(BF16).astype(F32) * v.astype(BF16).astype(F32)
            o_ref[b, h] = jnp.broadcast_to(o, (8, HG_DV))
            fcol = jnp.sum(jnp.where(ii == jj, f, 0.0), axis=1, keepdims=True)
            s_out_ref[b, h] = S * fcol + _dot_tn(_row0(x).astype(BF16),
                                                 _row0(jnp.broadcast_to(v, (8, HG_DV))).astype(BF16))
        return 0

    lax.fori_loop(0, bb, body, 0)


def _hg_decode(vec, S_all, l):
    return _state_step(_hg_decode_kernel, "hg_decode", vec, S_all, l)


def _head_rms(x, g):
    return x * lax.rsqrt(jnp.mean(x * x, axis=-1, keepdims=True) + NORM_EPS) * g


def _l2n(x):
    return x * lax.rsqrt(jnp.sum(x * x, axis=-1, keepdims=True) + L2_EPS)


def _rows8(rows):
    x = jnp.stack(rows, axis=-2)
    pad = [(0, 0)] * x.ndim
    pad[-2] = (0, 8 - len(rows))
    return jnp.pad(x, pad)


def _layer(x, l, prm, wts, past, page_table):
    B, T, D = x.shape
    M = B * T
    tm = min(M, ROW_TILE)
    tm_w = min(M, ROW_TILE_STREAMED)
    x2 = x.reshape(M, D)
    x2 = _ffn(x2, prm["ffn1_norm"][l], wts["ffn1_w_gu"][l], wts["ffn1_w_down"][l], tm_w)
    z, mgates = _inproj(x2, prm["mix_norm"][l], wts["w_in"][l], min(M, ROW_TILE_INPROJ))
    sizes, _, offs, _ = _z_layout()
    pee, w1e, w2e = wts["cmp"][l]
    kg = prm["nsa_k_norm"][l]
    p = jax.nn.softmax(prm["hg_lb_logits"], axis=0)
    lb = (jnp.cumsum(p, axis=0) - p[0])[l]
    if past is None:
        z3 = z.reshape(B, T, -1)
        qt, cmp_new, slc_new, win_new, sk, svt, wk, wvt, gt = _nsa_prep(z3, prm["nsa_q_norm"][l], kg, offs)
        ck, cvt = _cmp_prompt(cmp_new, pee, w1e, w2e, kg[0])
        o_a = _nsa_prompt(qt, ck, cvt, sk, svt, wk, wvt, gt)
        o_d, dn_state = _dn_prompt(z3, prm["dn_conv_w"][l], prm["dn_A_log"][l], prm["dn_dt_bias"][l],
                                   prm["dn_out_norm"][l], offs)
        o_h, hg_state = _hg_prompt(z3, lb, prm["hg_out_norm"][l], offs)
        rows = lambda a: a.reshape(B, -1, 2, NSA_KV, HEAD_DIM)
        cmp_new, slc_new, win_state = rows(cmp_new), rows(slc_new), rows(win_new[:, T - min(WINDOW, T):])
        conv_state = z3[:, T - (CONV_W - 1):, offs["d_qkv"]:offs["d_qkv"] + DN_QKV]
    else:
        zs = {n: z[:, offs[n]:offs[n] + sizes[n]].reshape(B, T, sizes[n]) for n in Z_ORDER if n != "m_g"}
        (o_a, o_d, o_h), (cmp_new, slc_new, win_state, conv_state, dn_state, hg_state) = _decode_mixers(
            zs, l, prm, (pee, w1e, w2e), lb, past, page_table)

    x2 = _merge(x2, o_a.reshape(M, NSA_W), o_d.reshape(M, DN_V), o_h.reshape(M, HG_WV), mgates, wts["w_branch"][l],
                wts["w_out"][l], tm)
    x2 = _ffn(x2, prm["ffn2_norm"][l], wts["ffn2_w_gu"][l], wts["ffn2_w_down"][l], tm_w)
    return x2.reshape(B, T, D), (cmp_new, slc_new, win_state, conv_state, dn_state, hg_state)


def _decode_mixers(zs, l, prm, cmp_w, lb, past, page_table):
    B, T = zs["a_q"].shape[:2]
    pee, w1e, w2e = cmp_w
    kg = prm["nsa_k_norm"][l]
    q = _head_rms(zs["a_q"].reshape(B, T, NSA_HEADS, HEAD_DIM), prm["nsa_q_norm"][l]) * (HEAD_DIM ** -0.5)
    kv = zs["a_kv"].reshape(B, T, 3, 2, NSA_KV, HEAD_DIM)
    cmp_new = kv[:, :, 0]
    slc_new = jnp.stack([_head_rms(kv[:, :, 1, 0], kg[1]), kv[:, :, 1, 1]], axis=2)
    win_new = jnp.stack([_head_rms(kv[:, :, 2, 0], kg[2]), kv[:, :, 2, 1]], axis=2)
    gates = jax.nn.sigmoid(zs["a_g"].reshape(B, T, NSA_HEADS, 3))
    qd = q.reshape(B, NSA_KV, NSA_GROUP, HEAD_DIM)
    qd = jnp.pad(qd, ((0, 0), (0, 0), (0, 8 - NSA_GROUP), (0, 0))).astype(BF16)
    new_rows = _rows8([cmp_new.reshape(B, ROW_W), slc_new.reshape(B, ROW_W), win_new.reshape(B, ROW_W)])
    gd = gates.reshape(B, NSA_KV, NSA_GROUP, 3)
    gd = jnp.pad(gd, ((0, 0), (0, 0), (0, 8 - NSA_GROUP), (0, LANE - 3)))
    o8 = _nsa_decode(page_table, l, qd, new_rows, gd, past["cmp"], past["slc"], past["win"], pee, w1e, w2e, kg[0])
    o_a = o8[:, :, :NSA_GROUP].reshape(B, T, NSA_W)

    d_qkv = zs["d_qkv"]
    xx = jnp.concatenate([past["conv"][l], d_qkv], axis=1)
    cw = prm["dn_conv_w"][l]
    qkv = sum(xx[:, j:j + T] * cw[j] for j in range(CONV_W))
    conv_state = xx[:, -(CONV_W - 1):]
    dq, dk, dv = jnp.split(jax.nn.silu(qkv), [DN_QK, 2 * DN_QK], axis=-1)
    dq = _l2n(dq.reshape(B, T, DN_HEADS, DN_DK)) * (DN_DK ** -0.5)
    dk = _l2n(dk.reshape(B, T, DN_HEADS, DN_DK))
    beta = jax.nn.sigmoid(zs["d_b"])
    g = -jnp.exp(prm["dn_A_log"][l]) * jax.nn.softplus(zs["d_a"] + prm["dn_dt_bias"][l])
    lanes = lambda a: jnp.broadcast_to(a[:, 0, :, None], (B, DN_HEADS, DN_DK))
    vec = _rows8([dk[:, 0], dq[:, 0], dv.reshape(B, DN_HEADS, DN_DV), lanes(jnp.exp(g)), lanes(beta)])
    o8, dn_state = _dn_decode(vec, past["dn_S"], l)
    o_d = o8[:, :, 0].reshape(B, T, DN_V)
    o_d = _head_rms(o_d.reshape(B, T, DN_HEADS, DN_DV), prm["dn_out_norm"][l]) * jax.nn.silu(
        zs["d_z"].reshape(B, T, DN_HEADS, DN_DV))

    zf = zs["r_f"]
    logf = jnp.log(lb + (1.0 - lb) * jax.nn.sigmoid(zf))
    k_in = (1.0 - lb) * jax.nn.sigmoid(-zf)
    hq = jax.nn.silu(zs["r_q"]) * (HG_DK ** -0.5)
    hd = lambda a: a.reshape(B, HG_HEADS, HG_DK)
    vec = _rows8([hd(k_in), hd(hq), hd(zs["r_i"]), hd(logf)])
    o8, hg_state = _hg_decode(vec, past["hg_S"], l)
    o_h = o8[:, :, 0].reshape(B, T, HG_WV)
    o_h = _head_rms(o_h.reshape(B, T, HG_HEADS, HG_DV), prm["hg_out_norm"][l]) * jax.nn.sigmoid(
        zs["r_og"].reshape(B, T, HG_HEADS, HG_DV))
    return (o_a, o_d, o_h), (cmp_new, slc_new, win_new, conv_state, dn_state, hg_state)


def _trunk(x, prm, wts, caches, page_table):
    new = []
    for l in range(DEPTH):
        x, st = _layer(x, l, prm, wts, caches, page_table)
        new.append(st)
    return x, [jnp.stack([s[i] for s in new], axis=0) for i in range(6)]


def kernel(x_prompt, x_sample, cache_cmp_kv, cache_slc_kv, cache_win_kv, state_dn_conv, state_dn_S, state_hg_S,
           page_table, ffn1_norm, ffn1_w_gu, ffn1_w_down, mix_norm, w_in, nsa_q_norm, nsa_k_norm, nsa_cmp_pe,
           nsa_cmp_w1, nsa_cmp_w2, dn_conv_w, dn_A_log, dn_dt_bias, dn_out_norm, hg_lb_logits, hg_out_norm,
           w_branch, w_out, ffn2_norm, ffn2_w_gu, ffn2_w_down):
    prm = dict(ffn1_norm=ffn1_norm, mix_norm=mix_norm, nsa_q_norm=nsa_q_norm, nsa_k_norm=nsa_k_norm,
               dn_conv_w=dn_conv_w, dn_A_log=dn_A_log, dn_dt_bias=dn_dt_bias, dn_out_norm=dn_out_norm,
               hg_lb_logits=hg_lb_logits, hg_out_norm=hg_out_norm, ffn2_norm=ffn2_norm)
    bf = lambda w: w.astype(BF16)
    wts = dict(ffn1_w_gu=bf(ffn1_w_gu), ffn1_w_down=bf(ffn1_w_down), ffn2_w_gu=bf(ffn2_w_gu),
               ffn2_w_down=bf(ffn2_w_down), w_branch=bf(w_branch), w_out=bf(w_out),
               w_in=jnp.stack([_permute_w_in(w_in[l]) for l in range(DEPTH)]),
               cmp=[_cmp_weights(nsa_cmp_pe[l], nsa_cmp_w1[l], nsa_cmp_w2[l]) for l in range(DEPTH)])
    y_p, (p_cmp, p_slc, p_win, p_conv, p_dn, p_hg) = _trunk(x_prompt, prm, wts, None, None)
    caches = dict(cmp=_rows_minor(cache_cmp_kv), slc=_rows_minor(cache_slc_kv), win=_rows_minor(cache_win_kv),
                  conv=state_dn_conv, dn_S=state_dn_S, hg_S=state_hg_S)
    y_s, (s_cmp, s_slc, s_win, s_conv, s_dn, s_hg) = _trunk(x_sample, prm, wts, caches, page_table)
    return (y_p, y_s, p_cmp, s_cmp, p_slc, s_slc, p_win, s_win, p_conv, s_conv, p_dn, s_dn, p_hg, s_hg)
```

```python
import functools

import jax
import jax.numpy as jnp
import numpy as np
from jax import lax
from jax.experimental import pallas as pl
from jax.experimental.pallas import tpu as pltpu

F32 = jnp.float32
BF16 = jnp.bfloat16

D_MODEL = 1024
DEPTH = 2
PAGE_SIZE = 128
HEAD_DIM = 64
NSA_HEADS = 8
NSA_KV = 2
NSA_GROUP = NSA_HEADS // NSA_KV
CMP_LEN = 32
CMP_STRIDE = 16
CMP_HIDDEN = 128
SLC_BLOCK = 64
N_SEL = 16
WINDOW = 512
Q_BLOCK = 128
DN_HEADS = 4
DN_DK = 128
DN_DV = 128
DN_CHUNK = 64
CONV_W = 4
HG_HEADS = 4
HG_DK = 128
HG_DV = 128
NORM_EPS = 1e-6
L2_EPS = 1e-6
NEG_BIG = -1e30
SEL_BIG = 1e9

NSA_W = NSA_HEADS * HEAD_DIM
KV_W = NSA_KV * HEAD_DIM
ROW_W = 2 * KV_W
DN_QK = DN_HEADS * DN_DK
DN_V = DN_HEADS * DN_DV
DN_QKV = 2 * DN_QK + DN_V
HG_WK = HG_HEADS * HG_DK
HG_WV = HG_HEADS * HG_DV
MIX_W = NSA_W + DN_V + HG_WV
IN_SPLITS = (NSA_W, 6 * KV_W, 3 * NSA_HEADS, DN_QKV, DN_HEADS, DN_HEADS, DN_V, HG_WK, HG_WV, HG_WK, HG_WV, 3 * D_MODEL)
IN_NAMES = ("a_q", "a_kv", "a_g", "d_qkv", "d_b", "d_a", "d_z", "r_f", "r_i", "r_q", "r_og", "m_g")
Z_ORDER = ("d_qkv", "d_z", "r_f", "r_i", "r_q", "r_og", "a_q", "a_kv", "a_g", "d_b", "d_a", "m_g")
Z_TN = 512
CHUNK_W = CMP_STRIDE * ROW_W
CMP_HID_W = 2 * NSA_KV * CMP_HIDDEN

V7X_VMEM_BYTES = 64 * 1024 * 1024
LANE = 128
ROW_TILE = 512
ROW_TILE_STREAMED = 1024
ROW_TILE_INPROJ = 2048


def _cparams(sem, vmem_mb):
    assert vmem_mb * 1024 * 1024 < V7X_VMEM_BYTES
    return pltpu.CompilerParams(dimension_semantics=sem, vmem_limit_bytes=vmem_mb * 1024 * 1024)


def _dot(a, b):
    return jnp.dot(a, b, preferred_element_type=F32)


def _dot_nt(a, b):
    return lax.dot_general(a, b, (((1,), (1,)), ((), ())), preferred_element_type=F32)


def _dot_tn(a, b):
    return lax.dot_general(a, b, (((0,), (0,)), ((), ())), preferred_element_type=F32)


def _split2(a):
    hi = a.astype(BF16)
    lo = (a - hi.astype(F32)).astype(BF16)
    return hi, lo


def _rms(x, g):
    return x * lax.rsqrt(jnp.mean(x * x, axis=-1, keepdims=True) + NORM_EPS) * g


def _silu(x):
    return x * jax.nn.sigmoid(x)


def _iota(shape, dim):
    return lax.broadcasted_iota(jnp.int32, shape, dim)


def _alibi_slope(head_in_group, g):
    out = jnp.full(head_in_group.shape, 2.0 ** -(NSA_GROUP * g + NSA_GROUP), F32)
    for j in range(NSA_GROUP - 2, -1, -1):
        out = jnp.where(head_in_group == j, 2.0 ** -(NSA_GROUP * g + j + 1), out)
    return out


def _ffn_kernel(x_ref, g_ref, wg_ref, wu_ref, wd_ref, o_ref, xn_ref, acc_ref, *, nf):
    j = pl.program_id(1)

    @pl.when(j == 0)
    def _():
        xn_ref[...] = _rms(x_ref[...], g_ref[...]).astype(BF16)
        acc_ref[...] = jnp.zeros_like(acc_ref)

    xn = xn_ref[...]
    tf = wg_ref.shape[1]
    for c0 in range(0, tf, FFN_CHUNK):
        c1 = min(c0 + FFN_CHUNK, tf)
        a = _silu(_dot(xn, wg_ref[:, c0:c1])) * _dot(xn, wu_ref[:, c0:c1])
        acc_ref[...] += _dot(a.astype(BF16), wd_ref[c0:c1, :])

    @pl.when(j == nf - 1)
    def _():
        o_ref[...] = x_ref[...] + 0.5 * acc_ref[...]


FFN_CHUNK = 256


def _ffn(x, gain, w_gu, w_down, tm):
    M, D = x.shape
    F = w_down.shape[0]
    tf = F // 2 if F % (2 * LANE) == 0 else F
    nf = F // tf
    return pl.pallas_call(
        functools.partial(_ffn_kernel, nf=nf),
        grid=(M // tm, nf),
        in_specs=[
            pl.BlockSpec((tm, D), lambda i, j: (i, 0)),
            pl.BlockSpec((1, D), lambda i, j: (0, 0)),
            pl.BlockSpec((D, tf), lambda i, j: (0, j)),
            pl.BlockSpec((D, tf), lambda i, j: (0, j + nf)),
            pl.BlockSpec((tf, D), lambda i, j: (j, 0)),
        ],
        out_specs=pl.BlockSpec((tm, D), lambda i, j: (i, 0)),
        out_shape=jax.ShapeDtypeStruct((M, D), F32),
        scratch_shapes=[pltpu.VMEM((tm, D), BF16), pltpu.VMEM((tm, D), F32)],
        compiler_params=_cparams(("parallel", "arbitrary"), 52),
        name="ffn",
    )(x, gain.reshape(1, D), w_gu, w_gu, w_down)


def _inproj_kernel(x_ref, g_ref, w_ref, z_ref, gate_ref, xn_ref, *, nz):
    j = pl.program_id(1)

    @pl.when(j == 0)
    def _():
        xn_ref[...] = _rms(x_ref[...], g_ref[...]).astype(BF16)

    acc = _dot(xn_ref[...], w_ref[...])

    @pl.when(j < nz)
    def _():
        z_ref[...] = acc

    @pl.when(j >= nz)
    def _():
        gate_ref[...] = jax.nn.sigmoid(acc).astype(gate_ref.dtype)


def _inproj(x, gain, w, tm):
    M, D = x.shape
    N = w.shape[1]
    z_width = _z_layout()[2]["m_g"]
    nz = z_width // Z_TN
    return pl.pallas_call(
        functools.partial(_inproj_kernel, nz=nz),
        grid=(M // tm, N // Z_TN),
        in_specs=[
            pl.BlockSpec((tm, D), lambda i, j: (i, 0)),
            pl.BlockSpec((1, D), lambda i, j: (0, 0)),
            pl.BlockSpec((D, Z_TN), lambda i, j: (0, j)),
        ],
        out_specs=[pl.BlockSpec((tm, Z_TN), lambda i, j: (i, jnp.minimum(j, nz - 1))),
                   pl.BlockSpec((tm, Z_TN), lambda i, j: (i, jnp.maximum(j - nz, 0)))],
        out_shape=[jax.ShapeDtypeStruct((M, z_width), F32), jax.ShapeDtypeStruct((M, N - z_width), BF16)],
        scratch_shapes=[pltpu.VMEM((tm, D), BF16)],
        compiler_params=_cparams(("parallel", "arbitrary"), 48),
        name="inproj",
    )(x, gain.reshape(1, D), w)


def _z_layout():
    sizes = dict(zip(IN_NAMES, IN_SPLITS))
    src = dict(zip(IN_NAMES, np.cumsum((0,) + IN_SPLITS[:-1]).tolist()))
    offs, o = {}, 0
    for n in Z_ORDER:
        if n == "m_g":
            o = -(-o // Z_TN) * Z_TN
        offs[n] = o
        o += sizes[n]
    assert Z_ORDER[-1] == "m_g" and o % Z_TN == 0
    return sizes, src, offs, o


def _permute_w_in(w_in):
    sizes, src, offs, total = _z_layout()
    w = jnp.zeros((w_in.shape[0], total), w_in.dtype)
    for n in Z_ORDER:
        w = lax.dynamic_update_slice(w, w_in[:, src[n]:src[n] + sizes[n]], (0, offs[n]))
    return w.astype(BF16)


def _merge_kernel(x_ref, oa_ref, od_ref, oh_ref, g0_ref, g1_ref, g2_ref, wb_ref, wo_ref, o_ref):
    m = g0_ref[...].astype(F32) * _dot(oa_ref[...].astype(BF16), wb_ref[0:NSA_W, :])
    m += g1_ref[...].astype(F32) * _dot(od_ref[...].astype(BF16), wb_ref[NSA_W:NSA_W + DN_V, :])
    m += g2_ref[...].astype(F32) * _dot(oh_ref[...].astype(BF16), wb_ref[NSA_W + DN_V:MIX_W, :])
    o_ref[...] = x_ref[...] + _dot(m.astype(BF16), wo_ref[...])


def _merge(x, o_a, o_d, o_h, z, w_branch, w_out, tm):
    M, D = x.shape
    row = lambda w: pl.BlockSpec((tm, w), lambda i: (i, 0))
    return pl.pallas_call(
        _merge_kernel,
        grid=(M // tm,),
        in_specs=[
            row(D), row(NSA_W), row(DN_V), row(HG_WV),
            pl.BlockSpec((tm, D), lambda i: (i, 0)),
            pl.BlockSpec((tm, D), lambda i: (i, 1)),
            pl.BlockSpec((tm, D), lambda i: (i, 2)),
            pl.BlockSpec((MIX_W, D), lambda i: (0, 0)),
            pl.BlockSpec((D, D), lambda i: (0, 0)),
        ],
        out_specs=row(D),
        out_shape=jax.ShapeDtypeStruct((M, D), F32),
        compiler_params=_cparams(("parallel",), 40),
        name="merge",
    )(x, o_a, o_d, o_h, z, z, z, w_branch, w_out)


def _cmp_weights(pe, w1, w2):
    n_part = CMP_LEN // CMP_STRIDE
    eye = jnp.eye(NSA_KV, dtype=F32)
    eye2 = jnp.eye(2, dtype=F32)
    w1r = w1.reshape(2, n_part, CMP_STRIDE, HEAD_DIM, CMP_HIDDEN)
    w1e = jnp.einsum("kmrdh,kK,gG->mrkgdKGh", w1r, eye2, eye).reshape(n_part, CHUNK_W, CMP_HID_W)
    w2e = jnp.einsum("khd,kK,gG->kghKGd", w2, eye2, eye).reshape(CMP_HID_W, ROW_W)
    per = pe.reshape(2, n_part, CMP_STRIDE, HEAD_DIM).transpose(1, 2, 0, 3)
    pee = jnp.broadcast_to(per[:, :, :, None, :], (n_part, CMP_STRIDE, 2, NSA_KV, HEAD_DIM)).reshape(n_part, CHUNK_W)
    return pee, w1e.astype(BF16), w2e.astype(BF16)


def _compress(x, xnext_rows, pe_ref, w1_ref, w2_ref, kg_ref, n_seq=1):
    rows = x.shape[0]
    n = rows // n_seq
    p0 = _dot((x + pe_ref[0:1, :]).astype(BF16), w1_ref[0])
    p1 = _dot((x + pe_ref[1:2, :]).astype(BF16), w1_ref[1])
    p1s = pltpu.roll(p1, rows - 1, 0)
    if xnext_rows is not None:
        p1n = _dot((xnext_rows + pe_ref[1:2, :]).astype(BF16), w1_ref[1])
        for s in range(n_seq):
            p1s = jnp.where(_iota((rows, 1), 0) == (s + 1) * n - 1, p1n[s:s + 1, :], p1s)
    comp = _dot(_silu(p0 + p1s).astype(BF16), w2_ref[...])
    cks = []
    for g in range(NSA_KV):
        kc = comp[:, g * HEAD_DIM:(g + 1) * HEAD_DIM]
        cks.append(_rms(kc, kg_ref[...]))
    return cks, comp[:, KV_W:]


def _cmp_prompt_kernel(x_ref, pe_ref, w1_ref, w2_ref, kg_ref, ck_ref, cvt_ref):
    cks, cv = _compress(x_ref[0], None, pe_ref, w1_ref, w2_ref, kg_ref)
    for g in range(NSA_KV):
        ck_ref[0, g] = cks[g].astype(BF16)
    cvt = cv.T
    for g in range(NSA_KV):
        cvt_ref[0, g] = cvt[g * HEAD_DIM:(g + 1) * HEAD_DIM, :].astype(BF16)


def _cmp_prompt(cmp_rows, pee, w1e, w2e, kg0):
    B, T = cmp_rows.shape[:2]
    nc = T // CMP_STRIDE
    x = cmp_rows.reshape(B, nc, CHUNK_W)
    full = lambda s: pl.BlockSpec(s, lambda b: (0,) * len(s))
    return pl.pallas_call(
        _cmp_prompt_kernel,
        grid=(B,),
        in_specs=[pl.BlockSpec((1, nc, CHUNK_W), lambda b: (b, 0, 0)), full(pee.shape), full(w1e.shape),
                  full(w2e.shape), full((1, HEAD_DIM))],
        out_specs=[pl.BlockSpec((1, NSA_KV, nc, HEAD_DIM), lambda b: (b, 0, 0, 0)),
                   pl.BlockSpec((1, NSA_KV, HEAD_DIM, nc), lambda b: (b, 0, 0, 0))],
        out_shape=[jax.ShapeDtypeStruct((B, NSA_KV, nc, HEAD_DIM), BF16),
                   jax.ShapeDtypeStruct((B, NSA_KV, HEAD_DIM, nc), BF16)],
        compiler_params=_cparams(("parallel",), 48),
        name="nsa_compress",
    )(x, pee, w1e, w2e, kg0.reshape(1, HEAD_DIM))


def _softmax_first(s, vt):
    m = jnp.max(s, axis=0, keepdims=True)
    e = jnp.exp(s - m)
    return m, jnp.sum(e, axis=0, keepdims=True), _dot(vt, e.astype(BF16))


KC_W = 2 * LANE - HEAD_DIM
KC_POS = 64
KC_POS_RADIX = 128
KW = HEAD_DIM + KC_W


def _key_features(T):
    k = np.arange(T)
    f = np.zeros((T, KW), np.float32)
    f[k, HEAD_DIM + k // SLC_BLOCK] = 1.0
    f[:, HEAD_DIM + KC_POS] = k // KC_POS_RADIX
    f[:, HEAD_DIM + KC_POS + 1] = k % KC_POS_RADIX
    return f


def _nsa_prompt_kernel(qt_ref, ck_ref, cvt_ref, sk_ref, svt_ref, wk_ref, wvt_ref, gt_ref, ovl_ref, o_ref,
                       *, n_cmp, n_blk):
    qb = pl.program_id(1)
    b0 = qb * Q_BLOCK
    QW = NSA_GROUP * Q_BLOCK
    TK = Q_BLOCK
    lane = _iota((1, QW), 1)
    tq = lane % Q_BLOCK
    q_pos = b0 + tq
    head = lane // Q_BLOCK
    qp1 = b0 + _iota((1, Q_BLOCK), 1)
    bj = _iota((n_blk, 1), 0)
    krel = _iota((TK, 1), 0) - tq
    frow = _iota((KC_W, 1), 0)
    k_diag = pl.multiple_of(b0, TK)
    gt = gt_ref[0, 0]

    def gate(g, branch):
        return jnp.concatenate([gt[(NSA_GROUP * g + j) * 3 + branch:(NSA_GROUP * g + j) * 3 + branch + 1, :]
                                for j in range(NSA_GROUP)], axis=1)

    qc_slcs, qc_wins, o_cmps = [], [], []
    for g in range(NSA_KV):
        slope = _alibi_slope(head, g)
        qgt = jnp.concatenate([qt_ref[0, NSA_GROUP * g + j] for j in range(NSA_GROUP)], axis=1)
        pos_rows = jnp.where(frow == KC_POS, slope * float(KC_POS_RADIX), jnp.where(frow == KC_POS + 1, slope, 0.0))

        dist = q_pos - (_iota((n_cmp, 1), 0) * CMP_STRIDE + (CMP_LEN - 1))
        mask = dist >= 0
        s = jnp.where(mask, _dot(ck_ref[0, g], qgt) - slope * dist.astype(F32), NEG_BIG)
        m = jnp.max(s, axis=0, keepdims=True)
        e = jnp.where(mask, jnp.exp(s - m), 0.0)
        p = (e / jnp.maximum(jnp.sum(e, axis=0, keepdims=True), 1e-30)).astype(BF16)
        o_cmps.append(_dot(cvt_ref[0, g], p))

        imp = _dot(ovl_ref[...], p[:, 0:Q_BLOCK])
        for j in range(1, NSA_GROUP):
            imp += _dot(ovl_ref[...], p[:, j * Q_BLOCK:(j + 1) * Q_BLOCK])
        cur = qp1 // SLC_BLOCK
        forced = (bj == 0) | (bj == cur) | (bj == cur - 1)
        imp = jnp.where(forced, SEL_BIG, jnp.where(bj * SLC_BLOCK <= qp1, imp, -SEL_BIG))
        tiles = [imp[8 * v:8 * v + 8, :] for v in range(n_blk // 8)]
        ranks = [jnp.zeros((8, Q_BLOCK), F32) for _ in tiles]
        for i in range(n_blk):
            row = imp[i:i + 1, :]
            for v, tile in enumerate(tiles):
                if 8 * v > i:
                    beats = row >= tile
                elif 8 * v + 8 <= i:
                    beats = row > tile
                else:
                    beats = (row > tile) | ((row == tile) & (bj[8 * v:8 * v + 8] > i))
                ranks[v] = ranks[v] + jnp.where(beats, 1.0, 0.0)
        rank = jnp.concatenate(ranks, axis=0)
        selb = jnp.where(rank < float(min(N_SEL, n_blk)), 0.0, NEG_BIG)
        selb = jnp.concatenate([selb] * NSA_GROUP, axis=1)
        if n_blk < KC_W:
            selb = jnp.concatenate([selb, jnp.zeros((KC_W - n_blk, QW), F32)], axis=0)
        qc_slcs.append(jnp.concatenate([qgt, (selb + pos_rows).astype(BF16)], axis=0))
        qc_wins.append(jnp.concatenate([qgt, pos_rows.astype(BF16)], axis=0))

    def scores(k_ref, g, k0, n, qcs):
        return _dot(k_ref[0, g, pl.ds(k0, n), :], qcs[g])

    WS = WINDOW + Q_BLOCK
    ws = pl.multiple_of(jnp.maximum(b0 - WINDOW, 0), TK)
    off = b0 - ws
    drel = tq - _iota((WS, 1), 0)
    in_win = (drel >= -off) & (drel < WINDOW - off)
    o_wins = []
    for g in range(NSA_KV):
        s = jnp.where(in_win, scores(wk_ref, g, ws, WS, qc_wins), NEG_BIG)
        _, l, acc = _softmax_first(s, wvt_ref[0, g, :, pl.ds(ws, WS)])
        o_wins.append(acc / jnp.maximum(l, 1e-30))

    carry = []
    for g in range(NSA_KV):
        s = jnp.where(krel <= 0, scores(sk_ref, g, k_diag, TK, qc_slcs), NEG_BIG)
        carry.append(_softmax_first(s, svt_ref[0, g, :, pl.ds(k_diag, TK)]))

    def slc_body(n):
        def body(i, carry):
            k0 = pl.multiple_of(i * n, n)
            G = range(NSA_KV)
            ss = [scores(sk_ref, g, k0, n, qc_slcs) for g in G]
            ms = [jnp.maximum(carry[g][0], jnp.max(ss[g], axis=0, keepdims=True)) for g in G]
            es = [jnp.exp(ss[g] - ms[g]) for g in G]
            pvs = [_dot(svt_ref[0, g, :, pl.ds(k0, n)], es[g].astype(BF16)) for g in G]
            out = []
            for g in G:
                m, l, acc = carry[g]
                alpha = jnp.exp(m - ms[g])
                out.append((ms[g], alpha * l + jnp.sum(es[g], axis=0, keepdims=True), alpha * acc + pvs[g]))
            return tuple(out)
        return body

    carry = tuple(carry)
    done = 0
    for width in (4, 2, 1):
        trips = (qb - done) // width
        carry = lax.fori_loop(done // width, done // width + trips, slc_body(width * TK), carry)
        done = done + trips * width

    outs = []
    for g in range(NSA_KV):
        _, l, acc = carry[g]
        o_slc = acc / jnp.maximum(l, 1e-30)
        outs.append(gate(g, 0) * o_cmps[g] + gate(g, 1) * o_slc + gate(g, 2) * o_wins[g])

    for g in range(NSA_KV):
        for jp in range(NSA_GROUP // 2):
            pair = jnp.concatenate([outs[g][:, (2 * jp) * Q_BLOCK:(2 * jp + 1) * Q_BLOCK],
                                    outs[g][:, (2 * jp + 1) * Q_BLOCK:(2 * jp + 2) * Q_BLOCK]], axis=0)
            c0 = (NSA_GROUP * g + 2 * jp) * HEAD_DIM
            o_ref[0, :, c0:c0 + 2 * HEAD_DIM] = pair.T.astype(o_ref.dtype)


def _overlap(n_cmp, n_blk):
    ci = np.arange(n_cmp)[:, None] * CMP_STRIDE
    bj = np.arange(n_blk)[None, :]
    return ((ci < (bj + 1) * SLC_BLOCK) & (ci + CMP_LEN > bj * SLC_BLOCK)).astype(np.float32)


def _nsa_prompt(qt, ck, cvt, sk, svt, wk, wvt, gates):
    B, _, _, T = qt.shape
    nq = T // Q_BLOCK
    n_cmp = ck.shape[2]
    n_blk = T // SLC_BLOCK
    assert n_blk <= KC_POS and T <= KC_POS_RADIX * 256 and T >= WINDOW + Q_BLOCK
    ovl = jnp.asarray(_overlap(n_cmp, n_blk).T, BF16)
    per_b = lambda s: pl.BlockSpec((1,) + s, lambda b, i: (b,) + (0,) * len(s))
    return pl.pallas_call(
        functools.partial(_nsa_prompt_kernel, n_cmp=n_cmp, n_blk=n_blk),
        grid=(B, nq),
        in_specs=[
            pl.BlockSpec((1, NSA_HEADS, HEAD_DIM, Q_BLOCK), lambda b, i: (b, 0, 0, i)),
            per_b((NSA_KV, n_cmp, HEAD_DIM)), per_b((NSA_KV, HEAD_DIM, n_cmp)),
            per_b((NSA_KV, T, KW)), per_b((NSA_KV, HEAD_DIM, T)),
            per_b((NSA_KV, T, KW)), per_b((NSA_KV, HEAD_DIM, T)),
            pl.BlockSpec((1, 1, GATE_ROWS, Q_BLOCK), lambda b, i: (b, i, 0, 0)),
            pl.BlockSpec((n_blk, n_cmp), lambda b, i: (0, 0)),
        ],
        out_specs=pl.BlockSpec((1, Q_BLOCK, NSA_W), lambda b, i: (b, i, 0)),
        out_shape=jax.ShapeDtypeStruct((B, T, NSA_W), BF16),
        compiler_params=_cparams(("parallel", "arbitrary"), 48),
        name="nsa_prompt",
    )(qt, ck, cvt, sk, svt, wk, wvt, gates, ovl)


GATE_ROWS = 32


def _pair_rms(x, gain2, ones_bd):
    hi, lo = _split2(x * x)
    ms = (_dot(hi, ones_bd) + _dot(lo, ones_bd)) * (1.0 / HEAD_DIM)
    return x * lax.rsqrt(ms + NORM_EPS) * gain2


def _nsa_prep_kernel(zq_ref, zc0_ref, zc1_ref, zs0_ref, zs1_ref, zw0_ref, zw1_ref, zg_ref, qg_ref, kg_ref, kf_ref,
                     qt_ref, cmp_ref, slc_ref, win_ref, sk_ref, svt_ref, wk_ref, wvt_ref, gt_ref):
    lane = _iota((1, LANE), 1)
    ones_bd = jnp.where(_iota((LANE, LANE), 0) // HEAD_DIM == _iota((LANE, LANE), 1) // HEAD_DIM, 1.0, 0.0).astype(BF16)
    for hp in range(NSA_HEADS // 2):
        qn = _pair_rms(zq_ref[0, :, hp * LANE:(hp + 1) * LANE], qg_ref[...], ones_bd) * (HEAD_DIM ** -0.5)
        qnt = qn.T
        qt_ref[0, 2 * hp] = qnt[0:HEAD_DIM].astype(BF16)
        qt_ref[0, 2 * hp + 1] = qnt[HEAD_DIM:].astype(BF16)
    cmp_ref[0, :, 0:KV_W] = zc0_ref[0]
    cmp_ref[0, :, KV_W:] = zc1_ref[0]
    feat = kf_ref[...]
    for i, (zk_ref, zv_ref, rows_ref, ka_ref, vt_ref) in enumerate(
            ((zs0_ref, zs1_ref, slc_ref, sk_ref, svt_ref), (zw0_ref, zw1_ref, win_ref, wk_ref, wvt_ref))):
        k = _pair_rms(zk_ref[0], kg_ref[i:i + 1, :], ones_bd)
        v = zv_ref[0]
        rows_ref[0, :, 0:KV_W] = k
        rows_ref[0, :, KV_W:] = v
        for g in range(NSA_KV):
            kg = k if g == 0 else pltpu.roll(k, HEAD_DIM, 1)
            ka_ref[0, g, :, 0:LANE] = jnp.where(lane < HEAD_DIM, kg.astype(BF16), feat[:, 0:LANE])
            ka_ref[0, g, :, LANE:] = feat[:, LANE:]
        vt = v.T
        for g in range(NSA_KV):
            vt_ref[0, g] = vt[g * HEAD_DIM:(g + 1) * HEAD_DIM].astype(BF16)
    gt_ref[0, 0] = jax.nn.sigmoid(zg_ref[0]).T[0:GATE_ROWS]


def _nsa_prep(z, q_gain, k_gains, offs):
    B, T, _ = z.shape
    nq = T // Q_BLOCK
    assert offs["a_q"] % NSA_W == 0 and offs["a_kv"] % LANE == 0 and offs["a_g"] % LANE == 0
    kv0 = offs["a_kv"] // LANE
    zcol = lambda c: pl.BlockSpec((1, Q_BLOCK, LANE), lambda b, i: (b, i, c))
    full = lambda s: pl.BlockSpec(s, lambda b, i: (0,) * len(s))
    rows = pl.BlockSpec((1, Q_BLOCK, ROW_W), lambda b, i: (b, i, 0))
    keys = pl.BlockSpec((1, NSA_KV, Q_BLOCK, KW), lambda b, i: (b, 0, i, 0))
    vals = pl.BlockSpec((1, NSA_KV, HEAD_DIM, Q_BLOCK), lambda b, i: (b, 0, 0, i))
    two = lambda gain: jnp.concatenate([gain, gain], axis=-1)
    return pl.pallas_call(
        _nsa_prep_kernel,
        grid=(B, nq),
        in_specs=[pl.BlockSpec((1, Q_BLOCK, NSA_W), lambda b, i, c=offs["a_q"] // NSA_W: (b, i, c))]
        + [zcol(kv0 + j) for j in range(6)] + [zcol(offs["a_g"] // LANE)]
        + [full((1, LANE)), full((2, LANE)), pl.BlockSpec((Q_BLOCK, KW), lambda b, i: (i, 0))],
        out_specs=[pl.BlockSpec((1, NSA_HEADS, HEAD_DIM, Q_BLOCK), lambda b, i: (b, 0, 0, i)), rows, rows, rows,
                   keys, vals, keys, vals, pl.BlockSpec((1, 1, GATE_ROWS, Q_BLOCK), lambda b, i: (b, i, 0, 0))],
        out_shape=[jax.ShapeDtypeStruct((B, NSA_HEADS, HEAD_DIM, T), BF16)]
        + [jax.ShapeDtypeStruct((B, T, ROW_W), F32)] * 3
        + [jax.ShapeDtypeStruct((B, NSA_KV, T, KW), BF16), jax.ShapeDtypeStruct((B, NSA_KV, HEAD_DIM, T), BF16)] * 2
        + [jax.ShapeDtypeStruct((B, nq, GATE_ROWS, Q_BLOCK), F32)],
        compiler_params=_cparams(("parallel", "parallel"), 32),
        name="nsa_prep",
    )(z, z, z, z, z, z, z, z, two(q_gain).reshape(1, LANE), two(k_gains[1:3]), jnp.asarray(_key_features(T), BF16))


NSA_DECODE_SEQS = 2

def _nsa_decode_kernel(pt_ref, q_ref, new_ref, gt_ref, pe_ref, w1_ref, w2_ref, kg_ref, ovl_ref, exp_ref, *rest,
                       n_pages, n_seq):
    del pt_ref
    all_cmp_pages = rest[:n_seq * n_pages]
    all_slc_pages = rest[n_seq * n_pages:2 * n_seq * n_pages]
    win_ref, o_ref, rows_ref, x_ref = rest[2 * n_seq * n_pages:2 * n_seq * n_pages + 4]
    n_past = n_pages * PAGE_SIZE
    n_cmp = n_past // CMP_STRIDE
    q_pos = n_past
    halves = ROW_W // LANE
    for p, r in enumerate(all_cmp_pages):
        for c in range(halves):
            rows_ref[c, p * PAGE_SIZE:(p + 1) * PAGE_SIZE, :] = r[0, 0, c * LANE:(c + 1) * LANE, :].T
    for r in range(CMP_STRIDE):
        for c in range(halves):
            x_ref[:, r * ROW_W + c * LANE:r * ROW_W + (c + 1) * LANE] = rows_ref.at[c][
                pl.ds(r, n_seq * n_cmp, stride=CMP_STRIDE), :]
    jrow = _iota((8, 1), 0)
    xnew = jnp.zeros((8, ROW_W), F32)
    for sq in range(n_seq):
        xnew = jnp.where(jrow == sq, new_ref[sq][0:1, 0:ROW_W], xnew)
    xnew = jnp.concatenate([xnew, jnp.zeros((8, CHUNK_W - ROW_W), F32)], axis=1)
    all_cks, all_cv = _compress(x_ref[...], xnew, pe_ref, w1_ref, w2_ref, kg_ref, n_seq)

    lanes = _iota((1, LANE), 1)
    n_blk = (n_past + 1 + SLC_BLOCK - 1) // SLC_BLOCK
    cur = q_pos // SLC_BLOCK
    ii = _iota((LANE, LANE), 0)
    jj = _iota((LANE, LANE), 1)
    n_win = win_ref.shape[3]
    kpos = _iota((1, n_past), 1)
    wpos = (q_pos - n_win) + _iota((1, n_win), 1)
    dist_w = q_pos - wpos
    mask_w = (dist_w < WINDOW) & (wpos >= 0)
    dist_c = q_pos - (lanes * CMP_STRIDE + (CMP_LEN - 1))
    mask_c = dist_c >= 0
    forced = (lanes == 0) | (lanes == cur) | (lanes == cur - 1)

    ch = []
    for sq in range(n_seq):
        for g in range(NSA_KV):
            lo = g * HEAD_DIM
            rows = slice(sq * n_cmp, (sq + 1) * n_cmp)
            ch.append(dict(sq=sq, g=g, lo=lo, hi=lo + HEAD_DIM, qg=q_ref[sq, g], slope=_alibi_slope(jrow, g),
                           ck=all_cks[g][rows].astype(BF16), cv=all_cv[rows, lo:lo + HEAD_DIM].astype(BF16),
                           new=new_ref[sq], pages=all_slc_pages[sq * n_pages:(sq + 1) * n_pages]))

    def new_key(c, row):
        return c["new"][row:row + 1, c["lo"]:c["hi"]], c["new"][row:row + 1, KV_W + c["lo"]:KV_W + c["hi"]]

    for c in ch:
        c["s"] = jnp.where(mask_c, _dot_nt(c["qg"], c["ck"]) - c["slope"] * dist_c.astype(F32), NEG_BIG)
    for c in ch:
        e = jnp.where(mask_c, jnp.exp(c["s"] - jnp.max(c["s"], axis=1, keepdims=True)), 0.0)
        c["p"] = (e / jnp.maximum(jnp.sum(e, axis=1, keepdims=True), 1e-30)).astype(BF16)
    for c in ch:
        c["o_cmp"] = _dot(c["p"], c["cv"])
        c["imp"] = _dot(c["p"], ovl_ref[...])
    for c in ch:
        imp = jnp.sum(jnp.where(jrow < NSA_GROUP, c["imp"], 0.0), axis=0, keepdims=True)
        imp = jnp.where(forced, SEL_BIG, jnp.where(lanes * SLC_BLOCK <= q_pos, imp, -SEL_BIG))
        imp = jnp.where(lanes < n_blk, imp, -3e38)
        impr = jnp.broadcast_to(imp, (LANE, LANE))
        impc = jnp.sum(jnp.where(ii == jj, impr, 0.0), axis=1, keepdims=True)
        beats = jnp.where(impc > impr, 1.0, jnp.where((impc == impr) & (ii < jj), 1.0, 0.0))
        rank = jnp.sum(beats, axis=0, keepdims=True)
        c["sel"] = jnp.where(rank < float(min(N_SEL, n_blk)), 1.0, 0.0)
    for c in ch:
        c["selk"] = _dot(jnp.broadcast_to(c["sel"], (8, LANE)).astype(BF16), exp_ref[...])

    def attend(scores, vts, masks, dists, new_rows, new_masks):
        outs = []
        ss, s_news = [], []
        for c, s, mask, dist, row, mask_new in zip(ch, scores, masks, dists, new_rows, new_masks):
            ss.append(jnp.where(mask, s - c["slope"] * dist.astype(F32), NEG_BIG))
            knew, _ = new_key(c, row)
            s_new = jnp.sum(c["qg"].astype(F32) * knew.astype(BF16).astype(F32), axis=1, keepdims=True)
            s_news.append(s_new if mask_new is None else jnp.where(mask_new, s_new, NEG_BIG))
        ms = [jnp.maximum(jnp.max(s, axis=1, keepdims=True), s_new) for s, s_new in zip(ss, s_news)]
        es = [jnp.where(mask, jnp.exp(s - m), 0.0) for s, m, mask in zip(ss, ms, masks)]
        pvs = [_dot_nt(e.astype(BF16), vt) for e, vt in zip(es, vts)]
        for c, e, m, s_new, pv, row, mask_new in zip(ch, es, ms, s_news, pvs, new_rows, new_masks):
            e_new = jnp.exp(s_new - m)
            if mask_new is not None:
                e_new = jnp.where(mask_new, e_new, 0.0)
            _, vnew = new_key(c, row)
            l = jnp.sum(e, axis=1, keepdims=True) + e_new
            o = pv + e_new.astype(BF16).astype(F32) * vnew.astype(BF16).astype(F32)
            outs.append(o / jnp.maximum(l, 1e-30))
        return outs

    def paged(c, off):
        return jnp.concatenate([r[0, 0, off + c["lo"]:off + c["hi"], :] for r in c["pages"]], axis=1).astype(BF16)

    n_ch = len(ch)
    o_slc = attend([_dot(c["qg"], paged(c, 0)) for c in ch], [paged(c, KV_W) for c in ch],
                   [c["selk"] > 0.5 for c in ch], [q_pos - kpos] * n_ch, [1] * n_ch,
                   [c["sel"][:, cur:cur + 1] > 0.5 for c in ch])
    o_win = attend([_dot(c["qg"], win_ref[0, c["sq"], c["lo"]:c["hi"], :].astype(BF16)) for c in ch],
                   [win_ref[0, c["sq"], KV_W + c["lo"]:KV_W + c["hi"], :].astype(BF16) for c in ch],
                   [mask_w] * n_ch, [dist_w] * n_ch, [2] * n_ch, [None] * n_ch)
    for c, o_s, o_w in zip(ch, o_slc, o_win):
        gt = gt_ref[c["sq"], c["g"]]
        o_ref[c["sq"], c["g"]] = gt[:, 0:1] * c["o_cmp"] + gt[:, 1:2] * o_s + gt[:, 2:3] * o_w


def _rows_minor(cache):
    lead = cache.ndim - 4
    perm = tuple(range(lead)) + (lead + 1, lead + 2, lead + 3, lead)
    return cache.transpose(perm).reshape(cache.shape[:lead] + (ROW_W, cache.shape[lead]))


def _nsa_decode(page_table, l, q, new_rows, gates, cmp_t, slc_t, win_t, pee, w1e, w2e, kg0):
    B, n_pages = page_table.shape
    n_past = n_pages * PAGE_SIZE
    n_cmp = n_past // CMP_STRIDE
    n_win = win_t.shape[3]
    assert n_cmp == LANE and n_win <= n_past
    n_blk = (n_past + 1 + SLC_BLOCK - 1) // SLC_BLOCK
    ovl = np.zeros((n_cmp, LANE), np.float32)
    ovl[:, :n_blk] = _overlap(n_cmp, n_blk)
    expand = (np.arange(n_past)[None, :] // SLC_BLOCK == np.arange(LANE)[:, None]).astype(np.float32)
    S = NSA_DECODE_SEQS
    assert B % S == 0
    full = lambda s: pl.BlockSpec(s, lambda i, pt: (0,) * len(s))
    per_b = lambda s: pl.BlockSpec((S,) + s, lambda i, pt: (i,) + (0,) * len(s))
    page = lambda sq, p: pl.BlockSpec((1, 1, ROW_W, PAGE_SIZE), lambda i, pt: (l, pt[S * i + sq, p], 0, 0))
    pages = [page(sq, p) for sq in range(S) for p in range(n_pages)]

    in_specs = [per_b((NSA_KV, 8, HEAD_DIM)), per_b((8, ROW_W)), per_b((NSA_KV, 8, LANE)), full(pee.shape),
                full(w1e.shape), full(w2e.shape), full((1, HEAD_DIM)), full(ovl.shape), full(expand.shape)]
    in_specs += pages * 2
    in_specs += [pl.BlockSpec((1, S, ROW_W, n_win), lambda i, pt: (l, i, 0, 0))]
    return pl.pallas_call(
        functools.partial(_nsa_decode_kernel, n_pages=n_pages, n_seq=S),
        grid_spec=pltpu.PrefetchScalarGridSpec(
            num_scalar_prefetch=1, grid=(B // S,), in_specs=in_specs,
            out_specs=pl.BlockSpec((S, NSA_KV, 8, HEAD_DIM), lambda i, pt: (i, 0, 0, 0)),
            scratch_shapes=[pltpu.VMEM((ROW_W // LANE, S * n_past, LANE), F32),
                            pltpu.VMEM((S * n_cmp, CHUNK_W), F32)]),
        out_shape=jax.ShapeDtypeStruct((B, NSA_KV, 8, HEAD_DIM), F32),
        compiler_params=_cparams(("arbitrary",), 56),
        name="nsa_decode",
    )(page_table, q, new_rows, gates, pee, w1e, w2e, kg0.reshape(1, HEAD_DIM), jnp.asarray(ovl, BF16),
      jnp.asarray(expand, BF16), *([cmp_t] * (S * n_pages)), *([slc_t] * (S * n_pages)), win_t)


DN_TB = 2 * DN_CHUNK
HALO = 8


def _softplus(x):
    return jnp.maximum(x, 0.0) + jnp.log(1.0 + jnp.exp(-jnp.abs(x)))


def _dn_prompt_kernel(zq_ref, zk_ref, zv_ref, zz_ref, zs_ref, cw_ref, a_ref, dtb_ref, gain_ref, o_ref, s_out_ref,
                      s_ref, buf_ref, *, n_step, col_b, col_a):
    t = pl.program_id(1)
    TB, C = DN_TB, DN_CHUNK

    @pl.when(t == 0)
    def _():
        s_ref[...] = jnp.zeros_like(s_ref)
        buf_ref[0:HALO, :] = jnp.zeros((HALO, DN_QKV), F32)

    @pl.when(t > 0)
    def _():
        buf_ref[0:HALO, :] = buf_ref[TB:TB + HALO, :]

    buf_ref[HALO:HALO + TB, 0:DN_QK] = zq_ref[0]
    buf_ref[HALO:HALO + TB, DN_QK:2 * DN_QK] = zk_ref[0]
    buf_ref[HALO:HALO + TB, 2 * DN_QK:] = zv_ref[0]
    cw = cw_ref[...]
    first = HALO - (CONV_W - 1)
    y = buf_ref[pl.ds(first, TB), :] * cw[0:1, :]
    for j in range(1, CONV_W):
        y += buf_ref[pl.ds(first + j, TB), :] * cw[j:j + 1, :]
    act = _silu(y)

    zs = zs_ref[0]
    beta_all = jax.nn.sigmoid(zs)
    g_all = -a_ref[...] * _softplus(zs + dtb_ref[...])
    zz = zz_ref[0]
    gain = gain_ref[...]

    ii = _iota((C, C), 0)
    jj = _iota((C, C), 1)
    incl = ii >= jj
    eye = (ii == jj).astype(F32)
    n_sub = TB // C
    ch = []
    for sc in range(n_sub):
        rows = slice(sc * C, (sc + 1) * C)
        for h in range(DN_HEADS):
            cs = slice(h * DN_DK, (h + 1) * DN_DK)
            q = _l2n(act[rows, cs]) * (DN_DK ** -0.5)
            k = _l2n(act[rows, DN_QK + h * DN_DK:DN_QK + (h + 1) * DN_DK])
            v = act[rows, 2 * DN_QK + h * DN_DV:2 * DN_QK + (h + 1) * DN_DV]
            gcol = g_all[rows, col_a + h:col_a + h + 1]
            bcol = beta_all[rows, col_b + h:col_b + h + 1]
            grow = jnp.sum(jnp.where(ii <= jj, gcol, 0.0), axis=0, keepdims=True)
            gcum = jnp.sum(jnp.where(ii == jj, grow, 0.0), axis=1, keepdims=True)
            decay = jnp.where(incl, jnp.exp(jnp.where(incl, gcum - grow, 0.0)), 0.0)
            kb = k.astype(BF16)
            ch.append(dict(q=q, k=k, v=v, kb=kb, bcol=bcol, gcum=gcum, decay=decay, sc=sc, h=h, rows=rows, cs=cs))
    for c in ch:
        c["npow"] = -jnp.where(ii > jj, c["bcol"] * _dot_nt(c["kb"], c["kb"]) * c["decay"], 0.0)
        c["tinv"] = eye + c["npow"]
    for _ in range(int(np.log2(C)) - 1):
        for c in ch:
            nb = c["npow"].astype(BF16)
            c["npow"] = _dot(nb, nb)
        for c in ch:
            c["tinv"] = c["tinv"] + _dot(c["tinv"].astype(BF16), c["npow"].astype(BF16))
    for c in ch:
        tb = c["tinv"].astype(BF16)
        c["eg"] = jnp.exp(c["gcum"])
        c["u"] = _dot(tb, (c["v"] * c["bcol"]).astype(BF16))
        c["w"] = _dot(tb, (c["k"] * (c["bcol"] * c["eg"])).astype(BF16)).astype(BF16)
        c["attn"] = (_dot_nt(c["q"].astype(BF16), c["kb"]) * c["decay"]).astype(BF16)
    for sc in range(n_sub):
        cur = [c for c in ch if c["sc"] == sc]
        Ss = [s_ref[c["h"]] for c in cur]
        Sbs = [S.astype(BF16) for S in Ss]
        vns = [c["u"] - _dot(c["w"], Sb) for c, Sb in zip(cur, Sbs)]
        for c, S, Sb, v_new in zip(cur, Ss, Sbs, vns):
            vnb = v_new.astype(BF16)
            o = _dot((c["q"] * c["eg"]).astype(BF16), Sb) + _dot(c["attn"], vnb)
            g_last = c["gcum"][C - 1:C, :]
            s_ref[c["h"]] = S * jnp.exp(g_last) + _dot_tn((c["k"] * jnp.exp(g_last - c["gcum"])).astype(BF16), vnb)
            o_ref[0, c["rows"], c["cs"]] = (_rms(o, gain) * _silu(zz[c["rows"], c["cs"]])).astype(o_ref.dtype)

    @pl.when(t == n_step - 1)
    def _():
        s_out_ref[0] = s_ref[...]


def _dn_prompt(z, conv_w, a_log, dt_bias, out_gain, offs):
    B, T, _ = z.shape
    n_step = T // DN_TB
    wide = lambda name, k=0: pl.BlockSpec((1, DN_TB, DN_QK), lambda b, t, c=offs[name] // DN_QK + k: (b, t, c))
    assert offs["d_qkv"] % DN_QK == 0 and offs["d_z"] % DN_V == 0 and offs["a_g"] % LANE == 0
    col_b, col_a = offs["d_b"] - offs["a_g"], offs["d_a"] - offs["a_g"]
    lanes = lambda vals, col: jnp.zeros((1, LANE), F32).at[0, col:col + DN_HEADS].set(vals)
    full = lambda s: pl.BlockSpec(s, lambda b, t: (0,) * len(s))
    return pl.pallas_call(
        functools.partial(_dn_prompt_kernel, n_step=n_step, col_b=col_b, col_a=col_a),
        grid=(B, n_step),
        in_specs=[wide("d_qkv", 0), wide("d_qkv", 1), wide("d_qkv", 2), wide("d_z"),
                  pl.BlockSpec((1, DN_TB, LANE), lambda b, t, c=offs["a_g"] // LANE: (b, t, c)),
                  full((CONV_W, DN_QKV)), full((1, LANE)), full((1, LANE)), full((1, DN_DV))],
        out_specs=[pl.BlockSpec((1, DN_TB, DN_V), lambda b, t: (b, t, 0)),
                   pl.BlockSpec((1, DN_HEADS, DN_DK, DN_DV), lambda b, t: (b, 0, 0, 0))],
        out_shape=[jax.ShapeDtypeStruct((B, T, DN_V), BF16), jax.ShapeDtypeStruct((B, DN_HEADS, DN_DK, DN_DV), F32)],
        scratch_shapes=[pltpu.VMEM((DN_HEADS, DN_DK, DN_DV), F32), pltpu.VMEM((HALO + DN_TB, DN_QKV), F32)],
        compiler_params=_cparams(("parallel", "arbitrary"), 32),
        name="dn_prompt",
    )(z, z, z, z, z, conv_w, lanes(jnp.exp(a_log), col_a), lanes(dt_bias, col_a), out_gain.reshape(1, DN_DV))


def _row0(x):
    return jnp.where(_iota(x.shape, 0) == 0, x, 0.0)


def _dn_decode_kernel(vec_ref, s_ref, o_ref, s_out_ref, *, bb):
    def body(b, _):
        for h in range(DN_HEADS):
            x = vec_ref[b, h]
            S = s_ref[b, h]
            xs = _dot(x.astype(BF16), S.astype(BF16))
            k, q, v, eg, beta = x[0:1], x[1:2], x[2:3], x[3:4], x[4:5]
            v_new = beta * (v - eg * xs[0:1])
            kb = k.astype(BF16).astype(F32)
            qk = jnp.sum(q.astype(BF16).astype(F32) * kb, axis=1, keepdims=True)
            o = eg * xs[1:2] + qk.astype(BF16).astype(F32) * v_new.astype(BF16).astype(F32)
            o_ref[b, h] = jnp.broadcast_to(o, (8, DN_DV))
            s_out_ref[b, h] = S * eg[:, 0:1] + _dot_tn(_row0(x).astype(BF16),
                                                      _row0(jnp.broadcast_to(v_new, (8, DN_DV))).astype(BF16))
        return 0

    lax.fori_loop(0, bb, body, 0)


def _state_step(kernel_fn, name, vec, S_all, l, bb=8):
    B, H = vec.shape[:2]
    spec_v = pl.BlockSpec((bb, H, 8, vec.shape[3]), lambda i: (i, 0, 0, 0))
    spec_s = pl.BlockSpec((bb, H) + S_all.shape[3:], lambda i: (i, 0, 0, 0))
    spec_sl = pl.BlockSpec((None, bb, H) + S_all.shape[3:], lambda i: (l, i, 0, 0, 0))
    return pl.pallas_call(
        functools.partial(kernel_fn, bb=bb),
        grid=(B // bb,),
        in_specs=[spec_v, spec_sl],
        out_specs=[spec_v, spec_s],
        out_shape=[jax.ShapeDtypeStruct(vec.shape, F32), jax.ShapeDtypeStruct(S_all.shape[1:], F32)],
        compiler_params=_cparams(("parallel",), 32),
        name=name,
    )(vec, S_all)


def _dn_decode(vec, S_all, l):
    return _state_step(_dn_decode_kernel, "dn_decode", vec, S_all, l)


HG_TB = 128
HG_LEVELS = tuple(HG_TB >> (i + 1) for i in range(int(np.log2(HG_TB))))


def _hg_tables():
    r = np.arange(HG_TB)
    tril = (r[:, None] >= r[None, :]).astype(np.float32)
    mats = [tril] + [tril[(r // (2 * w)) * 2 * w + w - 1] for w in HG_LEVELS]
    level = np.full((HG_TB, HG_TB), -1, np.int32)
    for i, w in enumerate(HG_LEVELS):
        same_block = r[:, None] // (2 * w) == r[None, :] // (2 * w)
        split = (r[:, None] % (2 * w) >= w) & (r[None, :] % (2 * w) < w)
        level[same_block & split] = i
    return np.concatenate(mats, axis=0), level


def _hg_prompt_kernel(zf_ref, zi_ref, zq_ref, zo_ref, lb_ref, gain_ref, sel_ref, lvl_ref, o_ref, s_out_ref, st_ref,
                      *, n_step):
    t = pl.program_id(1)
    TB = HG_TB

    @pl.when(t == 0)
    def _():
        st_ref[...] = jnp.zeros_like(st_ref)

    zf = zf_ref[0]
    lb = lb_ref[...]
    lf = jnp.log(lb + (1.0 - lb) * jax.nn.sigmoid(zf))
    k = (1.0 - lb) * jax.nn.sigmoid(-zf)
    q = _silu(zq_ref[0]) * (HG_DK ** -0.5)
    vb = zi_ref[0].astype(BF16)
    gain = gain_ref[...]
    hi = lf.astype(BF16)
    r1 = lf - hi.astype(F32)
    mid = r1.astype(BF16)
    lo = (r1 - mid.astype(F32)).astype(BF16)
    sel = sel_ref[...]
    gg = _dot(sel, hi) + (_dot(sel, mid) + _dot(sel, lo))
    G = gg[0:TB]
    row = _iota((TB, 1), 0)
    qts, kts = [], []
    for i, w in enumerate(HG_LEVELS):
        d = G - gg[(i + 1) * TB:(i + 2) * TB]
        right = (row % (2 * w)) >= w
        qts.append(jnp.where(right, q * jnp.exp(jnp.minimum(d, 0.0)), 0.0).astype(BF16))
        kts.append(jnp.where(right, 0.0, k * jnp.exp(jnp.minimum(-d, 0.0))).astype(BF16))
    lvl = lvl_ref[...]
    eye = _iota((TB, TB), 0) == _iota((TB, TB), 1)
    qe = (q * jnp.exp(G)).astype(BF16)
    g_last = G[TB - 1:TB, :]
    kd = (k * jnp.exp(g_last - G)).astype(BF16)
    eg_last = jnp.exp(g_last)
    qk = q * k
    heads = [slice(h * HG_DK, (h + 1) * HG_DK) for h in range(HG_HEADS)]
    sts = [st_ref[h] for h in range(HG_HEADS)]
    o_inter = [_dot_nt(qe[:, cs], st.astype(BF16)) for cs, st in zip(heads, sts)]
    kv = [_dot_tn(vb[:, cs], kd[:, cs]) for cs in heads]
    a = [jnp.where(eye, jnp.sum(qk[:, cs], axis=1, keepdims=True), 0.0) for cs in heads]
    for i in range(len(HG_LEVELS)):
        parts = [_dot_nt(qts[i][:, cs], kts[i][:, cs]) for cs in heads]
        a = [a_h + jnp.where(lvl == i, p, 0.0) for a_h, p in zip(a, parts)]
    o_intra = [_dot(a_h.astype(BF16), vb[:, cs]) for a_h, cs in zip(a, heads)]
    for h, cs in enumerate(heads):
        o = o_inter[h] + o_intra[h]
        o_ref[0, :, cs] = (_rms(o, gain) * jax.nn.sigmoid(zo_ref[0, :, cs])).astype(o_ref.dtype)
        st_ref[h] = sts[h] * eg_last[:, cs] + kv[h]

    @pl.when(t == n_step - 1)
    def _():
        for h in range(HG_HEADS):
            s_out_ref[0, h] = st_ref[h].T


def _hg_prompt(z, lb, out_gain, offs):
    B, T, _ = z.shape
    tb = HG_TB
    n_step = T // tb
    sel, level = _hg_tables()
    assert all(offs[n] % HG_WK == 0 for n in ("r_f", "r_i", "r_q", "r_og"))
    col = lambda name: pl.BlockSpec((1, tb, HG_WK), lambda b, t, c=offs[name] // HG_WK: (b, t, c))
    full = lambda s: pl.BlockSpec(s, lambda b, t: (0,) * len(s))
    return pl.pallas_call(
        functools.partial(_hg_prompt_kernel, n_step=n_step),
        grid=(B, n_step),
        in_specs=[col("r_f"), col("r_i"), col("r_q"), col("r_og"), full((1, HG_WK)), full((1, HG_DV)),
                  full(sel.shape), full(level.shape)],
        out_specs=[pl.BlockSpec((1, tb, HG_WV), lambda b, t: (b, t, 0)),
                   pl.BlockSpec((1, HG_HEADS, HG_DK, HG_DV), lambda b, t: (b, 0, 0, 0))],
        out_shape=[jax.ShapeDtypeStruct((B, T, HG_WV), BF16), jax.ShapeDtypeStruct((B, HG_HEADS, HG_DK, HG_DV), F32)],
        scratch_shapes=[pltpu.VMEM((HG_HEADS, HG_DV, HG_DK), F32)],
        compiler_params=_cparams(("parallel", "arbitrary"), 32),
        name="hg_prompt",
    )(z, z, z, z, lb.reshape(1, HG_WK), out_gain.reshape(1, HG_DV), jnp.asarray(sel, BF16), jnp.asarray(level))


def _hg_decode_kernel(vec_ref, s_ref, o_ref, s_out_ref, *, bb):
    ii = _iota((HG_DK, HG_DK), 0)
    jj = _iota((HG_DK, HG_DK), 1)

    def body(b, _):
        for h in range(HG_HEADS):
            x = vec_ref[b, h]
            S = s_ref[b, h]
            k, q, v, lf = x[0:1], x[1:2], x[2:3], x[3:4]
            f = jnp.exp(lf)
            qs = _dot(jnp.broadcast_to(q * f, (8, HG_DK)).astype(BF16), S.astype(BF16))
            a = jnp.sum(q * k, axis=1, keepdims=True)
            o = qs[0:1] + a.astype(BF16).astype(F32) * v.astype(BF16).astype(F32)
            o_ref[b, h] = jnp.broadcast_to(o, (8, HG_DV))
            fcol = jnp.sum(jnp.where(ii == jj, f, 0.0), axis=1, keepdims=True)
            s_out_ref[b, h] = S * fcol + _dot_tn(_row0(x).astype(BF16),
                                                 _row0(jnp.broadcast_to(v, (8, HG_DV))).astype(BF16))
        return 0

    lax.fori_loop(0, bb, body, 0)


def _hg_decode(vec, S_all, l):
    return _state_step(_hg_decode_kernel, "hg_decode", vec, S_all, l)


def _head_rms(x, g):
    return x * lax.rsqrt(jnp.mean(x * x, axis=-1, keepdims=True) + NORM_EPS) * g


def _l2n(x):
    return x * lax.rsqrt(jnp.sum(x * x, axis=-1, keepdims=True) + L2_EPS)


def _rows8(rows):
    x = jnp.stack(rows, axis=-2)
    pad = [(0, 0)] * x.ndim
    pad[-2] = (0, 8 - len(rows))
    return jnp.pad(x, pad)


def _layer(x, l, prm, wts, past, page_table):
    B, T, D = x.shape
    M = B * T
    tm = min(M, ROW_TILE)
    tm_w = min(M, ROW_TILE_STREAMED)
    x2 = x.reshape(M, D)
    x2 = _ffn(x2, prm["ffn1_norm"][l], wts["ffn1_w_gu"][l], wts["ffn1_w_down"][l], tm_w)
    z, mgates = _inproj(x2, prm["mix_norm"][l], wts["w_in"][l], min(M, ROW_TILE_INPROJ))
    sizes, _, offs, _ = _z_layout()
    pee, w1e, w2e = wts["cmp"][l]
    kg = prm["nsa_k_norm"][l]
    p = jax.nn.softmax(prm["hg_lb_logits"], axis=0)
    lb = (jnp.cumsum(p, axis=0) - p[0])[l]
    if past is None:
        z3 = z.reshape(B, T, -1)
        qt, cmp_new, slc_new, win_new, sk, svt, wk, wvt, gt = _nsa_prep(z3, prm["nsa_q_norm"][l], kg, offs)
        ck, cvt = _cmp_prompt(cmp_new, pee, w1e, w2e, kg[0])
        o_a = _nsa_prompt(qt, ck, cvt, sk, svt, wk, wvt, gt)
        o_d, dn_state = _dn_prompt(z3, prm["dn_conv_w"][l], prm["dn_A_log"][l], prm["dn_dt_bias"][l],
                                   prm["dn_out_norm"][l], offs)
        o_h, hg_state = _hg_prompt(z3, lb, prm["hg_out_norm"][l], offs)
        rows = lambda a: a.reshape(B, -1, 2, NSA_KV, HEAD_DIM)
        cmp_new, slc_new, win_state = rows(cmp_new), rows(slc_new), rows(win_new[:, T - min(WINDOW, T):])
        conv_state = z3[:, T - (CONV_W - 1):, offs["d_qkv"]:offs["d_qkv"] + DN_QKV]
    else:
        zs = {n: z[:, offs[n]:offs[n] + sizes[n]].reshape(B, T, sizes[n]) for n in Z_ORDER if n != "m_g"}
        (o_a, o_d, o_h), (cmp_new, slc_new, win_state, conv_state, dn_state, hg_state) = _decode_mixers(
            zs, l, prm, (pee, w1e, w2e), lb, past, page_table)

    x2 = _merge(x2, o_a.reshape(M, NSA_W), o_d.reshape(M, DN_V), o_h.reshape(M, HG_WV), mgates, wts["w_branch"][l],
                wts["w_out"][l], tm)
    x2 = _ffn(x2, prm["ffn2_norm"][l], wts["ffn2_w_gu"][l], wts["ffn2_w_down"][l], tm_w)
    return x2.reshape(B, T, D), (cmp_new, slc_new, win_state, conv_state, dn_state, hg_state)


def _decode_mixers(zs, l, prm, cmp_w, lb, past, page_table):
    B, T = zs["a_q"].shape[:2]
    pee, w1e, w2e = cmp_w
    kg = prm["nsa_k_norm"][l]
    q = _head_rms(zs["a_q"].reshape(B, T, NSA_HEADS, HEAD_DIM), prm["nsa_q_norm"][l]) * (HEAD_DIM ** -0.5)
    kv = zs["a_kv"].reshape(B, T, 3, 2, NSA_KV, HEAD_DIM)
    cmp_new = kv[:, :, 0]
    slc_new = jnp.stack([_head_rms(kv[:, :, 1, 0], kg[1]), kv[:, :, 1, 1]], axis=2)
    win_new = jnp.stack([_head_rms(kv[:, :, 2, 0], kg[2]), kv[:, :, 2, 1]], axis=2)
    gates = jax.nn.sigmoid(zs["a_g"].reshape(B, T, NSA_HEADS, 3))
    qd = q.reshape(B, NSA_KV, NSA_GROUP, HEAD_DIM)
    qd = jnp.pad(qd, ((0, 0), (0, 0), (0, 8 - NSA_GROUP), (0, 0))).astype(BF16)
    new_rows = _rows8([cmp_new.reshape(B, ROW_W), slc_new.reshape(B, ROW_W), win_new.reshape(B, ROW_W)])
    gd = gates.reshape(B, NSA_KV, NSA_GROUP, 3)
    gd = jnp.pad(gd, ((0, 0), (0, 0), (0, 8 - NSA_GROUP), (0, LANE - 3)))
    o8 = _nsa_decode(page_table, l, qd, new_rows, gd, past["cmp"], past["slc"], past["win"], pee, w1e, w2e, kg[0])
    o_a = o8[:, :, :NSA_GROUP].reshape(B, T, NSA_W)

    d_qkv = zs["d_qkv"]
    xx = jnp.concatenate([past["conv"][l], d_qkv], axis=1)
    cw = prm["dn_conv_w"][l]
    qkv = sum(xx[:, j:j + T] * cw[j] for j in range(CONV_W))
    conv_state = xx[:, -(CONV_W - 1):]
    dq, dk, dv = jnp.split(jax.nn.silu(qkv), [DN_QK, 2 * DN_QK], axis=-1)
    dq = _l2n(dq.reshape(B, T, DN_HEADS, DN_DK)) * (DN_DK ** -0.5)
    dk = _l2n(dk.reshape(B, T, DN_HEADS, DN_DK))
    beta = jax.nn.sigmoid(zs["d_b"])
    g = -jnp.exp(prm["dn_A_log"][l]) * jax.nn.softplus(zs["d_a"] + prm["dn_dt_bias"][l])
    lanes = lambda a: jnp.broadcast_to(a[:, 0, :, None], (B, DN_HEADS, DN_DK))
    vec = _rows8([dk[:, 0], dq[:, 0], dv.reshape(B, DN_HEADS, DN_DV), lanes(jnp.exp(g)), lanes(beta)])
    o8, dn_state = _dn_decode(vec, past["dn_S"], l)
    o_d = o8[:, :, 0].reshape(B, T, DN_V)
    o_d = _head_rms(o_d.reshape(B, T, DN_HEADS, DN_DV), prm["dn_out_norm"][l]) * jax.nn.silu(
        zs["d_z"].reshape(B, T, DN_HEADS, DN_DV))

    zf = zs["r_f"]
    logf = jnp.log(lb + (1.0 - lb) * jax.nn.sigmoid(zf))
    k_in = (1.0 - lb) * jax.nn.sigmoid(-zf)
    hq = jax.nn.silu(zs["r_q"]) * (HG_DK ** -0.5)
    hd = lambda a: a.reshape(B, HG_HEADS, HG_DK)
    vec = _rows8([hd(k_in), hd(hq), hd(zs["r_i"]), hd(logf)])
    o8, hg_state = _hg_decode(vec, past["hg_S"], l)
    o_h = o8[:, :, 0].reshape(B, T, HG_WV)
    o_h = _head_rms(o_h.reshape(B, T, HG_HEADS, HG_DV), prm["hg_out_norm"][l]) * jax.nn.sigmoid(
        zs["r_og"].reshape(B, T, HG_HEADS, HG_DV))
    return (o_a, o_d, o_h), (cmp_new, slc_new, win_new, conv_state, dn_state, hg_state)


def _trunk(x, prm, wts, caches, page_table):
    new = []
    for l in range(DEPTH):
        x, st = _layer(x, l, prm, wts, caches, page_table)
        new.append(st)
    return x, [jnp.stack([s[i] for s in new], axis=0) for i in range(6)]


def kernel(x_prompt, x_sample, cache_cmp_kv, cache_slc_kv, cache_win_kv, state_dn_conv, state_dn_S, state_hg_S,
           page_table, ffn1_norm, ffn1_w_gu, ffn1_w_down, mix_norm, w_in, nsa_q_norm, nsa_k_norm, nsa_cmp_pe,
           nsa_cmp_w1, nsa_cmp_w2, dn_conv_w, dn_A_log, dn_dt_bias, dn_out_norm, hg_lb_logits, hg_out_norm,
           w_branch, w_out, ffn2_norm, ffn2_w_gu, ffn2_w_down):
    prm = dict(ffn1_norm=ffn1_norm, mix_norm=mix_norm, nsa_q_norm=nsa_q_norm, nsa_k_norm=nsa_k_norm,
               dn_conv_w=dn_conv_w, dn_A_log=dn_A_log, dn_dt_bias=dn_dt_bias, dn_out_norm=dn_out_norm,
               hg_lb_logits=hg_lb_logits, hg_out_norm=hg_out_norm, ffn2_norm=ffn2_norm)
    bf = lambda w: w.astype(BF16)
    wts = dict(ffn1_w_gu=bf(ffn1_w_gu), ffn1_w_down=bf(ffn1_w_down), ffn2_w_gu=bf(ffn2_w_gu),
               ffn2_w_down=bf(ffn2_w_down), w_branch=bf(w_branch), w_out=bf(w_out),
               w_in=jnp.stack([_permute_w_in(w_in[l]) for l in range(DEPTH)]),
               cmp=[_cmp_weights(nsa_cmp_pe[l], nsa_cmp_w1[l], nsa_cmp_w2[l]) for l in range(DEPTH)])
    y_p, (p_cmp, p_slc, p_win, p_conv, p_dn, p_hg) = _trunk(x_prompt, prm, wts, None, None)
    caches = dict(cmp=_rows_minor(cache_cmp_kv), slc=_rows_minor(cache_slc_kv), win=_rows_minor(cache_win_kv),
                  conv=state_dn_conv, dn_S=state_dn_S, hg_S=state_hg_S)
    y_s, (s_cmp, s_slc, s_win, s_conv, s_dn, s_hg) = _trunk(x_sample, prm, wts, caches, page_table)
    return (y_p, y_s, p_cmp, s_cmp, p_slc, s_slc, p_win, s_win, p_conv, s_conv, p_dn, s_dn, p_hg, s_hg)
```

```python
import functools

import jax
import jax.numpy as jnp
import numpy as np
from jax import lax
from jax.experimental import pallas as pl
from jax.experimental.pallas import tpu as pltpu

F32 = jnp.float32
BF16 = jnp.bfloat16

D_MODEL = 1024
DEPTH = 2
PAGE_SIZE = 128
HEAD_DIM = 64
NSA_HEADS = 8
NSA_KV = 2
NSA_GROUP = NSA_HEADS // NSA_KV
CMP_LEN = 32
CMP_STRIDE = 16
CMP_HIDDEN = 128
SLC_BLOCK = 64
N_SEL = 16
WINDOW = 512
Q_BLOCK = 128
DN_HEADS = 4
DN_DK = 128
DN_DV = 128
DN_CHUNK = 64
CONV_W = 4
HG_HEADS = 4
HG_DK = 128
HG_DV = 128
NORM_EPS = 1e-6
L2_EPS = 1e-6
NEG_BIG = -1e30
SEL_BIG = 1e9

NSA_W = NSA_HEADS * HEAD_DIM
KV_W = NSA_KV * HEAD_DIM
ROW_W = 2 * KV_W
DN_QK = DN_HEADS * DN_DK
DN_V = DN_HEADS * DN_DV
DN_QKV = 2 * DN_QK + DN_V
HG_WK = HG_HEADS * HG_DK
HG_WV = HG_HEADS * HG_DV
MIX_W = NSA_W + DN_V + HG_WV
IN_SPLITS = (NSA_W, 6 * KV_W, 3 * NSA_HEADS, DN_QKV, DN_HEADS, DN_HEADS, DN_V, HG_WK, HG_WV, HG_WK, HG_WV, 3 * D_MODEL)
IN_NAMES = ("a_q", "a_kv", "a_g", "d_qkv", "d_b", "d_a", "d_z", "r_f", "r_i", "r_q", "r_og", "m_g")
Z_ORDER = ("d_qkv", "d_z", "r_f", "r_i", "r_q", "r_og", "a_q", "a_kv", "a_g", "d_b", "d_a", "m_g")
Z_TN = 512
CHUNK_W = CMP_STRIDE * ROW_W
CMP_HID_W = 2 * NSA_KV * CMP_HIDDEN

V7X_VMEM_BYTES = 64 * 1024 * 1024
LANE = 128
ROW_TILE = 512
ROW_TILE_STREAMED = 1024
ROW_TILE_INPROJ = 2048


def _cparams(sem, vmem_mb):
    assert vmem_mb * 1024 * 1024 < V7X_VMEM_BYTES
    return pltpu.CompilerParams(dimension_semantics=sem, vmem_limit_bytes=vmem_mb * 1024 * 1024)


def _dot(a, b):
    return jnp.dot(a, b, preferred_element_type=F32)


def _dot_nt(a, b):
    return lax.dot_general(a, b, (((1,), (1,)), ((), ())), preferred_element_type=F32)


def _dot_tn(a, b):
    return lax.dot_general(a, b, (((0,), (0,)), ((), ())), preferred_element_type=F32)


def _split2(a):
    hi = a.astype(BF16)
    lo = (a - hi.astype(F32)).astype(BF16)
    return hi, lo


def _rms(x, g):
    return x * lax.rsqrt(jnp.mean(x * x, axis=-1, keepdims=True) + NORM_EPS) * g


def _silu(x):
    return x * jax.nn.sigmoid(x)


def _iota(shape, dim):
    return lax.broadcasted_iota(jnp.int32, shape, dim)


def _alibi_slope(head_in_group, g):
    out = jnp.full(head_in_group.shape, 2.0 ** -(NSA_GROUP * g + NSA_GROUP), F32)
    for j in range(NSA_GROUP - 2, -1, -1):
        out = jnp.where(head_in_group == j, 2.0 ** -(NSA_GROUP * g + j + 1), out)
    return out


def _ffn_kernel(x_ref, g_ref, wg_ref, wu_ref, wd_ref, o_ref, xn_ref, acc_ref, *, nf):
    j = pl.program_id(1)

    @pl.when(j == 0)
    def _():
        xn_ref[...] = _rms(x_ref[...], g_ref[...]).astype(BF16)
        acc_ref[...] = jnp.zeros_like(acc_ref)

    xn = xn_ref[...]
    tf = wg_ref.shape[1]
    for c0 in range(0, tf, FFN_CHUNK):
        c1 = min(c0 + FFN_CHUNK, tf)
        a = _silu(_dot(xn, wg_ref[:, c0:c1])) * _dot(xn, wu_ref[:, c0:c1])
        acc_ref[...] += _dot(a.astype(BF16), wd_ref[c0:c1, :])

    @pl.when(j == nf - 1)
    def _():
        o_ref[...] = x_ref[...] + 0.5 * acc_ref[...]


FFN_CHUNK = 256


def _ffn(x, gain, w_gu, w_down, tm):
    M, D = x.shape
    F = w_down.shape[0]
    tf = F // 2 if F % (2 * LANE) == 0 else F
    nf = F // tf
    return pl.pallas_call(
        functools.partial(_ffn_kernel, nf=nf),
        grid=(M // tm, nf),
        in_specs=[
            pl.BlockSpec((tm, D), lambda i, j: (i, 0)),
            pl.BlockSpec((1, D), lambda i, j: (0, 0)),
            pl.BlockSpec((D, tf), lambda i, j: (0, j)),
            pl.BlockSpec((D, tf), lambda i, j: (0, j + nf)),
            pl.BlockSpec((tf, D), lambda i, j: (j, 0)),
        ],
        out_specs=pl.BlockSpec((tm, D), lambda i, j: (i, 0)),
        out_shape=jax.ShapeDtypeStruct((M, D), F32),
        scratch_shapes=[pltpu.VMEM((tm, D), BF16), pltpu.VMEM((tm, D), F32)],
        compiler_params=_cparams(("parallel", "arbitrary"), 52),
        name="ffn",
    )(x, gain.reshape(1, D), w_gu, w_gu, w_down)


def _inproj_kernel(x_ref, g_ref, w_ref, z_ref, gate_ref, xn_ref, *, nz):
    j = pl.program_id(1)

    @pl.when(j == 0)
    def _():
        xn_ref[...] = _rms(x_ref[...], g_ref[...]).astype(BF16)

    acc = _dot(xn_ref[...], w_ref[...])

    @pl.when(j < nz)
    def _():
        z_ref[...] = acc

    @pl.when(j >= nz)
    def _():
        gate_ref[...] = jax.nn.sigmoid(acc).astype(gate_ref.dtype)


def _inproj(x, gain, w, tm):
    M, D = x.shape
    N = w.shape[1]
    z_width = _z_layout()[2]["m_g"]
    nz = z_width // Z_TN
    return pl.pallas_call(
        functools.partial(_inproj_kernel, nz=nz),
        grid=(M // tm, N // Z_TN),
        in_specs=[
            pl.BlockSpec((tm, D), lambda i, j: (i, 0)),
            pl.BlockSpec((1, D), lambda i, j: (0, 0)),
            pl.BlockSpec((D, Z_TN), lambda i, j: (0, j)),
        ],
        out_specs=[pl.BlockSpec((tm, Z_TN), lambda i, j: (i, jnp.minimum(j, nz - 1))),
                   pl.BlockSpec((tm, Z_TN), lambda i, j: (i, jnp.maximum(j - nz, 0)))],
        out_shape=[jax.ShapeDtypeStruct((M, z_width), F32), jax.ShapeDtypeStruct((M, N - z_width), BF16)],
        scratch_shapes=[pltpu.VMEM((tm, D), BF16)],
        compiler_params=_cparams(("parallel", "arbitrary"), 48),
        name="inproj",
    )(x, gain.reshape(1, D), w)


def _z_layout():
    sizes = dict(zip(IN_NAMES, IN_SPLITS))
    src = dict(zip(IN_NAMES, np.cumsum((0,) + IN_SPLITS[:-1]).tolist()))
    offs, o = {}, 0
    for n in Z_ORDER:
        if n == "m_g":
            o = -(-o // Z_TN) * Z_TN
        offs[n] = o
        o += sizes[n]
    assert Z_ORDER[-1] == "m_g" and o % Z_TN == 0
    return sizes, src, offs, o


def _permute_w_in(w_in):
    sizes, src, offs, total = _z_layout()
    w = jnp.zeros((w_in.shape[0], total), w_in.dtype)
    for n in Z_ORDER:
        w = lax.dynamic_update_slice(w, w_in[:, src[n]:src[n] + sizes[n]], (0, offs[n]))
    return w.astype(BF16)


def _merge_kernel(x_ref, oa_ref, od_ref, oh_ref, g0_ref, g1_ref, g2_ref, wb_ref, wo_ref, o_ref):
    m = g0_ref[...].astype(F32) * _dot(oa_ref[...].astype(BF16), wb_ref[0:NSA_W, :])
    m += g1_ref[...].astype(F32) * _dot(od_ref[...].astype(BF16), wb_ref[NSA_W:NSA_W + DN_V, :])
    m += g2_ref[...].astype(F32) * _dot(oh_ref[...].astype(BF16), wb_ref[NSA_W + DN_V:MIX_W, :])
    o_ref[...] = x_ref[...] + _dot(m.astype(BF16), wo_ref[...])


def _merge(x, o_a, o_d, o_h, z, w_branch, w_out, tm):
    M, D = x.shape
    row = lambda w: pl.BlockSpec((tm, w), lambda i: (i, 0))
    return pl.pallas_call(
        _merge_kernel,
        grid=(M // tm,),
        in_specs=[
            row(D), row(NSA_W), row(DN_V), row(HG_WV),
            pl.BlockSpec((tm, D), lambda i: (i, 0)),
            pl.BlockSpec((tm, D), lambda i: (i, 1)),
            pl.BlockSpec((tm, D), lambda i: (i, 2)),
            pl.BlockSpec((MIX_W, D), lambda i: (0, 0)),
            pl.BlockSpec((D, D), lambda i: (0, 0)),
        ],
        out_specs=row(D),
        out_shape=jax.ShapeDtypeStruct((M, D), F32),
        compiler_params=_cparams(("parallel",), 40),
        name="merge",
    )(x, o_a, o_d, o_h, z, z, z, w_branch, w_out)


def _cmp_weights(pe, w1, w2):
    n_part = CMP_LEN // CMP_STRIDE
    eye = jnp.eye(NSA_KV, dtype=F32)
    eye2 = jnp.eye(2, dtype=F32)
    w1r = w1.reshape(2, n_part, CMP_STRIDE, HEAD_DIM, CMP_HIDDEN)
    w1e = jnp.einsum("kmrdh,kK,gG->mrkgdKGh", w1r, eye2, eye).reshape(n_part, CHUNK_W, CMP_HID_W)
    w2e = jnp.einsum("khd,kK,gG->kghKGd", w2, eye2, eye).reshape(CMP_HID_W, ROW_W)
    per = pe.reshape(2, n_part, CMP_STRIDE, HEAD_DIM).transpose(1, 2, 0, 3)
    pee = jnp.broadcast_to(per[:, :, :, None, :], (n_part, CMP_STRIDE, 2, NSA_KV, HEAD_DIM)).reshape(n_part, CHUNK_W)
    return pee, w1e.astype(BF16), w2e.astype(BF16)


def _compress(x, xnext_rows, pe_ref, w1_ref, w2_ref, kg_ref, n_seq=1):
    rows = x.shape[0]
    n = rows // n_seq
    p0 = _dot((x + pe_ref[0:1, :]).astype(BF16), w1_ref[0])
    p1 = _dot((x + pe_ref[1:2, :]).astype(BF16), w1_ref[1])
    p1s = pltpu.roll(p1, rows - 1, 0)
    if xnext_rows is not None:
        p1n = _dot((xnext_rows + pe_ref[1:2, :]).astype(BF16), w1_ref[1])
        for s in range(n_seq):
            p1s = jnp.where(_iota((rows, 1), 0) == (s + 1) * n - 1, p1n[s:s + 1, :], p1s)
    comp = _dot(_silu(p0 + p1s).astype(BF16), w2_ref[...])
    cks = []
    for g in range(NSA_KV):
        kc = comp[:, g * HEAD_DIM:(g + 1) * HEAD_DIM]
        cks.append(_rms(kc, kg_ref[...]))
    return cks, comp[:, KV_W:]


def _cmp_prompt_kernel(x_ref, pe_ref, w1_ref, w2_ref, kg_ref, ck_ref, cvt_ref):
    cks, cv = _compress(x_ref[0], None, pe_ref, w1_ref, w2_ref, kg_ref)
    for g in range(NSA_KV):
        ck_ref[0, g] = cks[g].astype(BF16)
    cvt = cv.T
    for g in range(NSA_KV):
        cvt_ref[0, g] = cvt[g * HEAD_DIM:(g + 1) * HEAD_DIM, :].astype(BF16)


def _cmp_prompt(cmp_rows, pee, w1e, w2e, kg0):
    B, T = cmp_rows.shape[:2]
    nc = T // CMP_STRIDE
    x = cmp_rows.reshape(B, nc, CHUNK_W)
    full = lambda s: pl.BlockSpec(s, lambda b: (0,) * len(s))
    return pl.pallas_call(
        _cmp_prompt_kernel,
        grid=(B,),
        in_specs=[pl.BlockSpec((1, nc, CHUNK_W), lambda b: (b, 0, 0)), full(pee.shape), full(w1e.shape),
                  full(w2e.shape), full((1, HEAD_DIM))],
        out_specs=[pl.BlockSpec((1, NSA_KV, nc, HEAD_DIM), lambda b: (b, 0, 0, 0)),
                   pl.BlockSpec((1, NSA_KV, HEAD_DIM, nc), lambda b: (b, 0, 0, 0))],
        out_shape=[jax.ShapeDtypeStruct((B, NSA_KV, nc, HEAD_DIM), BF16),
                   jax.ShapeDtypeStruct((B, NSA_KV, HEAD_DIM, nc), BF16)],
        compiler_params=_cparams(("parallel",), 48),
        name="nsa_compress",
    )(x, pee, w1e, w2e, kg0.reshape(1, HEAD_DIM))


def _softmax_first(s, vt):
    m = jnp.max(s, axis=0, keepdims=True)
    e = jnp.exp(s - m)
    return m, jnp.sum(e, axis=0, keepdims=True), _dot(vt, e.astype(BF16))


KC_W = 2 * LANE - HEAD_DIM
KC_POS = 64
KC_POS_RADIX = 128
KW = HEAD_DIM + KC_W


def _key_features(T):
    k = np.arange(T)
    f = np.zeros((T, KW), np.float32)
    f[k, HEAD_DIM + k // SLC_BLOCK] = 1.0
    f[:, HEAD_DIM + KC_POS] = k // KC_POS_RADIX
    f[:, HEAD_DIM + KC_POS + 1] = k % KC_POS_RADIX
    return f


def _nsa_prompt_kernel(qt_ref, ck_ref, cvt_ref, sk_ref, svt_ref, wk_ref, wvt_ref, gt_ref, ovl_ref, o_ref,
                       *, n_cmp, n_blk):
    qb = pl.program_id(1)
    b0 = qb * Q_BLOCK
    QW = NSA_GROUP * Q_BLOCK
    TK = Q_BLOCK
    lane = _iota((1, QW), 1)
    tq = lane % Q_BLOCK
    q_pos = b0 + tq
    head = lane // Q_BLOCK
    qp1 = b0 + _iota((1, Q_BLOCK), 1)
    bj = _iota((n_blk, 1), 0)
    krel = _iota((TK, 1), 0) - tq
    frow = _iota((KC_W, 1), 0)
    k_diag = pl.multiple_of(b0, TK)
    gt = gt_ref[0, 0]

    def gate(g, branch):
        return jnp.concatenate([gt[(NSA_GROUP * g + j) * 3 + branch:(NSA_GROUP * g + j) * 3 + branch + 1, :]
                                for j in range(NSA_GROUP)], axis=1)

    qc_slcs, qc_wins, o_cmps = [], [], []
    for g in range(NSA_KV):
        slope = _alibi_slope(head, g)
        qgt = jnp.concatenate([qt_ref[0, NSA_GROUP * g + j] for j in range(NSA_GROUP)], axis=1)
        pos_rows = jnp.where(frow == KC_POS, slope * float(KC_POS_RADIX), jnp.where(frow == KC_POS + 1, slope, 0.0))

        dist = q_pos - (_iota((n_cmp, 1), 0) * CMP_STRIDE + (CMP_LEN - 1))
        mask = dist >= 0
        s = jnp.where(mask, _dot(ck_ref[0, g], qgt) - slope * dist.astype(F32), NEG_BIG)
        m = jnp.max(s, axis=0, keepdims=True)
        e = jnp.where(mask, jnp.exp(s - m), 0.0)
        p = (e / jnp.maximum(jnp.sum(e, axis=0, keepdims=True), 1e-30)).astype(BF16)
        o_cmps.append(_dot(cvt_ref[0, g], p))

        imp = _dot(ovl_ref[...], p[:, 0:Q_BLOCK])
        for j in range(1, NSA_GROUP):
            imp += _dot(ovl_ref[...], p[:, j * Q_BLOCK:(j + 1) * Q_BLOCK])
        cur = qp1 // SLC_BLOCK
        forced = (bj == 0) | (bj == cur) | (bj == cur - 1)
        imp = jnp.where(forced, SEL_BIG, jnp.where(bj * SLC_BLOCK <= qp1, imp, -SEL_BIG))
        tiles = [imp[8 * v:8 * v + 8, :] for v in range(n_blk // 8)]
        ranks = [jnp.zeros((8, Q_BLOCK), F32) for _ in tiles]
        for i in range(n_blk):
            row = imp[i:i + 1, :]
            for v, tile in enumerate(tiles):
                if 8 * v > i:
                    beats = row >= tile
                elif 8 * v + 8 <= i:
                    beats = row > tile
                else:
                    beats = (row > tile) | ((row == tile) & (bj[8 * v:8 * v + 8] > i))
                ranks[v] = ranks[v] + jnp.where(beats, 1.0, 0.0)
        rank = jnp.concatenate(ranks, axis=0)
        selb = jnp.where(rank < float(min(N_SEL, n_blk)), 0.0, NEG_BIG)
        selb = jnp.concatenate([selb] * NSA_GROUP, axis=1)
        if n_blk < KC_W:
            selb = jnp.concatenate([selb, jnp.zeros((KC_W - n_blk, QW), F32)], axis=0)
        qc_slcs.append(jnp.concatenate([qgt, (selb + pos_rows).astype(BF16)], axis=0))
        qc_wins.append(jnp.concatenate([qgt, pos_rows.astype(BF16)], axis=0))

    def scores(k_ref, g, k0, n, qcs):
        return _dot(k_ref[0, g, pl.ds(k0, n), :], qcs[g])

    WS = WINDOW + Q_BLOCK
    ws = pl.multiple_of(jnp.maximum(b0 - WINDOW, 0), TK)
    off = b0 - ws
    drel = tq - _iota((WS, 1), 0)
    in_win = (drel >= -off) & (drel < WINDOW - off)
    o_wins = []
    for g in range(NSA_KV):
        s = jnp.where(in_win, scores(wk_ref, g, ws, WS, qc_wins), NEG_BIG)
        _, l, acc = _softmax_first(s, wvt_ref[0, g, :, pl.ds(ws, WS)])
        o_wins.append(acc / jnp.maximum(l, 1e-30))

    carry = []
    for g in range(NSA_KV):
        s = jnp.where(krel <= 0, scores(sk_ref, g, k_diag, TK, qc_slcs), NEG_BIG)
        carry.append(_softmax_first(s, svt_ref[0, g, :, pl.ds(k_diag, TK)]))

    def slc_body(n):
        def body(i, carry):
            k0 = pl.multiple_of(i * n, n)
            G = range(NSA_KV)
            ss = [scores(sk_ref, g, k0, n, qc_slcs) for g in G]
            ms = [jnp.maximum(carry[g][0], jnp.max(ss[g], axis=0, keepdims=True)) for g in G]
            es = [jnp.exp(ss[g] - ms[g]) for g in G]
            pvs = [_dot(svt_ref[0, g, :, pl.ds(k0, n)], es[g].astype(BF16)) for g in G]
            out = []
            for g in G:
                m, l, acc = carry[g]
                alpha = jnp.exp(m - ms[g])
                out.append((ms[g], alpha * l + jnp.sum(es[g], axis=0, keepdims=True), alpha * acc + pvs[g]))
            return tuple(out)
        return body

    carry = tuple(carry)
    done = 0
    for width in (8, 4, 2, 1):
        trips = (qb - done) // width
        carry = lax.fori_loop(done // width, done // width + trips, slc_body(width * TK), carry)
        done = done + trips * width

    outs = []
    for g in range(NSA_KV):
        _, l, acc = carry[g]
        o_slc = acc / jnp.maximum(l, 1e-30)
        outs.append(gate(g, 0) * o_cmps[g] + gate(g, 1) * o_slc + gate(g, 2) * o_wins[g])

    for g in range(NSA_KV):
        for jp in range(NSA_GROUP // 2):
            pair = jnp.concatenate([outs[g][:, (2 * jp) * Q_BLOCK:(2 * jp + 1) * Q_BLOCK],
                                    outs[g][:, (2 * jp + 1) * Q_BLOCK:(2 * jp + 2) * Q_BLOCK]], axis=0)
            c0 = (NSA_GROUP * g + 2 * jp) * HEAD_DIM
            o_ref[0, :, c0:c0 + 2 * HEAD_DIM] = pair.T.astype(o_ref.dtype)


def _overlap(n_cmp, n_blk):
    ci = np.arange(n_cmp)[:, None] * CMP_STRIDE
    bj = np.arange(n_blk)[None, :]
    return ((ci < (bj + 1) * SLC_BLOCK) & (ci + CMP_LEN > bj * SLC_BLOCK)).astype(np.float32)


def _nsa_prompt(qt, ck, cvt, sk, svt, wk, wvt, gates):
    B, _, _, T = qt.shape
    nq = T // Q_BLOCK
    n_cmp = ck.shape[2]
    n_blk = T // SLC_BLOCK
    assert n_blk <= KC_POS and T <= KC_POS_RADIX * 256 and T >= WINDOW + Q_BLOCK
    ovl = jnp.asarray(_overlap(n_cmp, n_blk).T, BF16)
    per_b = lambda s: pl.BlockSpec((1,) + s, lambda b, i: (b,) + (0,) * len(s))
    return pl.pallas_call(
        functools.partial(_nsa_prompt_kernel, n_cmp=n_cmp, n_blk=n_blk),
        grid=(B, nq),
        in_specs=[
            pl.BlockSpec((1, NSA_HEADS, HEAD_DIM, Q_BLOCK), lambda b, i: (b, 0, 0, i)),
            per_b((NSA_KV, n_cmp, HEAD_DIM)), per_b((NSA_KV, HEAD_DIM, n_cmp)),
            per_b((NSA_KV, T, KW)), per_b((NSA_KV, HEAD_DIM, T)),
            per_b((NSA_KV, T, KW)), per_b((NSA_KV, HEAD_DIM, T)),
            pl.BlockSpec((1, 1, GATE_ROWS, Q_BLOCK), lambda b, i: (b, i, 0, 0)),
            pl.BlockSpec((n_blk, n_cmp), lambda b, i: (0, 0)),
        ],
        out_specs=pl.BlockSpec((1, Q_BLOCK, NSA_W), lambda b, i: (b, i, 0)),
        out_shape=jax.ShapeDtypeStruct((B, T, NSA_W), BF16),
        compiler_params=_cparams(("parallel", "arbitrary"), 48),
        name="nsa_prompt",
    )(qt, ck, cvt, sk, svt, wk, wvt, gates, ovl)


GATE_ROWS = 32


def _pair_rms(x, gain2, ones_bd):
    hi, lo = _split2(x * x)
    ms = (_dot(hi, ones_bd) + _dot(lo, ones_bd)) * (1.0 / HEAD_DIM)
    return x * lax.rsqrt(ms + NORM_EPS) * gain2


def _nsa_prep_kernel(zq_ref, zc0_ref, zc1_ref, zs0_ref, zs1_ref, zw0_ref, zw1_ref, zg_ref, qg_ref, kg_ref, kf_ref,
                     qt_ref, cmp_ref, slc_ref, win_ref, sk_ref, svt_ref, wk_ref, wvt_ref, gt_ref):
    lane = _iota((1, LANE), 1)
    ones_bd = jnp.where(_iota((LANE, LANE), 0) // HEAD_DIM == _iota((LANE, LANE), 1) // HEAD_DIM, 1.0, 0.0).astype(BF16)
    for hp in range(NSA_HEADS // 2):
        qn = _pair_rms(zq_ref[0, :, hp * LANE:(hp + 1) * LANE], qg_ref[...], ones_bd) * (HEAD_DIM ** -0.5)
        qnt = qn.T
        qt_ref[0, 2 * hp] = qnt[0:HEAD_DIM].astype(BF16)
        qt_ref[0, 2 * hp + 1] = qnt[HEAD_DIM:].astype(BF16)
    cmp_ref[0, :, 0:KV_W] = zc0_ref[0]
    cmp_ref[0, :, KV_W:] = zc1_ref[0]
    feat = kf_ref[...]
    for i, (zk_ref, zv_ref, rows_ref, ka_ref, vt_ref) in enumerate(
            ((zs0_ref, zs1_ref, slc_ref, sk_ref, svt_ref), (zw0_ref, zw1_ref, win_ref, wk_ref, wvt_ref))):
        k = _pair_rms(zk_ref[0], kg_ref[i:i + 1, :], ones_bd)
        v = zv_ref[0]
        rows_ref[0, :, 0:KV_W] = k
        rows_ref[0, :, KV_W:] = v
        for g in range(NSA_KV):
            kg = k if g == 0 else pltpu.roll(k, HEAD_DIM, 1)
            ka_ref[0, g, :, 0:LANE] = jnp.where(lane < HEAD_DIM, kg.astype(BF16), feat[:, 0:LANE])
            ka_ref[0, g, :, LANE:] = feat[:, LANE:]
        vt = v.T
        for g in range(NSA_KV):
            vt_ref[0, g] = vt[g * HEAD_DIM:(g + 1) * HEAD_DIM].astype(BF16)
    gt_ref[0, 0] = jax.nn.sigmoid(zg_ref[0]).T[0:GATE_ROWS]


def _nsa_prep(z, q_gain, k_gains, offs):
    B, T, _ = z.shape
    nq = T // Q_BLOCK
    assert offs["a_q"] % NSA_W == 0 and offs["a_kv"] % LANE == 0 and offs["a_g"] % LANE == 0
    kv0 = offs["a_kv"] // LANE
    zcol = lambda c: pl.BlockSpec((1, Q_BLOCK, LANE), lambda b, i: (b, i, c))
    full = lambda s: pl.BlockSpec(s, lambda b, i: (0,) * len(s))
    rows = pl.BlockSpec((1, Q_BLOCK, ROW_W), lambda b, i: (b, i, 0))
    keys = pl.BlockSpec((1, NSA_KV, Q_BLOCK, KW), lambda b, i: (b, 0, i, 0))
    vals = pl.BlockSpec((1, NSA_KV, HEAD_DIM, Q_BLOCK), lambda b, i: (b, 0, 0, i))
    two = lambda gain: jnp.concatenate([gain, gain], axis=-1)
    return pl.pallas_call(
        _nsa_prep_kernel,
        grid=(B, nq),
        in_specs=[pl.BlockSpec((1, Q_BLOCK, NSA_W), lambda b, i, c=offs["a_q"] // NSA_W: (b, i, c))]
        + [zcol(kv0 + j) for j in range(6)] + [zcol(offs["a_g"] // LANE)]
        + [full((1, LANE)), full((2, LANE)), pl.BlockSpec((Q_BLOCK, KW), lambda b, i: (i, 0))],
        out_specs=[pl.BlockSpec((1, NSA_HEADS, HEAD_DIM, Q_BLOCK), lambda b, i: (b, 0, 0, i)), rows, rows, rows,
                   keys, vals, keys, vals, pl.BlockSpec((1, 1, GATE_ROWS, Q_BLOCK), lambda b, i: (b, i, 0, 0))],
        out_shape=[jax.ShapeDtypeStruct((B, NSA_HEADS, HEAD_DIM, T), BF16)]
        + [jax.ShapeDtypeStruct((B, T, ROW_W), F32)] * 3
        + [jax.ShapeDtypeStruct((B, NSA_KV, T, KW), BF16), jax.ShapeDtypeStruct((B, NSA_KV, HEAD_DIM, T), BF16)] * 2
        + [jax.ShapeDtypeStruct((B, nq, GATE_ROWS, Q_BLOCK), F32)],
        compiler_params=_cparams(("parallel", "parallel"), 32),
        name="nsa_prep",
    )(z, z, z, z, z, z, z, z, two(q_gain).reshape(1, LANE), two(k_gains[1:3]), jnp.asarray(_key_features(T), BF16))


NSA_DECODE_SEQS = 2

def _nsa_decode_kernel(pt_ref, q_ref, new_ref, gt_ref, pe_ref, w1_ref, w2_ref, kg_ref, ovl_ref, exp_ref, *rest,
                       n_pages, n_seq):
    del pt_ref
    all_cmp_pages = rest[:n_seq * n_pages]
    all_slc_pages = rest[n_seq * n_pages:2 * n_seq * n_pages]
    win_ref, o_ref, rows_ref, x_ref = rest[2 * n_seq * n_pages:2 * n_seq * n_pages + 4]
    n_past = n_pages * PAGE_SIZE
    n_cmp = n_past // CMP_STRIDE
    q_pos = n_past
    halves = ROW_W // LANE
    for p, r in enumerate(all_cmp_pages):
        for c in range(halves):
            rows_ref[c, p * PAGE_SIZE:(p + 1) * PAGE_SIZE, :] = r[0, 0, c * LANE:(c + 1) * LANE, :].T
    for r in range(CMP_STRIDE):
        for c in range(halves):
            x_ref[:, r * ROW_W + c * LANE:r * ROW_W + (c + 1) * LANE] = rows_ref.at[c][
                pl.ds(r, n_seq * n_cmp, stride=CMP_STRIDE), :]
    jrow = _iota((8, 1), 0)
    xnew = jnp.zeros((8, ROW_W), F32)
    for sq in range(n_seq):
        xnew = jnp.where(jrow == sq, new_ref[sq][0:1, 0:ROW_W], xnew)
    xnew = jnp.concatenate([xnew, jnp.zeros((8, CHUNK_W - ROW_W), F32)], axis=1)
    all_cks, all_cv = _compress(x_ref[...], xnew, pe_ref, w1_ref, w2_ref, kg_ref, n_seq)

    lanes = _iota((1, LANE), 1)
    n_blk = (n_past + 1 + SLC_BLOCK - 1) // SLC_BLOCK
    cur = q_pos // SLC_BLOCK
    ii = _iota((LANE, LANE), 0)
    jj = _iota((LANE, LANE), 1)
    n_win = win_ref.shape[3]
    kpos = _iota((1, n_past), 1)
    wpos = (q_pos - n_win) + _iota((1, n_win), 1)
    dist_w = q_pos - wpos
    mask_w = (dist_w < WINDOW) & (wpos >= 0)
    dist_c = q_pos - (lanes * CMP_STRIDE + (CMP_LEN - 1))
    mask_c = dist_c >= 0
    forced = (lanes == 0) | (lanes == cur) | (lanes == cur - 1)

    ch = []
    for sq in range(n_seq):
        for g in range(NSA_KV):
            lo = g * HEAD_DIM
            rows = slice(sq * n_cmp, (sq + 1) * n_cmp)
            ch.append(dict(sq=sq, g=g, lo=lo, hi=lo + HEAD_DIM, qg=q_ref[sq, g], slope=_alibi_slope(jrow, g),
                           ck=all_cks[g][rows].astype(BF16), cv=all_cv[rows, lo:lo + HEAD_DIM].astype(BF16),
                           new=new_ref[sq], pages=all_slc_pages[sq * n_pages:(sq + 1) * n_pages]))

    def new_key(c, row):
        return c["new"][row:row + 1, c["lo"]:c["hi"]], c["new"][row:row + 1, KV_W + c["lo"]:KV_W + c["hi"]]

    for c in ch:
        c["s"] = jnp.where(mask_c, _dot_nt(c["qg"], c["ck"]) - c["slope"] * dist_c.astype(F32), NEG_BIG)
    for c in ch:
        e = jnp.where(mask_c, jnp.exp(c["s"] - jnp.max(c["s"], axis=1, keepdims=True)), 0.0)
        c["p"] = (e / jnp.maximum(jnp.sum(e, axis=1, keepdims=True), 1e-30)).astype(BF16)
    for c in ch:
        c["o_cmp"] = _dot(c["p"], c["cv"])
        c["imp"] = _dot(c["p"], ovl_ref[...])
    for c in ch:
        imp = jnp.sum(jnp.where(jrow < NSA_GROUP, c["imp"], 0.0), axis=0, keepdims=True)
        imp = jnp.where(forced, SEL_BIG, jnp.where(lanes * SLC_BLOCK <= q_pos, imp, -SEL_BIG))
        imp = jnp.where(lanes < n_blk, imp, -3e38)
        impr = jnp.broadcast_to(imp, (LANE, LANE))
        impc = jnp.sum(jnp.where(ii == jj, impr, 0.0), axis=1, keepdims=True)
        beats = jnp.where(impc > impr, 1.0, jnp.where((impc == impr) & (ii < jj), 1.0, 0.0))
        rank = jnp.sum(beats, axis=0, keepdims=True)
        c["sel"] = jnp.where(rank < float(min(N_SEL, n_blk)), 1.0, 0.0)
    for c in ch:
        c["selk"] = _dot(jnp.broadcast_to(c["sel"], (8, LANE)).astype(BF16), exp_ref[...])

    def attend(scores, vts, masks, dists, new_rows, new_masks):
        outs = []
        ss, s_news = [], []
        for c, s, mask, dist, row, mask_new in zip(ch, scores, masks, dists, new_rows, new_masks):
            ss.append(jnp.where(mask, s - c["slope"] * dist.astype(F32), NEG_BIG))
            knew, _ = new_key(c, row)
            s_new = jnp.sum(c["qg"].astype(F32) * knew.astype(BF16).astype(F32), axis=1, keepdims=True)
            s_news.append(s_new if mask_new is None else jnp.where(mask_new, s_new, NEG_BIG))
        ms = [jnp.maximum(jnp.max(s, axis=1, keepdims=True), s_new) for s, s_new in zip(ss, s_news)]
        es = [jnp.where(mask, jnp.exp(s - m), 0.0) for s, m, mask in zip(ss, ms, masks)]
        pvs = [_dot_nt(e.astype(BF16), vt) for e, vt in zip(es, vts)]
        for c, e, m, s_new, pv, row, mask_new in zip(ch, es, ms, s_news, pvs, new_rows, new_masks):
            e_new = jnp.exp(s_new - m)
            if mask_new is not None:
                e_new = jnp.where(mask_new, e_new, 0.0)
            _, vnew = new_key(c, row)
            l = jnp.sum(e, axis=1, keepdims=True) + e_new
            o = pv + e_new.astype(BF16).astype(F32) * vnew.astype(BF16).astype(F32)
            outs.append(o / jnp.maximum(l, 1e-30))
        return outs

    def paged(c, off):
        return jnp.concatenate([r[0, 0, off + c["lo"]:off + c["hi"], :] for r in c["pages"]], axis=1).astype(BF16)

    n_ch = len(ch)
    o_slc = attend([_dot(c["qg"], paged(c, 0)) for c in ch], [paged(c, KV_W) for c in ch],
                   [c["selk"] > 0.5 for c in ch], [q_pos - kpos] * n_ch, [1] * n_ch,
                   [c["sel"][:, cur:cur + 1] > 0.5 for c in ch])
    o_win = attend([_dot(c["qg"], win_ref[0, c["sq"], c["lo"]:c["hi"], :].astype(BF16)) for c in ch],
                   [win_ref[0, c["sq"], KV_W + c["lo"]:KV_W + c["hi"], :].astype(BF16) for c in ch],
                   [mask_w] * n_ch, [dist_w] * n_ch, [2] * n_ch, [None] * n_ch)
    for c, o_s, o_w in zip(ch, o_slc, o_win):
        gt = gt_ref[c["sq"], c["g"]]
        o_ref[c["sq"], c["g"]] = gt[:, 0:1] * c["o_cmp"] + gt[:, 1:2] * o_s + gt[:, 2:3] * o_w


def _rows_minor(cache):
    lead = cache.ndim - 4
    perm = tuple(range(lead)) + (lead + 1, lead + 2, lead + 3, lead)
    return cache.transpose(perm).reshape(cache.shape[:lead] + (ROW_W, cache.shape[lead]))


def _nsa_decode(page_table, l, q, new_rows, gates, cmp_t, slc_t, win_t, pee, w1e, w2e, kg0):
    B, n_pages = page_table.shape
    n_past = n_pages * PAGE_SIZE
    n_cmp = n_past // CMP_STRIDE
    n_win = win_t.shape[3]
    assert n_cmp == LANE and n_win <= n_past
    n_blk = (n_past + 1 + SLC_BLOCK - 1) // SLC_BLOCK
    ovl = np.zeros((n_cmp, LANE), np.float32)
    ovl[:, :n_blk] = _overlap(n_cmp, n_blk)
    expand = (np.arange(n_past)[None, :] // SLC_BLOCK == np.arange(LANE)[:, None]).astype(np.float32)
    S = NSA_DECODE_SEQS
    assert B % S == 0
    full = lambda s: pl.BlockSpec(s, lambda i, pt: (0,) * len(s))
    per_b = lambda s: pl.BlockSpec((S,) + s, lambda i, pt: (i,) + (0,) * len(s))
    page = lambda sq, p: pl.BlockSpec((1, 1, ROW_W, PAGE_SIZE), lambda i, pt: (l, pt[S * i + sq, p], 0, 0))
    pages = [page(sq, p) for sq in range(S) for p in range(n_pages)]

    in_specs = [per_b((NSA_KV, 8, HEAD_DIM)), per_b((8, ROW_W)), per_b((NSA_KV, 8, LANE)), full(pee.shape),
                full(w1e.shape), full(w2e.shape), full((1, HEAD_DIM)), full(ovl.shape), full(expand.shape)]
    in_specs += pages * 2
    in_specs += [pl.BlockSpec((1, S, ROW_W, n_win), lambda i, pt: (l, i, 0, 0))]
    return pl.pallas_call(
        functools.partial(_nsa_decode_kernel, n_pages=n_pages, n_seq=S),
        grid_spec=pltpu.PrefetchScalarGridSpec(
            num_scalar_prefetch=1, grid=(B // S,), in_specs=in_specs,
            out_specs=pl.BlockSpec((S, NSA_KV, 8, HEAD_DIM), lambda i, pt: (i, 0, 0, 0)),
            scratch_shapes=[pltpu.VMEM((ROW_W // LANE, S * n_past, LANE), F32),
                            pltpu.VMEM((S * n_cmp, CHUNK_W), F32)]),
        out_shape=jax.ShapeDtypeStruct((B, NSA_KV, 8, HEAD_DIM), F32),
        compiler_params=_cparams(("arbitrary",), 56),
        name="nsa_decode",
    )(page_table, q, new_rows, gates, pee, w1e, w2e, kg0.reshape(1, HEAD_DIM), jnp.asarray(ovl, BF16),
      jnp.asarray(expand, BF16), *([cmp_t] * (S * n_pages)), *([slc_t] * (S * n_pages)), win_t)


DN_TB = 2 * DN_CHUNK
HALO = 8


def _softplus(x):
    return jnp.maximum(x, 0.0) + jnp.log(1.0 + jnp.exp(-jnp.abs(x)))


def _dn_prompt_kernel(zq_ref, zk_ref, zv_ref, zz_ref, zs_ref, cw_ref, a_ref, dtb_ref, gain_ref, o_ref, s_out_ref,
                      s_ref, buf_ref, *, n_step, col_b, col_a):
    t = pl.program_id(1)
    TB, C = DN_TB, DN_CHUNK

    @pl.when(t == 0)
    def _():
        s_ref[...] = jnp.zeros_like(s_ref)
        buf_ref[0:HALO, :] = jnp.zeros((HALO, DN_QKV), F32)

    @pl.when(t > 0)
    def _():
        buf_ref[0:HALO, :] = buf_ref[TB:TB + HALO, :]

    buf_ref[HALO:HALO + TB, 0:DN_QK] = zq_ref[0]
    buf_ref[HALO:HALO + TB, DN_QK:2 * DN_QK] = zk_ref[0]
    buf_ref[HALO:HALO + TB, 2 * DN_QK:] = zv_ref[0]
    cw = cw_ref[...]
    first = HALO - (CONV_W - 1)
    y = buf_ref[pl.ds(first, TB), :] * cw[0:1, :]
    for j in range(1, CONV_W):
        y += buf_ref[pl.ds(first + j, TB), :] * cw[j:j + 1, :]
    act = _silu(y)

    zs = zs_ref[0]
    beta_all = jax.nn.sigmoid(zs)
    g_all = -a_ref[...] * _softplus(zs + dtb_ref[...])
    zz = zz_ref[0]
    gain = gain_ref[...]

    ii = _iota((C, C), 0)
    jj = _iota((C, C), 1)
    incl = ii >= jj
    eye = (ii == jj).astype(F32)
    n_sub = TB // C
    ch = []
    for sc in range(n_sub):
        rows = slice(sc * C, (sc + 1) * C)
        for h in range(DN_HEADS):
            cs = slice(h * DN_DK, (h + 1) * DN_DK)
            q = _l2n(act[rows, cs]) * (DN_DK ** -0.5)
            k = _l2n(act[rows, DN_QK + h * DN_DK:DN_QK + (h + 1) * DN_DK])
            v = act[rows, 2 * DN_QK + h * DN_DV:2 * DN_QK + (h + 1) * DN_DV]
            gcol = g_all[rows, col_a + h:col_a + h + 1]
            bcol = beta_all[rows, col_b + h:col_b + h + 1]
            grow = jnp.sum(jnp.where(ii <= jj, gcol, 0.0), axis=0, keepdims=True)
            gcum = jnp.sum(jnp.where(ii == jj, grow, 0.0), axis=1, keepdims=True)
            decay = jnp.where(incl, jnp.exp(jnp.where(incl, gcum - grow, 0.0)), 0.0)
            kb = k.astype(BF16)
            ch.append(dict(q=q, k=k, v=v, kb=kb, bcol=bcol, gcum=gcum, decay=decay, sc=sc, h=h, rows=rows, cs=cs))
    for c in ch:
        c["npow"] = -jnp.where(ii > jj, c["bcol"] * _dot_nt(c["kb"], c["kb"]) * c["decay"], 0.0)
        c["tinv"] = eye + c["npow"]
    for _ in range(int(np.log2(C)) - 1):
        for c in ch:
            nb = c["npow"].astype(BF16)
            c["npow"] = _dot(nb, nb)
        for c in ch:
            c["tinv"] = c["tinv"] + _dot(c["tinv"].astype(BF16), c["npow"].astype(BF16))
    for c in ch:
        tb = c["tinv"].astype(BF16)
        c["eg"] = jnp.exp(c["gcum"])
        c["u"] = _dot(tb, (c["v"] * c["bcol"]).astype(BF16))
        c["w"] = _dot(tb, (c["k"] * (c["bcol"] * c["eg"])).astype(BF16)).astype(BF16)
        c["attn"] = (_dot_nt(c["q"].astype(BF16), c["kb"]) * c["decay"]).astype(BF16)
    for sc in range(n_sub):
        cur = [c for c in ch if c["sc"] == sc]
        Ss = [s_ref[c["h"]] for c in cur]
        Sbs = [S.astype(BF16) for S in Ss]
        vns = [c["u"] - _dot(c["w"], Sb) for c, Sb in zip(cur, Sbs)]
        for c, S, Sb, v_new in zip(cur, Ss, Sbs, vns):
            vnb = v_new.astype(BF16)
            o = _dot((c["q"] * c["eg"]).astype(BF16), Sb) + _dot(c["attn"], vnb)
            g_last = c["gcum"][C - 1:C, :]
            s_ref[c["h"]] = S * jnp.exp(g_last) + _dot_tn((c["k"] * jnp.exp(g_last - c["gcum"])).astype(BF16), vnb)
            o_ref[0, c["rows"], c["cs"]] = (_rms(o, gain) * _silu(zz[c["rows"], c["cs"]])).astype(o_ref.dtype)

    @pl.when(t == n_step - 1)
    def _():
        s_out_ref[0] = s_ref[...]


def _dn_prompt(z, conv_w, a_log, dt_bias, out_gain, offs):
    B, T, _ = z.shape
    n_step = T // DN_TB
    wide = lambda name, k=0: pl.BlockSpec((1, DN_TB, DN_QK), lambda b, t, c=offs[name] // DN_QK + k: (b, t, c))
    assert offs["d_qkv"] % DN_QK == 0 and offs["d_z"] % DN_V == 0 and offs["a_g"] % LANE == 0
    col_b, col_a = offs["d_b"] - offs["a_g"], offs["d_a"] - offs["a_g"]
    lanes = lambda vals, col: jnp.zeros((1, LANE), F32).at[0, col:col + DN_HEADS].set(vals)
    full = lambda s: pl.BlockSpec(s, lambda b, t: (0,) * len(s))
    return pl.pallas_call(
        functools.partial(_dn_prompt_kernel, n_step=n_step, col_b=col_b, col_a=col_a),
        grid=(B, n_step),
        in_specs=[wide("d_qkv", 0), wide("d_qkv", 1), wide("d_qkv", 2), wide("d_z"),
                  pl.BlockSpec((1, DN_TB, LANE), lambda b, t, c=offs["a_g"] // LANE: (b, t, c)),
                  full((CONV_W, DN_QKV)), full((1, LANE)), full((1, LANE)), full((1, DN_DV))],
        out_specs=[pl.BlockSpec((1, DN_TB, DN_V), lambda b, t: (b, t, 0)),
                   pl.BlockSpec((1, DN_HEADS, DN_DK, DN_DV), lambda b, t: (b, 0, 0, 0))],
        out_shape=[jax.ShapeDtypeStruct((B, T, DN_V), BF16), jax.ShapeDtypeStruct((B, DN_HEADS, DN_DK, DN_DV), F32)],
        scratch_shapes=[pltpu.VMEM((DN_HEADS, DN_DK, DN_DV), F32), pltpu.VMEM((HALO + DN_TB, DN_QKV), F32)],
        compiler_params=_cparams(("parallel", "arbitrary"), 32),
        name="dn_prompt",
    )(z, z, z, z, z, conv_w, lanes(jnp.exp(a_log), col_a), lanes(dt_bias, col_a), out_gain.reshape(1, DN_DV))


def _row0(x):
    return jnp.where(_iota(x.shape, 0) == 0, x, 0.0)


def _dn_decode_kernel(vec_ref, s_ref, o_ref, s_out_ref, *, bb):
    def body(b, _):
        for h in range(DN_HEADS):
            x = vec_ref[b, h]
            S = s_ref[b, h]
            xs = _dot(x.astype(BF16), S.astype(BF16))
            k, q, v, eg, beta = x[0:1], x[1:2], x[2:3], x[3:4], x[4:5]
            v_new = beta * (v - eg * xs[0:1])
            kb = k.astype(BF16).astype(F32)
            qk = jnp.sum(q.astype(BF16).astype(F32) * kb, axis=1, keepdims=True)
            o = eg * xs[1:2] + qk.astype(BF16).astype(F32) * v_new.astype(BF16).astype(F32)
            o_ref[b, h] = jnp.broadcast_to(o, (8, DN_DV))
            s_out_ref[b, h] = S * eg[:, 0:1] + _dot_tn(_row0(x).astype(BF16),
                                                      _row0(jnp.broadcast_to(v_new, (8, DN_DV))).astype(BF16))
        return 0

    lax.fori_loop(0, bb, body, 0)


def _state_step(kernel_fn, name, vec, S_all, l, bb=8):
    B, H = vec.shape[:2]
    spec_v = pl.BlockSpec((bb, H, 8, vec.shape[3]), lambda i: (i, 0, 0, 0))
    spec_s = pl.BlockSpec((bb, H) + S_all.shape[3:], lambda i: (i, 0, 0, 0))
    spec_sl = pl.BlockSpec((None, bb, H) + S_all.shape[3:], lambda i: (l, i, 0, 0, 0))
    return pl.pallas_call(
        functools.partial(kernel_fn, bb=bb),
        grid=(B // bb,),
        in_specs=[spec_v, spec_sl],
        out_specs=[spec_v, spec_s],
        out_shape=[jax.ShapeDtypeStruct(vec.shape, F32), jax.ShapeDtypeStruct(S_all.shape[1:], F32)],
        compiler_params=_cparams(("parallel",), 32),
        name=name,
    )(vec, S_all)


def _dn_decode(vec, S_all, l):
    return _state_step(_dn_decode_kernel, "dn_decode", vec, S_all, l)


HG_TB = 128
HG_LEVELS = tuple(HG_TB >> (i + 1) for i in range(int(np.log2(HG_TB))))


def _hg_tables():
    r = np.arange(HG_TB)
    tril = (r[:, None] >= r[None, :]).astype(np.float32)
    mats = [tril] + [tril[(r // (2 * w)) * 2 * w + w - 1] for w in HG_LEVELS]
    level = np.full((HG_TB, HG_TB), -1, np.int32)
    for i, w in enumerate(HG_LEVELS):
        same_block = r[:, None] // (2 * w) == r[None, :] // (2 * w)
        split = (r[:, None] % (2 * w) >= w) & (r[None, :] % (2 * w) < w)
        level[same_block & split] = i
    return np.concatenate(mats, axis=0), level


def _hg_prompt_kernel(zf_ref, zi_ref, zq_ref, zo_ref, lb_ref, gain_ref, sel_ref, lvl_ref, o_ref, s_out_ref, st_ref,
                      *, n_step):
    t = pl.program_id(1)
    TB = HG_TB

    @pl.when(t == 0)
    def _():
        st_ref[...] = jnp.zeros_like(st_ref)

    zf = zf_ref[0]
    lb = lb_ref[...]
    lf = jnp.log(lb + (1.0 - lb) * jax.nn.sigmoid(zf))
    k = (1.0 - lb) * jax.nn.sigmoid(-zf)
    q = _silu(zq_ref[0]) * (HG_DK ** -0.5)
    vb = zi_ref[0].astype(BF16)
    gain = gain_ref[...]
    hi = lf.astype(BF16)
    r1 = lf - hi.astype(F32)
    mid = r1.astype(BF16)
    lo = (r1 - mid.astype(F32)).astype(BF16)
    sel = sel_ref[...]
    gg = _dot(sel, hi) + (_dot(sel, mid) + _dot(sel, lo))
    G = gg[0:TB]
    row = _iota((TB, 1), 0)
    qts, kts = [], []
    for i, w in enumerate(HG_LEVELS):
        d = G - gg[(i + 1) * TB:(i + 2) * TB]
        right = (row % (2 * w)) >= w
        qts.append(jnp.where(right, q * jnp.exp(jnp.minimum(d, 0.0)), 0.0).astype(BF16))
        kts.append(jnp.where(right, 0.0, k * jnp.exp(jnp.minimum(-d, 0.0))).astype(BF16))
    lvl = lvl_ref[...]
    eye = _iota((TB, TB), 0) == _iota((TB, TB), 1)
    qe = (q * jnp.exp(G)).astype(BF16)
    g_last = G[TB - 1:TB, :]
    kd = (k * jnp.exp(g_last - G)).astype(BF16)
    eg_last = jnp.exp(g_last)
    qk = q * k
    heads = [slice(h * HG_DK, (h + 1) * HG_DK) for h in range(HG_HEADS)]
    sts = [st_ref[h] for h in range(HG_HEADS)]
    o_inter = [_dot_nt(qe[:, cs], st.astype(BF16)) for cs, st in zip(heads, sts)]
    kv = [_dot_tn(vb[:, cs], kd[:, cs]) for cs in heads]
    a = [jnp.where(eye, jnp.sum(qk[:, cs], axis=1, keepdims=True), 0.0) for cs in heads]
    for i in range(len(HG_LEVELS)):
        parts = [_dot_nt(qts[i][:, cs], kts[i][:, cs]) for cs in heads]
        a = [a_h + jnp.where(lvl == i, p, 0.0) for a_h, p in zip(a, parts)]
    o_intra = [_dot(a_h.astype(BF16), vb[:, cs]) for a_h, cs in zip(a, heads)]
    for h, cs in enumerate(heads):
        o = o_inter[h] + o_intra[h]
        o_ref[0, :, cs] = (_rms(o, gain) * jax.nn.sigmoid(zo_ref[0, :, cs])).astype(o_ref.dtype)
        st_ref[h] = sts[h] * eg_last[:, cs] + kv[h]

    @pl.when(t == n_step - 1)
    def _():
        for h in range(HG_HEADS):
            s_out_ref[0, h] = st_ref[h].T


def _hg_prompt(z, lb, out_gain, offs):
    B, T, _ = z.shape
    tb = HG_TB
    n_step = T // tb
    sel, level = _hg_tables()
    assert all(offs[n] % HG_WK == 0 for n in ("r_f", "r_i", "r_q", "r_og"))
    col = lambda name: pl.BlockSpec((1, tb, HG_WK), lambda b, t, c=offs[name] // HG_WK: (b, t, c))
    full = lambda s: pl.BlockSpec(s, lambda b, t: (0,) * len(s))
    return pl.pallas_call(
        functools.partial(_hg_prompt_kernel, n_step=n_step),
        grid=(B, n_step),
        in_specs=[col("r_f"), col("r_i"), col("r_q"), col("r_og"), full((1, HG_WK)), full((1, HG_DV)),
                  full(sel.shape), full(level.shape)],
        out_specs=[pl.BlockSpec((1, tb, HG_WV), lambda b, t: (b, t, 0)),
                   pl.BlockSpec((1, HG_HEADS, HG_DK, HG_DV), lambda b, t: (b, 0, 0, 0))],
        out_shape=[jax.ShapeDtypeStruct((B, T, HG_WV), BF16), jax.ShapeDtypeStruct((B, HG_HEADS, HG_DK, HG_DV), F32)],
        scratch_shapes=[pltpu.VMEM((HG_HEADS, HG_DV, HG_DK), F32)],
        compiler_params=_cparams(("parallel", "arbitrary"), 32),
        name="hg_prompt",
    )(z, z, z, z, lb.reshape(1, HG_WK), out_gain.reshape(1, HG_DV), jnp.asarray(sel, BF16), jnp.asarray(level))


def _hg_decode_kernel(vec_ref, s_ref, o_ref, s_out_ref, *, bb):
    ii = _iota((HG_DK, HG_DK), 0)
    jj = _iota((HG_DK, HG_DK), 1)

    def body(b, _):
        for h in range(HG_HEADS):
            x = vec_ref[b, h]
            S = s_ref[b, h]
            k, q, v, lf = x[0:1], x[1:2], x[2:3], x[3:4]
            f = jnp.exp(lf)
            qs = _dot(jnp.broadcast_to(q * f, (8, HG_DK)).astype(BF16), S.astype(BF16))
            a = jnp.sum(q * k, axis=1, keepdims=True)
            o = qs[0:1] + a.astype(BF16).astype(F32) * v.astype(BF16).astype(F32)
            o_ref[b, h] = jnp.broadcast_to(o, (8, HG_DV))
            fcol = jnp.sum(jnp.where(ii == jj, f, 0.0), axis=1, keepdims=True)
            s_out_ref[b, h] = S * fcol + _dot_tn(_row0(x).astype(BF16),
                                                 _row0(jnp.broadcast_to(v, (8, HG_DV))).astype(BF16))
        return 0

    lax.fori_loop(0, bb, body, 0)


def _hg_decode(vec, S_all, l):
    return _state_step(_hg_decode_kernel, "hg_decode", vec, S_all, l)


def _head_rms(x, g):
    return x * lax.rsqrt(jnp.mean(x * x, axis=-1, keepdims=True) + NORM_EPS) * g


def _l2n(x):
    return x * lax.rsqrt(jnp.sum(x * x, axis=-1, keepdims=True) + L2_EPS)


def _rows8(rows):
    x = jnp.stack(rows, axis=-2)
    pad = [(0, 0)] * x.ndim
    pad[-2] = (0, 8 - len(rows))
    return jnp.pad(x, pad)


def _layer(x, l, prm, wts, past, page_table):
    B, T, D = x.shape
    M = B * T
    tm = min(M, ROW_TILE)
    tm_w = min(M, ROW_TILE_STREAMED)
    x2 = x.reshape(M, D)
    x2 = _ffn(x2, prm["ffn1_norm"][l], wts["ffn1_w_gu"][l], wts["ffn1_w_down"][l], tm_w)
    z, mgates = _inproj(x2, prm["mix_norm"][l], wts["w_in"][l], min(M, ROW_TILE_INPROJ))
    sizes, _, offs, _ = _z_layout()
    pee, w1e, w2e = wts["cmp"][l]
    kg = prm["nsa_k_norm"][l]
    p = jax.nn.softmax(prm["hg_lb_logits"], axis=0)
    lb = (jnp.cumsum(p, axis=0) - p[0])[l]
    if past is None:
        z3 = z.reshape(B, T, -1)
        qt, cmp_new, slc_new, win_new, sk, svt, wk, wvt, gt = _nsa_prep(z3, prm["nsa_q_norm"][l], kg, offs)
        ck, cvt = _cmp_prompt(cmp_new, pee, w1e, w2e, kg[0])
        o_a = _nsa_prompt(qt, ck, cvt, sk, svt, wk, wvt, gt)
        o_d, dn_state = _dn_prompt(z3, prm["dn_conv_w"][l], prm["dn_A_log"][l], prm["dn_dt_bias"][l],
                                   prm["dn_out_norm"][l], offs)
        o_h, hg_state = _hg_prompt(z3, lb, prm["hg_out_norm"][l], offs)
        rows = lambda a: a.reshape(B, -1, 2, NSA_KV, HEAD_DIM)
        cmp_new, slc_new, win_state = rows(cmp_new), rows(slc_new), rows(win_new[:, T - min(WINDOW, T):])
        conv_state = z3[:, T - (CONV_W - 1):, offs["d_qkv"]:offs["d_qkv"] + DN_QKV]
    else:
        zs = {n: z[:, offs[n]:offs[n] + sizes[n]].reshape(B, T, sizes[n]) for n in Z_ORDER if n != "m_g"}
        (o_a, o_d, o_h), (cmp_new, slc_new, win_state, conv_state, dn_state, hg_state) = _decode_mixers(
            zs, l, prm, (pee, w1e, w2e), lb, past, page_table)

    x2 = _merge(x2, o_a.reshape(M, NSA_W), o_d.reshape(M, DN_V), o_h.reshape(M, HG_WV), mgates, wts["w_branch"][l],
                wts["w_out"][l], tm)
    x2 = _ffn(x2, prm["ffn2_norm"][l], wts["ffn2_w_gu"][l], wts["ffn2_w_down"][l], tm_w)
    return x2.reshape(B, T, D), (cmp_new, slc_new, win_state, conv_state, dn_state, hg_state)


def _decode_mixers(zs, l, prm, cmp_w, lb, past, page_table):
    B, T = zs["a_q"].shape[:2]
    pee, w1e, w2e = cmp_w
    kg = prm["nsa_k_norm"][l]
    q = _head_rms(zs["a_q"].reshape(B, T, NSA_HEADS, HEAD_DIM), prm["nsa_q_norm"][l]) * (HEAD_DIM ** -0.5)
    kv = zs["a_kv"].reshape(B, T, 3, 2, NSA_KV, HEAD_DIM)
    cmp_new = kv[:, :, 0]
    slc_new = jnp.stack([_head_rms(kv[:, :, 1, 0], kg[1]), kv[:, :, 1, 1]], axis=2)
    win_new = jnp.stack([_head_rms(kv[:, :, 2, 0], kg[2]), kv[:, :, 2, 1]], axis=2)
    gates = jax.nn.sigmoid(zs["a_g"].reshape(B, T, NSA_HEADS, 3))
    qd = q.reshape(B, NSA_KV, NSA_GROUP, HEAD_DIM)
    qd = jnp.pad(qd, ((0, 0), (0, 0), (0, 8 - NSA_GROUP), (0, 0))).astype(BF16)
    new_rows = _rows8([cmp_new.reshape(B, ROW_W), slc_new.reshape(B, ROW_W), win_new.reshape(B, ROW_W)])
    gd = gates.reshape(B, NSA_KV, NSA_GROUP, 3)
    gd = jnp.pad(gd, ((0, 0), (0, 0), (0, 8 - NSA_GROUP), (0, LANE - 3)))
    o8 = _nsa_decode(page_table, l, qd, new_rows, gd, past["cmp"], past["slc"], past["win"], pee, w1e, w2e, kg[0])
    o_a = o8[:, :, :NSA_GROUP].reshape(B, T, NSA_W)

    d_qkv = zs["d_qkv"]
    xx = jnp.concatenate([past["conv"][l], d_qkv], axis=1)
    cw = prm["dn_conv_w"][l]
    qkv = sum(xx[:, j:j + T] * cw[j] for j in range(CONV_W))
    conv_state = xx[:, -(CONV_W - 1):]
    dq, dk, dv = jnp.split(jax.nn.silu(qkv), [DN_QK, 2 * DN_QK], axis=-1)
    dq = _l2n(dq.reshape(B, T, DN_HEADS, DN_DK)) * (DN_DK ** -0.5)
    dk = _l2n(dk.reshape(B, T, DN_HEADS, DN_DK))
    beta = jax.nn.sigmoid(zs["d_b"])
    g = -jnp.exp(prm["dn_A_log"][l]) * jax.nn.softplus(zs["d_a"] + prm["dn_dt_bias"][l])
    lanes = lambda a: jnp.broadcast_to(a[:, 0, :, None], (B, DN_HEADS, DN_DK))
    vec = _rows8([dk[:, 0], dq[:, 0], dv.reshape(B, DN_HEADS, DN_DV), lanes(jnp.exp(g)), lanes(beta)])
    o8, dn_state = _dn_decode(vec, past["dn_S"], l)
    o_d = o8[:, :, 0].reshape(B, T, DN_V)
    o_d = _head_rms(o_d.reshape(B, T, DN_HEADS, DN_DV), prm["dn_out_norm"][l]) * jax.nn.silu(
        zs["d_z"].reshape(B, T, DN_HEADS, DN_DV))

    zf = zs["r_f"]
    logf = jnp.log(lb + (1.0 - lb) * jax.nn.sigmoid(zf))
    k_in = (1.0 - lb) * jax.nn.sigmoid(-zf)
    hq = jax.nn.silu(zs["r_q"]) * (HG_DK ** -0.5)
    hd = lambda a: a.reshape(B, HG_HEADS, HG_DK)
    vec = _rows8([hd(k_in), hd(hq), hd(zs["r_i"]), hd(logf)])
    o8, hg_state = _hg_decode(vec, past["hg_S"], l)
    o_h = o8[:, :, 0].reshape(B, T, HG_WV)
    o_h = _head_rms(o_h.reshape(B, T, HG_HEADS, HG_DV), prm["hg_out_norm"][l]) * jax.nn.sigmoid(
        zs["r_og"].reshape(B, T, HG_HEADS, HG_DV))
    return (o_a, o_d, o_h), (cmp_new, slc_new, win_new, conv_state, dn_state, hg_state)


def _trunk(x, prm, wts, caches, page_table):
    new = []
    for l in range(DEPTH):
        x, st = _layer(x, l, prm, wts, caches, page_table)
        new.append(st)
    return x, [jnp.stack([s[i] for s in new], axis=0) for i in range(6)]


def kernel(x_prompt, x_sample, cache_cmp_kv, cache_slc_kv, cache_win_kv, state_dn_conv, state_dn_S, state_hg_S,
           page_table, ffn1_norm, ffn1_w_gu, ffn1_w_down, mix_norm, w_in, nsa_q_norm, nsa_k_norm, nsa_cmp_pe,
           nsa_cmp_w1, nsa_cmp_w2, dn_conv_w, dn_A_log, dn_dt_bias, dn_out_norm, hg_lb_logits, hg_out_norm,
           w_branch, w_out, ffn2_norm, ffn2_w_gu, ffn2_w_down):
    prm = dict(ffn1_norm=ffn1_norm, mix_norm=mix_norm, nsa_q_norm=nsa_q_norm, nsa_k_norm=nsa_k_norm,
               dn_conv_w=dn_conv_w, dn_A_log=dn_A_log, dn_dt_bias=dn_dt_bias, dn_out_norm=dn_out_norm,
               hg_lb_logits=hg_lb_logits, hg_out_norm=hg_out_norm, ffn2_norm=ffn2_norm)
    bf = lambda w: w.astype(BF16)
    wts = dict(ffn1_w_gu=bf(ffn1_w_gu), ffn1_w_down=bf(ffn1_w_down), ffn2_w_gu=bf(ffn2_w_gu),
               ffn2_w_down=bf(ffn2_w_down), w_branch=bf(w_branch), w_out=bf(w_out),
               w_in=jnp.stack([_permute_w_in(w_in[l]) for l in range(DEPTH)]),
               cmp=[_cmp_weights(nsa_cmp_pe[l], nsa_cmp_w1[l], nsa_cmp_w2[l]) for l in range(DEPTH)])
    y_p, (p_cmp, p_slc, p_win, p_conv, p_dn, p_hg) = _trunk(x_prompt, prm, wts, None, None)
    caches = dict(cmp=_rows_minor(cache_cmp_kv), slc=_rows_minor(cache_slc_kv), win=_rows_minor(cache_win_kv),
                  conv=state_dn_conv, dn_S=state_dn_S, hg_S=state_hg_S)
    y_s, (s_cmp, s_slc, s_win, s_conv, s_dn, s_hg) = _trunk(x_sample, prm, wts, caches, page_table)
    return (y_p, y_s, p_cmp, s_cmp, p_slc, s_slc, p_win, s_win, p_conv, s_conv, p_dn, s_dn, p_hg, s_hg)
```
